```python
import math
import jax, jax.numpy as jnp
from jax import lax
import numpy as np

D_MODEL = 1024
BATCH = 2
SEQ = 8192
DEPTH = 1
DEC_BATCH = 32
DEC_SEQ = 8
PAST_LEN = 8192
PAGE_SIZE = 128

ATTN_WIDTH = D_MODEL // 2
HEAD_DIM = 64
N_HEADS_A = ATTN_WIDTH // (2 * HEAD_DIM)
N_MAPS = 2 * N_HEADS_A
GMLP_WIDTH = D_MODEL - ATTN_WIDTH
N_GROUPS_B = 4
GROUP_B = GMLP_WIDTH // N_GROUPS_B
CHUNK = 128
IN_WIDTH = 3 * ATTN_WIDTH + 2 * GMLP_WIDTH
D_FF = 2816
N_BUCKETS = 32
MAX_DISTANCE = 128
Q_BLOCK = 128
EPS = 1e-6
NEG = -1e30

kernel_name = "hybrid_diffattn_gmlp_macaron_step"


def rmsnorm(x, g):
    xf = x.astype(jnp.float32)
    y = xf * lax.rsqrt(jnp.mean(xf * xf, axis=-1, keepdims=True) + EPS)
    return (y * g.astype(jnp.float32)).astype(x.dtype)


def swiglu(h, w_gu, w_down):
    gate, up = jnp.split(h @ w_gu, 2, axis=-1)
    return (jax.nn.silu(gate) * up) @ w_down


def rel_bucket(d):
    n = jnp.maximum(d, 0)
    max_exact = N_BUCKETS // 2
    nf = jnp.maximum(n, 1).astype(jnp.float32)
    large = max_exact + (jnp.log(nf / max_exact) / math.log(MAX_DISTANCE / max_exact)
                         * (N_BUCKETS - max_exact)).astype(jnp.int32)
    large = jnp.minimum(large, N_BUCKETS - 1)
    return jnp.where(n < max_exact, n, large)


def diff_attend(q, k, v, q_pos, k_pos, rel_table, lam):
    d = q_pos[:, None] - k_pos[None, :]
    bias = jnp.moveaxis(rel_table.astype(jnp.float32)[rel_bucket(d)], -1, 0)
    logits = jnp.einsum('bqmd,bkmd->bmqk', q.astype(jnp.float32), k.astype(jnp.float32))
    logits = logits * (HEAD_DIM ** -0.5) + bias
    logits = jnp.where(d >= 0, logits, NEG)
    p = jax.nn.softmax(logits, axis=-1)
    b, _, tq, tk = p.shape
    p = p.reshape(b, N_HEADS_A, 2, tq, tk)
    a = p[:, :, 0] - lam * p[:, :, 1]
    return jnp.einsum('bhqk,bkhe->bqhe', a.astype(v.dtype), v)


def chunk_spatial(v, w_s, b_s):
    b, t, _ = v.shape
    n_chunks = -(-t // CHUNK)
    pad = n_chunks * CHUNK - t
    vp = jnp.pad(v, ((0, 0), (0, pad), (0, 0))).reshape(b, n_chunks, CHUNK, N_GROUPS_B, GROUP_B)
    w = w_s * jnp.tril(jnp.ones((CHUNK, CHUNK), w_s.dtype))
    s = jnp.einsum('gij,bcjgd->bcigd', w, vp) + b_s.T[None, None, :, :, None]
    return s.reshape(b, n_chunks * CHUNK, GMLP_WIDTH)[:, :t]


def layer_forward(x, attend, lam, lam_init, nf1, wf1gu, wf1d, nm, w_in, subln, gnorm,
                  w_s, b_s, w_out, nf2, wf2gu, wf2d):
    b, t, _ = x.shape
    x = x + 0.5 * swiglu(rmsnorm(x, nf1), wf1gu, wf1d)
    h = rmsnorm(x, nm)
    q, k, v, u_pre, g_pre = jnp.split(
        h @ w_in, [ATTN_WIDTH, 2 * ATTN_WIDTH, 3 * ATTN_WIDTH, 3 * ATTN_WIDTH + GMLP_WIDTH], axis=-1)
    q = q.reshape(b, t, N_MAPS, HEAD_DIM)
    k = k.reshape(b, t, N_MAPS, HEAD_DIM)
    v = v.reshape(b, t, N_HEADS_A, 2 * HEAD_DIM)
    o = attend(q, k, v, lam)
    o = (rmsnorm(o, subln) * (1.0 - lam_init)).reshape(b, t, ATTN_WIDTH)
    u = jax.nn.gelu(u_pre)
    gv = rmsnorm(jax.nn.gelu(g_pre).reshape(b, t, N_GROUPS_B, GROUP_B),
                 gnorm.reshape(N_GROUPS_B, GROUP_B)).reshape(b, t, GMLP_WIDTH)
    s = u * chunk_spatial(gv, w_s, b_s)
    x = x + jnp.concatenate([o, s], axis=-1) @ w_out
    x = x + 0.5 * swiglu(rmsnorm(x, nf2), wf2gu, wf2d)
    return x, k, v, gv


def setup_inputs(seed: int = 0) -> dict:
    key = jax.random.key(seed)
    ks = jax.random.split(key, 26)
    f32 = jnp.float32
    n_pages = PAST_LEN // PAGE_SIZE
    n_used = DEC_BATCH * n_pages
    n_pool = (n_used * 5) // 4
    nrm = lambda k, shape, s: jax.random.normal(k, shape, f32) * s
    gain = lambda k, shape: 1.0 + 0.02 * jax.random.normal(k, shape, f32)
    page_table = jax.random.permutation(ks[0], n_pool)[:n_used].reshape(DEC_BATCH, n_pages).astype(jnp.int32)
    return {
        "x_prompt": nrm(ks[1], (BATCH, SEQ, D_MODEL), 1.0),
        "x_sample": nrm(ks[2], (DEC_BATCH, DEC_SEQ, D_MODEL), 1.0),
        "cache_k": nrm(ks[3], (DEPTH, n_pool, PAGE_SIZE, N_MAPS, HEAD_DIM), 1.0),
        "cache_v": nrm(ks[4], (DEPTH, n_pool, PAGE_SIZE, N_HEADS_A, 2 * HEAD_DIM), 1.0),
        "page_table": page_table,
        "rel_table": nrm(ks[5], (N_BUCKETS, N_MAPS), 0.5),
        "norm_ffn1": gain(ks[6], (DEPTH, D_MODEL)),
        "w_ffn1_gu": nrm(ks[7], (DEPTH, D_MODEL, 2 * D_FF), D_MODEL ** -0.5),
        "w_ffn1_down": nrm(ks[8], (DEPTH, D_FF, D_MODEL), D_FF ** -0.5),
        "norm_mix": gain(ks[9], (DEPTH, D_MODEL)),
        "w_in": nrm(ks[10], (DEPTH, D_MODEL, IN_WIDTH), D_MODEL ** -0.5),
        "lambda_q1": nrm(ks[11], (DEPTH, HEAD_DIM), 0.1),
        "lambda_k1": nrm(ks[12], (DEPTH, HEAD_DIM), 0.1),
        "lambda_q2": nrm(ks[13], (DEPTH, HEAD_DIM), 0.1),
        "lambda_k2": nrm(ks[14], (DEPTH, HEAD_DIM), 0.1),
        "subln": gain(ks[15], (DEPTH, 2 * HEAD_DIM)),
        "gmlp_norm": gain(ks[16], (DEPTH, GMLP_WIDTH)),
        "w_spatial": nrm(ks[17], (DEPTH, N_GROUPS_B, CHUNK, CHUNK), CHUNK ** -0.5),
        "b_spatial": gain(ks[18], (DEPTH, N_GROUPS_B, CHUNK)),
        "w_out": nrm(ks[19], (DEPTH, D_MODEL, D_MODEL), D_MODEL ** -0.5),
        "norm_ffn2": gain(ks[20], (DEPTH, D_MODEL)),
        "w_ffn2_gu": nrm(ks[21], (DEPTH, D_MODEL, 2 * D_FF), D_MODEL ** -0.5),
        "w_ffn2_down": nrm(ks[22], (DEPTH, D_FF, D_MODEL), D_FF ** -0.5),
        "norm_final": gain(ks[23], (D_MODEL,)),
    }


def reference(x_prompt, x_sample, cache_k, cache_v, page_table, rel_table, norm_ffn1, w_ffn1_gu,
              w_ffn1_down, norm_mix, w_in, lambda_q1, lambda_k1, lambda_q2, lambda_k2, subln,
              gmlp_norm, w_spatial, b_spatial, w_out, norm_ffn2, w_ffn2_gu, w_ffn2_down, norm_final):
    bs = x_sample.shape[0]
    t_p = x_prompt.shape[1]
    t_s = x_sample.shape[1]
    n_pages = page_table.shape[1]
    past_len = n_pages * cache_k.shape[2]
    pos_p = jnp.arange(t_p, dtype=jnp.int32)
    pos_s = past_len + jnp.arange(t_s, dtype=jnp.int32)
    kpos_s = jnp.arange(past_len + t_s, dtype=jnp.int32)

    def attend_prompt(q, k, v, lam):
        b = q.shape[0]
        nb = t_p // Q_BLOCK
        qb = jnp.moveaxis(q.reshape(b, nb, Q_BLOCK, N_MAPS, HEAD_DIM), 1, 0)
        pb = pos_p.reshape(nb, Q_BLOCK)
        ob = lax.map(lambda a: diff_attend(a[0], k, v, a[1], pos_p, rel_table, lam), (qb, pb))
        return jnp.moveaxis(ob, 0, 1).reshape(b, t_p, N_HEADS_A, 2 * HEAD_DIM)

    xp, xs = x_prompt, x_sample
    kp_l, vp_l, ks_l, vs_l, gs_l = [], [], [], [], []
    for l in range(DEPTH):
        lam_init = 0.8 - 0.6 * math.exp(-0.3 * l)
        lam = (jnp.exp(jnp.sum(lambda_q1[l].astype(jnp.float32) * lambda_k1[l].astype(jnp.float32)))
               - jnp.exp(jnp.sum(lambda_q2[l].astype(jnp.float32) * lambda_k2[l].astype(jnp.float32)))
               + lam_init)
        ck, cv = cache_k[l], cache_v[l]

        def attend_sample(q, k, v, lam, ck=ck, cv=cv):
            k_past = ck[page_table].reshape(bs, past_len, N_MAPS, HEAD_DIM).astype(k.dtype)
            v_past = cv[page_table].reshape(bs, past_len, N_HEADS_A, 2 * HEAD_DIM).astype(v.dtype)
            k_all = jnp.concatenate([k_past, k], axis=1)
            v_all = jnp.concatenate([v_past, v], axis=1)
            return diff_attend(q, k_all, v_all, pos_s, kpos_s, rel_table, lam)

        w = (norm_ffn1[l], w_ffn1_gu[l], w_ffn1_down[l], norm_mix[l], w_in[l], subln[l],
             gmlp_norm[l], w_spatial[l], b_spatial[l], w_out[l], norm_ffn2[l], w_ffn2_gu[l],
             w_ffn2_down[l])
        xp, kp, vp, _ = layer_forward(xp, attend_prompt, lam, lam_init, *w)
        xs, kn, vn, gn = layer_forward(xs, attend_sample, lam, lam_init, *w)
        kp_l.append(kp)
        vp_l.append(vp)
        ks_l.append(kn)
        vs_l.append(vn)
        gs_l.append(gn)

    y_prompt = rmsnorm(xp, norm_final)
    y_sample = rmsnorm(xs, norm_final)
    k_prompt = jnp.stack(kp_l)
    v_prompt = jnp.stack(vp_l)
    k_sample = jnp.stack(ks_l)
    v_sample = jnp.stack(vs_l)
    gmlp_v_sample = jnp.stack(gs_l)
    return (y_prompt, y_sample, k_prompt, v_prompt, k_sample, v_sample, gmlp_v_sample)
```

```python
import functools
import math

import jax
import jax.numpy as jnp
from jax import lax
from jax.experimental import pallas as pl
from jax.experimental.pallas import tpu as pltpu

F32 = jnp.float32
BF16 = jnp.bfloat16

D_MODEL = 1024
D_FF = 2816
FF_CHUNK = 256
N_FF_CHUNKS = D_FF // FF_CHUNK
ATTN_WIDTH = 512
HEAD_DIM = 64
N_HEADS = 4
N_MAPS = 2 * N_HEADS
V_DIM = 2 * HEAD_DIM
GMLP_WIDTH = 512
N_GROUPS = 4
GROUP = GMLP_WIDTH // N_GROUPS
CHUNK = 128
N_BUCKETS = 32
MAX_EXACT = N_BUCKETS // 2
MAX_DISTANCE = 128
EPS = 1e-6
NEG = -1e30
LAM_INIT = 0.8 - 0.6 * math.exp(-0.3 * 0)

TOKEN_BLOCK = 256
TQ = 512
TK = 512
SUB = 256
PAGES_PER_STEP = 8
VMEM_LIMIT = 52 * 1024 * 1024


def _const_spec(shape):
    nd = len(shape)
    return pl.BlockSpec(shape, lambda *_: (0,) * nd, pipeline_mode=pl.Buffered(1))


def _rmsnorm(x, g):
    return x * lax.rsqrt(jnp.mean(x * x, axis=-1, keepdims=True) + EPS) * g


def _lam(lamv_ref):
    a = jnp.sum(lamv_ref[0:1, :] * lamv_ref[1:2, :], axis=1, keepdims=True)
    b = jnp.sum(lamv_ref[2:3, :] * lamv_ref[3:4, :], axis=1, keepdims=True)
    return jnp.exp(a) - jnp.exp(b) + LAM_INIT


def _bias_of_distance(d, tab_ref, m):
    n = jnp.maximum(d, 0)
    nf = jnp.maximum(n, 1).astype(F32)
    large = MAX_EXACT + (jnp.log(nf / MAX_EXACT) / math.log(MAX_DISTANCE / MAX_EXACT)
                         * (N_BUCKETS - MAX_EXACT)).astype(jnp.int32)
    large = jnp.minimum(large, N_BUCKETS - 1)
    bucket = jnp.where(n < MAX_EXACT, n, large)
    val = jnp.zeros(d.shape, F32)
    for b in range(N_BUCKETS):
        val = jnp.where(bucket == b, tab_ref[b, m], val)
    val = val - tab_ref[N_BUCKETS - 1, m]
    return jnp.where(d >= 0, val, NEG)


def _bias_kernel(tab_ref, e0_ref, e1_ref, bl_ref, bn_ref):
    m = pl.program_id(0)
    key = lax.broadcasted_iota(jnp.int32, (SUB, SUB), 0)
    qry = lax.broadcasted_iota(jnp.int32, (SUB, SUB), 1)
    e0_ref[0] = _bias_of_distance(qry - key, tab_ref, m)
    e1_ref[0] = _bias_of_distance(SUB + qry - key, tab_ref, m)
    t = lax.broadcasted_iota(jnp.int32, (8, CHUNK), 0)
    c = lax.broadcasted_iota(jnp.int32, (8, CHUNK), 1)
    bl_ref[...] = _bias_of_distance(CHUNK + t - c, tab_ref, m)
    bn_ref[...] = _bias_of_distance(t - c, tab_ref, m)


def _bias_tables(rel_table):
    return pl.pallas_call(
        _bias_kernel,
        grid=(N_MAPS,),
        in_specs=[pl.BlockSpec(memory_space=pltpu.SMEM)],
        out_specs=[
            pl.BlockSpec((1, SUB, SUB), lambda m: (m, 0, 0)),
            pl.BlockSpec((1, SUB, SUB), lambda m: (m, 0, 0)),
            pl.BlockSpec((8, CHUNK), lambda m: (m, 0)),
            pl.BlockSpec((8, CHUNK), lambda m: (m, 0)),
        ],
        out_shape=[
            jax.ShapeDtypeStruct((N_MAPS, SUB, SUB), F32),
            jax.ShapeDtypeStruct((N_MAPS, SUB, SUB), F32),
            jax.ShapeDtypeStruct((N_MAPS * 8, CHUNK), F32),
            jax.ShapeDtypeStruct((N_MAPS * 8, CHUNK), F32),
        ],
        name="bias_tables",
    )(rel_table)


def _swiglu_half_step(x, norm_ref, wg_ref, wu_ref, wd_ref, hn_ref, acc_ref):
    hn_ref[...] = _rmsnorm(x, norm_ref[...]).astype(BF16)
    acc_ref[...] = jnp.zeros_like(acc_ref)

    def body(c, carry):
        hn = hn_ref[...]
        gate = jnp.dot(hn, wg_ref[c], preferred_element_type=F32)
        up = jnp.dot(hn, wu_ref[c], preferred_element_type=F32)
        act = (jax.nn.silu(gate) * up).astype(BF16)
        acc_ref[...] += jnp.dot(act, wd_ref[c], preferred_element_type=F32)
        return carry

    lax.fori_loop(0, N_FF_CHUNKS, body, 0)
    return x + 0.5 * acc_ref[...]


def _ffn_in_kernel(x_ref, nf_ref, wg_ref, wu_ref, wd_ref, nm_ref, win_ref, gn_ref, ws_ref, bs_ref,
                   *rest, prompt, seq_in_chunk):
    if prompt:
        x1_ref, k_ref, v_ref, s_ref, qtm_ref, kb_ref, vt_ref, hn_ref, acc_ref = rest
    else:
        x1_ref, k_ref, v_ref, s_ref, q_ref, gv_ref, hn_ref, acc_ref = rest
    tm = x_ref.shape[0]

    x1 = _swiglu_half_step(x_ref[...], nf_ref, wg_ref, wu_ref, wd_ref, hn_ref, acc_ref)
    x1_ref[...] = x1
    h = _rmsnorm(x1, nm_ref[...]).astype(BF16)

    def proj(i):
        return jnp.dot(h, win_ref[:, i * ATTN_WIDTH:(i + 1) * ATTN_WIDTH], preferred_element_type=F32)

    q = proj(0) * (HEAD_DIM ** -0.5)
    k = proj(1)
    v = proj(2)
    u = jax.nn.gelu(proj(3))
    g = jax.nn.gelu(proj(4))
    k_ref[...] = k
    v_ref[...] = v

    row = lax.broadcasted_iota(jnp.int32, (CHUNK, CHUNK), 0)
    col = lax.broadcasted_iota(jnp.int32, (CHUNK, CHUNK), 1)
    keep = col <= row
    if seq_in_chunk < CHUNK:
        keep = keep & ((row // seq_in_chunk) == (col // seq_in_chunk))
    for grp in range(N_GROUPS):
        lanes = slice(grp * GROUP, (grp + 1) * GROUP)
        gg = g[:, lanes]
        gvn = gg * lax.rsqrt(jnp.mean(gg * gg, axis=-1, keepdims=True) + EPS) * gn_ref[:, lanes]
        if not prompt:
            gv_ref[:, lanes] = gvn
        wm = jnp.where(keep, ws_ref[grp], 0.0).astype(BF16)
        bcol = bs_ref[:, grp:grp + 1]
        for c in range(tm // CHUNK):
            rows = slice(c * CHUNK, (c + 1) * CHUNK)
            mixed = jnp.dot(wm, gvn[rows, :].astype(BF16), preferred_element_type=F32) + bcol
            s_ref[rows, lanes] = (u[rows, lanes] * mixed).astype(BF16)

    if prompt:
        for hd in range(N_HEADS):
            lanes = slice(hd * V_DIM, (hd + 1) * V_DIM)
            qt = q[:, lanes].T
            feat = lax.broadcasted_iota(jnp.int32, qt.shape, 0)
            qtm_ref[0, hd, 0] = jnp.where(feat < HEAD_DIM, qt, 0.0).astype(BF16)
            qtm_ref[0, hd, 1] = jnp.where(feat >= HEAD_DIM, qt, 0.0).astype(BF16)
            kb_ref[0, hd] = k[:, lanes].astype(BF16)
            vt_ref[0, hd] = v[:, lanes].T.astype(BF16)
    else:
        q_ref[...] = q


def _ffn_in(x, nf, wg, wu, wd, nm, win, gn, ws, bs, *, batch, seq_in_chunk, prompt):
    n = x.shape[0]
    tm = TOKEN_BLOCK
    nb = n // tm
    per_batch = nb // batch
    tok = lambda w: pl.BlockSpec((tm, w), lambda i: (i, 0))
    in_specs = [
        tok(D_MODEL),
        _const_spec((1, D_MODEL)),
        _const_spec(wg.shape), _const_spec(wu.shape), _const_spec(wd.shape),
        _const_spec((1, D_MODEL)),
        _const_spec(win.shape),
        _const_spec((1, GMLP_WIDTH)),
        _const_spec(ws.shape), _const_spec(bs.shape),
    ]
    out_specs = [tok(D_MODEL), tok(ATTN_WIDTH), tok(ATTN_WIDTH), tok(GMLP_WIDTH)]
    out_shape = [
        jax.ShapeDtypeStruct((n, D_MODEL), F32),
        jax.ShapeDtypeStruct((n, ATTN_WIDTH), F32),
        jax.ShapeDtypeStruct((n, ATTN_WIDTH), F32),
        jax.ShapeDtypeStruct((n, GMLP_WIDTH), BF16),
    ]
    if prompt:
        t = n // batch
        out_specs += [
            pl.BlockSpec((1, N_HEADS, 2, V_DIM, tm), lambda i: (i // per_batch, 0, 0, 0, i % per_batch)),
            pl.BlockSpec((1, N_HEADS, tm, V_DIM), lambda i: (i // per_batch, 0, i % per_batch, 0)),
            pl.BlockSpec((1, N_HEADS, V_DIM, tm), lambda i: (i // per_batch, 0, 0, i % per_batch)),
        ]
        out_shape += [
            jax.ShapeDtypeStruct((batch, N_HEADS, 2, V_DIM, t), BF16),
            jax.ShapeDtypeStruct((batch, N_HEADS, t, V_DIM), BF16),
            jax.ShapeDtypeStruct((batch, N_HEADS, V_DIM, t), BF16),
        ]
    else:
        out_specs += [tok(ATTN_WIDTH), tok(GMLP_WIDTH)]
        out_shape += [
            jax.ShapeDtypeStruct((n, ATTN_WIDTH), F32),
            jax.ShapeDtypeStruct((n, GMLP_WIDTH), F32),
        ]
    return pl.pallas_call(
        functools.partial(_ffn_in_kernel, prompt=prompt, seq_in_chunk=seq_in_chunk),
        grid=(nb,),
        in_specs=in_specs,
        out_specs=out_specs,
        out_shape=out_shape,
        scratch_shapes=[pltpu.VMEM((tm, D_MODEL), BF16), pltpu.VMEM((tm, D_MODEL), F32)],
        compiler_params=pltpu.CompilerParams(dimension_semantics=("arbitrary",),
                                             vmem_limit_bytes=VMEM_LIMIT),
        name="ffn_in_prompt" if prompt else "ffn_in_sample",
    )(x, nf, wg, wu, wd, nm, win, gn, ws, bs)


def _out_ffn_kernel(x1_ref, o_ref, s_ref, wout_ref, nf_ref, wg_ref, wu_ref, wd_ref, nfin_ref,
                    y_ref, hn_ref, acc_ref, *, head_major):
    if head_major:
        o = jnp.concatenate([o_ref[0, hd] for hd in range(N_HEADS)], axis=1)
    else:
        o = o_ref[...]
    x2 = (x1_ref[...]
          + jnp.dot(o, wout_ref[0:ATTN_WIDTH, :], preferred_element_type=F32)
          + jnp.dot(s_ref[...], wout_ref[ATTN_WIDTH:D_MODEL, :], preferred_element_type=F32))
    x3 = _swiglu_half_step(x2, nf_ref, wg_ref, wu_ref, wd_ref, hn_ref, acc_ref)
    y_ref[...] = _rmsnorm(x3, nfin_ref[...])


def _out_ffn(x1, o, s, wout, nf, wg, wu, wd, nfin, *, batch, head_major):
    n = x1.shape[0]
    tm = TOKEN_BLOCK
    nb = n // tm
    per_batch = nb // batch
    tok = lambda w: pl.BlockSpec((tm, w), lambda i: (i, 0))
    if head_major:
        o_spec = pl.BlockSpec((1, N_HEADS, tm, V_DIM), lambda i: (i // per_batch, 0, i % per_batch, 0))
    else:
        o_spec = tok(ATTN_WIDTH)
    return pl.pallas_call(
        functools.partial(_out_ffn_kernel, head_major=head_major),
        grid=(nb,),
        in_specs=[
            tok(D_MODEL), o_spec, tok(GMLP_WIDTH),
            _const_spec(wout.shape),
            _const_spec((1, D_MODEL)),
            _const_spec(wg.shape), _const_spec(wu.shape), _const_spec(wd.shape),
            _const_spec((1, D_MODEL)),
        ],
        out_specs=tok(D_MODEL),
        out_shape=jax.ShapeDtypeStruct((n, D_MODEL), F32),
        scratch_shapes=[pltpu.VMEM((tm, D_MODEL), BF16), pltpu.VMEM((tm, D_MODEL), F32)],
        compiler_params=pltpu.CompilerParams(dimension_semantics=("arbitrary",),
                                             vmem_limit_bytes=VMEM_LIMIT),
        name="out_ffn_prompt" if head_major else "out_ffn_sample",
    )(x1, o, s, wout, nf, wg, wu, wd, nfin)


def _attn_prompt_kernel(qi_ref, ki_ref, qtm_ref, kb_ref, vt_ref, e0_ref, e1_ref, lamv_ref, subln_ref,
                        o_ref, m_ref, l_ref, acc_ref):
    step = pl.program_id(1)
    qi = qi_ref[step]
    ki = ki_ref[step]

    @pl.when(ki == 0)
    def _():
        m_ref[...] = jnp.full_like(m_ref, NEG)
        l_ref[...] = jnp.zeros_like(l_ref)
        acc_ref[...] = jnp.zeros_like(acc_ref)

    def update(hd, mp, q_lo, q_n, key_blocks):
        idx = 2 * hd + mp
        cols = slice(q_lo, q_lo + q_n)
        qm = qtm_ref[0, hd, mp, :, cols]
        logits = []
        for k_lo, k_n, bias_ref in key_blocks:
            s = jnp.dot(kb_ref[0, hd, k_lo:k_lo + k_n, :], qm, preferred_element_type=F32)
            if bias_ref is not None:
                s = s + bias_ref[idx]
            logits.append(s)
        m_prev = m_ref[idx, :, cols]
        m_new = m_prev
        for s in logits:
            m_new = jnp.maximum(m_new, jnp.max(s, axis=0, keepdims=True))
        alpha = jnp.exp(m_prev - m_new)
        l_new = alpha * l_ref[idx, :, cols]
        pv = None
        for (k_lo, k_n, _), s in zip(key_blocks, logits):
            p = jnp.exp(s - m_new)
            l_new = l_new + jnp.sum(p, axis=0, keepdims=True)
            d = jnp.dot(vt_ref[0, hd, :, k_lo:k_lo + k_n], p.astype(BF16), preferred_element_type=F32)
            pv = d if pv is None else pv + d
        acc_ref[idx, :, cols] = alpha * acc_ref[idx, :, cols] + pv
        m_ref[idx, :, cols] = m_new
        l_ref[idx, :, cols] = l_new

    def for_heads(fn):
        def body(hd, carry):
            fn(hd)
            return carry
        lax.fori_loop(0, N_HEADS, body, 0)

    @pl.when(ki < qi - 1)
    def _():
        def far(hd):
            for mp in range(2):
                update(hd, mp, 0, TQ, [(0, TK, None)])
        for_heads(far)

    @pl.when(ki == qi - 1)
    def _():
        def near(hd):
            for mp in range(2):
                update(hd, mp, 0, SUB, [(0, SUB, None), (SUB, SUB, e1_ref)])
                update(hd, mp, SUB, SUB, [(0, TK, None)])
        for_heads(near)

    @pl.when(ki == qi)
    def _():
        lam = _lam(lamv_ref)

        def diag(hd):
            for mp in range(2):
                update(hd, mp, 0, SUB, [(0, SUB, e0_ref)])
                update(hd, mp, SUB, SUB, [(0, SUB, e1_ref), (SUB, SUB, e0_ref)])
            o1 = acc_ref[2 * hd] * (1.0 / l_ref[2 * hd])
            o2 = acc_ref[2 * hd + 1] * (1.0 / l_ref[2 * hd + 1])
            o = o1 - lam * o2
            inv = lax.rsqrt(jnp.mean(o * o, axis=0, keepdims=True) + EPS)
            on = o * inv * subln_ref[...] * (1.0 - LAM_INIT)
            o_ref[0, hd] = on.T.astype(BF16)
        for_heads(diag)


def _attn_prompt(qtm, kb, vt, e0, e1, lamv, subln_col):
    batch, _, _, _, t = qtm.shape
    nq = t // TQ
    pairs = [(q, k) for q in range(nq) for k in range(q + 1)]
    qi_arr = jnp.asarray([p[0] for p in pairs], jnp.int32)
    ki_arr = jnp.asarray([p[1] for p in pairs], jnp.int32)
    full = lambda shape: pl.BlockSpec(shape, lambda b, s, qi, ki: (0,) * len(shape),
                                      pipeline_mode=pl.Buffered(1))
    grid_spec = pltpu.PrefetchScalarGridSpec(
        num_scalar_prefetch=2,
        grid=(batch, len(pairs)),
        in_specs=[
            pl.BlockSpec((1, N_HEADS, 2, V_DIM, TQ), lambda b, s, qi, ki: (b, 0, 0, 0, qi[s])),
            pl.BlockSpec((1, N_HEADS, TK, V_DIM), lambda b, s, qi, ki: (b, 0, ki[s], 0)),
            pl.BlockSpec((1, N_HEADS, V_DIM, TK), lambda b, s, qi, ki: (b, 0, 0, ki[s])),
            full(e0.shape), full(e1.shape), full(lamv.shape), full(subln_col.shape),
        ],
        out_specs=pl.BlockSpec((1, N_HEADS, TQ, V_DIM), lambda b, s, qi, ki: (b, 0, qi[s], 0)),
        scratch_shapes=[
            pltpu.VMEM((N_MAPS, 1, TQ), F32),
            pltpu.VMEM((N_MAPS, 1, TQ), F32),
            pltpu.VMEM((N_MAPS, V_DIM, TQ), F32),
        ],
    )
    return pl.pallas_call(
        _attn_prompt_kernel,
        grid_spec=grid_spec,
        out_shape=jax.ShapeDtypeStruct((batch, N_HEADS, t, V_DIM), BF16),
        compiler_params=pltpu.CompilerParams(dimension_semantics=("arbitrary", "arbitrary"),
                                             vmem_limit_bytes=VMEM_LIMIT),
        name="attn_prompt",
    )(qi_arr, ki_arr, qtm, kb, vt, e0, e1, lamv, subln_col)


def _attn_sample_kernel(pt_ref, q_ref, kn_ref, vn_ref, *rest):
    kp = rest[:PAGES_PER_STEP]
    vp = rest[PAGES_PER_STEP:2 * PAGES_PER_STEP]
    bl_ref, bn_ref, lamv_ref, subln_ref, o_ref, qbd_ref, m_ref, l_ref, acc_ref = rest[2 * PAGES_PER_STEP:]
    j = pl.program_id(1)
    last = pl.num_programs(1) - 1
    n_tok = q_ref.shape[1]
    n_rows = N_MAPS * n_tok

    @pl.when(j == 0)
    def _():
        q = jnp.concatenate([q_ref[0]] * N_MAPS, axis=0)
        r = lax.broadcasted_iota(jnp.int32, q.shape, 0)
        c = lax.broadcasted_iota(jnp.int32, q.shape, 1)
        qbd_ref[...] = jnp.where((r // n_tok) == (c // HEAD_DIM), q, 0.0).astype(BF16)
        m_ref[...] = jnp.full_like(m_ref, NEG)
        l_ref[...] = jnp.zeros_like(l_ref)
        acc_ref[...] = jnp.zeros_like(acc_ref)

    def update(logit_blocks, value_blocks):
        s = jnp.concatenate(logit_blocks, axis=1) if len(logit_blocks) > 1 else logit_blocks[0]
        m_prev = m_ref[...]
        m_new = jnp.maximum(m_prev, jnp.max(s, axis=1, keepdims=True))
        alpha = jnp.exp(m_prev - m_new)
        p = jnp.exp(s - m_new)
        l_ref[...] = alpha * l_ref[...] + jnp.sum(p, axis=1, keepdims=True)
        pv = None
        for i, vb in enumerate(value_blocks):
            d = jnp.dot(p[:, i * CHUNK:(i + 1) * CHUNK].astype(BF16), vb, preferred_element_type=F32)
            pv = d if pv is None else pv + d
        acc_ref[...] = alpha * acc_ref[...] + pv
        m_ref[...] = m_new

    def logits_of(keys):
        return lax.dot_general(qbd_ref[...], keys, (((1,), (1,)), ((), ())), preferred_element_type=F32)

    logit_blocks = [logits_of(kp[i][0].astype(BF16)) for i in range(PAGES_PER_STEP)]
    logit_blocks[-1] = logit_blocks[-1] + jnp.where(j == last, bl_ref[...], 0.0)
    update(logit_blocks, [vp[i][0].astype(BF16) for i in range(PAGES_PER_STEP)])

    @pl.when(j == last)
    def _():
        pad = jnp.zeros((CHUNK - n_tok, ATTN_WIDTH), F32)
        k_new = jnp.concatenate([kn_ref[0], pad], axis=0).astype(BF16)
        v_new = jnp.concatenate([vn_ref[0], pad], axis=0).astype(BF16)
        update([logits_of(k_new) + bn_ref[...]], [v_new])

        lam = _lam(lamv_ref)
        inv_l = 1.0 / l_ref[...]
        for hd in range(N_HEADS):
            lanes = slice(hd * V_DIM, (hd + 1) * V_DIM)
            r1 = slice(2 * hd * n_tok, (2 * hd + 1) * n_tok)
            r2 = slice((2 * hd + 1) * n_tok, (2 * hd + 2) * n_tok)
            o = acc_ref[r1, lanes] * inv_l[r1] - lam * (acc_ref[r2, lanes] * inv_l[r2])
            o_ref[0, :, lanes] = (_rmsnorm(o, subln_ref[...]) * (1.0 - LAM_INIT)).astype(BF16)


def _attn_sample(page_table, q, k_new, v_new, cache_k, cache_v, bl, bn, lamv, subln_row):
    n_seq, n_tok, _ = q.shape
    n_pages = page_table.shape[1]
    n_steps = n_pages // PAGES_PER_STEP
    n_rows = N_MAPS * n_tok
    tok_spec = pl.BlockSpec((1, n_tok, ATTN_WIDTH), lambda b, j, pt: (b, 0, 0))

    def page_spec(i):
        return pl.BlockSpec((1, CHUNK, ATTN_WIDTH),
                            lambda b, j, pt: (pt[b, j * PAGES_PER_STEP + i], 0, 0))

    full = lambda shape: pl.BlockSpec(shape, lambda b, j, pt: (0,) * len(shape))
    grid_spec = pltpu.PrefetchScalarGridSpec(
        num_scalar_prefetch=1,
        grid=(n_seq, n_steps),
        in_specs=([tok_spec, tok_spec, tok_spec]
                  + [page_spec(i) for i in range(PAGES_PER_STEP)]
                  + [page_spec(i) for i in range(PAGES_PER_STEP)]
                  + [full(bl.shape), full(bn.shape), full(lamv.shape), full(subln_row.shape)]),
        out_specs=tok_spec,
        scratch_shapes=[
            pltpu.VMEM((n_rows, ATTN_WIDTH), BF16),
            pltpu.VMEM((n_rows, 1), F32),
            pltpu.VMEM((n_rows, 1), F32),
            pltpu.VMEM((n_rows, ATTN_WIDTH), F32),
        ],
    )
    return pl.pallas_call(
        _attn_sample_kernel,
        grid_spec=grid_spec,
        out_shape=jax.ShapeDtypeStruct((n_seq, n_tok, ATTN_WIDTH), BF16),
        compiler_params=pltpu.CompilerParams(dimension_semantics=("arbitrary", "arbitrary"),
                                             vmem_limit_bytes=VMEM_LIMIT),
        name="attn_sample",
    )(page_table, q, k_new, v_new, *([cache_k] * PAGES_PER_STEP), *([cache_v] * PAGES_PER_STEP),
      bl, bn, lamv, subln_row)


def _ffn_weights(w_gu, w_down):
    w_gu = w_gu.astype(BF16)
    split = lambda w: w.reshape(D_MODEL, N_FF_CHUNKS, FF_CHUNK).transpose(1, 0, 2)
    return (split(w_gu[:, :D_FF]), split(w_gu[:, D_FF:]),
            w_down.astype(BF16).reshape(N_FF_CHUNKS, FF_CHUNK, D_MODEL))


def kernel(x_prompt, x_sample, cache_k, cache_v, page_table, rel_table, norm_ffn1, w_ffn1_gu, w_ffn1_down, norm_mix, w_in, lambda_q1, lambda_k1, lambda_q2, lambda_k2, subln, gmlp_norm, w_spatial, b_spatial, w_out, norm_ffn2, w_ffn2_gu, w_ffn2_down, norm_final):
    depth = cache_k.shape[0]
    assert depth == 1
    batch, t_p, _ = x_prompt.shape
    n_seq, t_s, _ = x_sample.shape
    n_pool, page = cache_k.shape[1], cache_k.shape[2]
    assert page == CHUNK and CHUNK % t_s == 0 and (n_seq * t_s) % TOKEN_BLOCK == 0

    row = lambda a: a.reshape(1, -1)
    wg1, wu1, wd1 = _ffn_weights(w_ffn1_gu[0], w_ffn1_down[0])
    wg2, wu2, wd2 = _ffn_weights(w_ffn2_gu[0], w_ffn2_down[0])
    win = w_in[0].astype(BF16)
    wout = w_out[0].astype(BF16)
    lamv = jnp.stack([lambda_q1[0], lambda_k1[0], lambda_q2[0], lambda_k2[0]])
    reps = CHUNK // t_s
    ws_sample = jnp.tile(w_spatial[0][:, :t_s, :t_s], (1, reps, reps))
    bs_sample = jnp.tile(b_spatial[0][:, :t_s], (1, reps)).T
    ffn1 = (row(norm_ffn1[0]), wg1, wu1, wd1)
    mix = (row(norm_mix[0]), win, row(gmlp_norm[0]))

    e0, e1, bl, bn = _bias_tables(rel_table)

    x1p, kp, vp, sp, qtm, kb, vt = _ffn_in(
        x_prompt.reshape(batch * t_p, D_MODEL), *ffn1, *mix, w_spatial[0], b_spatial[0].T,
        batch=batch, seq_in_chunk=CHUNK, prompt=True)
    x1s, ks, vs, ss, qs, gvs = _ffn_in(
        x_sample.reshape(n_seq * t_s, D_MODEL), *ffn1, *mix, ws_sample, bs_sample,
        batch=1, seq_in_chunk=t_s, prompt=False)

    op = _attn_prompt(qtm, kb, vt, e0, e1, lamv, subln[0].reshape(V_DIM, 1))
    os_ = _attn_sample(
        page_table,
        qs.reshape(n_seq, t_s, ATTN_WIDTH), ks.reshape(n_seq, t_s, ATTN_WIDTH),
        vs.reshape(n_seq, t_s, ATTN_WIDTH),
        cache_k[0].reshape(n_pool, page, ATTN_WIDTH), cache_v[0].reshape(n_pool, page, ATTN_WIDTH),
        bl, bn, lamv, row(subln[0]))

    ffn2 = (wout, row(norm_ffn2[0]), wg2, wu2, wd2, row(norm_final))
    yp = _out_ffn(x1p, op, sp, *ffn2, batch=batch, head_major=True)
    ys = _out_ffn(x1s, os_.reshape(n_seq * t_s, ATTN_WIDTH), ss, *ffn2, batch=1, head_major=False)

    return (
        yp.reshape(batch, t_p, D_MODEL),
        ys.reshape(n_seq, t_s, D_MODEL),
        kp.reshape(depth, batch, t_p, N_MAPS, HEAD_DIM),
        vp.reshape(depth, batch, t_p, N_HEADS, V_DIM),
        ks.reshape(depth, n_seq, t_s, N_MAPS, HEAD_DIM),
        vs.reshape(depth, n_seq, t_s, N_HEADS, V_DIM),
        gvs.reshape(depth, n_seq, t_s, GMLP_WIDTH),
    )
```

```python
import functools
import math

import jax
import jax.numpy as jnp
from jax import lax
from jax.experimental import pallas as pl
from jax.experimental.pallas import tpu as pltpu

F32 = jnp.float32
BF16 = jnp.bfloat16

D_MODEL = 1024
D_FF = 2816
FF_CHUNK = 256
N_FF_CHUNKS = D_FF // FF_CHUNK
ATTN_WIDTH = 512
HEAD_DIM = 64
N_HEADS = 4
N_MAPS = 2 * N_HEADS
V_DIM = 2 * HEAD_DIM
GMLP_WIDTH = 512
N_GROUPS = 4
GROUP = GMLP_WIDTH // N_GROUPS
CHUNK = 128
N_BUCKETS = 32
MAX_EXACT = N_BUCKETS // 2
MAX_DISTANCE = 128
EPS = 1e-6
NEG = -1e30
LAM_INIT = 0.8 - 0.6 * math.exp(-0.3 * 0)

TOKEN_BLOCK = 256
TQ = 512
TK = 512
SUB = 256
PAGES_PER_STEP = 8
VMEM_LIMIT = 52 * 1024 * 1024


def _const_spec(shape):
    nd = len(shape)
    return pl.BlockSpec(shape, lambda *_: (0,) * nd, pipeline_mode=pl.Buffered(1))


def _rmsnorm(x, g):
    return x * lax.rsqrt(jnp.mean(x * x, axis=-1, keepdims=True) + EPS) * g


def _lam(lamv_ref):
    a = jnp.sum(lamv_ref[0:1, :] * lamv_ref[1:2, :], axis=1, keepdims=True)
    b = jnp.sum(lamv_ref[2:3, :] * lamv_ref[3:4, :], axis=1, keepdims=True)
    return jnp.exp(a) - jnp.exp(b) + LAM_INIT


def _bias_of_distance(d, tab_ref, m):
    n = jnp.maximum(d, 0)
    nf = jnp.maximum(n, 1).astype(F32)
    large = MAX_EXACT + (jnp.log(nf / MAX_EXACT) / math.log(MAX_DISTANCE / MAX_EXACT)
                         * (N_BUCKETS - MAX_EXACT)).astype(jnp.int32)
    large = jnp.minimum(large, N_BUCKETS - 1)
    bucket = jnp.where(n < MAX_EXACT, n, large)
    val = jnp.zeros(d.shape, F32)
    for b in range(N_BUCKETS):
        val = jnp.where(bucket == b, tab_ref[b, m], val)
    val = val - tab_ref[N_BUCKETS - 1, m]
    return jnp.where(d >= 0, val, NEG)


def _bias_kernel(tab_ref, e0_ref, e1_ref, bl_ref, bn_ref):
    m = pl.program_id(0)
    key = lax.broadcasted_iota(jnp.int32, (SUB, SUB), 0)
    qry = lax.broadcasted_iota(jnp.int32, (SUB, SUB), 1)
    e0_ref[0] = _bias_of_distance(qry - key, tab_ref, m)
    e1_ref[0] = _bias_of_distance(SUB + qry - key, tab_ref, m)
    t = lax.broadcasted_iota(jnp.int32, (8, CHUNK), 0)
    c = lax.broadcasted_iota(jnp.int32, (8, CHUNK), 1)
    bl_ref[...] = _bias_of_distance(CHUNK + t - c, tab_ref, m)
    bn_ref[...] = _bias_of_distance(t - c, tab_ref, m)


def _bias_tables(rel_table):
    return pl.pallas_call(
        _bias_kernel,
        grid=(N_MAPS,),
        in_specs=[pl.BlockSpec(memory_space=pltpu.SMEM)],
        out_specs=[
            pl.BlockSpec((1, SUB, SUB), lambda m: (m, 0, 0)),
            pl.BlockSpec((1, SUB, SUB), lambda m: (m, 0, 0)),
            pl.BlockSpec((8, CHUNK), lambda m: (m, 0)),
            pl.BlockSpec((8, CHUNK), lambda m: (m, 0)),
        ],
        out_shape=[
            jax.ShapeDtypeStruct((N_MAPS, SUB, SUB), F32),
            jax.ShapeDtypeStruct((N_MAPS, SUB, SUB), F32),
            jax.ShapeDtypeStruct((N_MAPS * 8, CHUNK), F32),
            jax.ShapeDtypeStruct((N_MAPS * 8, CHUNK), F32),
        ],
        name="bias_tables",
    )(rel_table)


def _swiglu_half_step(x, norm_ref, wg_ref, wu_ref, wd_ref, hn_ref, acc_ref):
    hn_ref[...] = _rmsnorm(x, norm_ref[...]).astype(BF16)
    acc_ref[...] = jnp.zeros_like(acc_ref)

    def body(c, carry):
        hn = hn_ref[...]
        gate = jnp.dot(hn, wg_ref[c], preferred_element_type=F32)
        up = jnp.dot(hn, wu_ref[c], preferred_element_type=F32)
        act = (jax.nn.silu(gate) * up).astype(BF16)
        acc_ref[...] += jnp.dot(act, wd_ref[c], preferred_element_type=F32)
        return carry

    lax.fori_loop(0, N_FF_CHUNKS, body, 0)
    return x + 0.5 * acc_ref[...]


def _ffn_in_kernel(x_ref, nf_ref, wg_ref, wu_ref, wd_ref, nm_ref, win_ref, gn_ref, ws_ref, bs_ref,
                   *rest, prompt, seq_in_chunk):
    if prompt:
        x1_ref, k_ref, v_ref, s_ref, qtm_ref, kb_ref, vt_ref, hn_ref, acc_ref = rest
    else:
        x1_ref, k_ref, v_ref, s_ref, q_ref, gv_ref, hn_ref, acc_ref = rest
    tm = x_ref.shape[0]

    x1 = _swiglu_half_step(x_ref[...], nf_ref, wg_ref, wu_ref, wd_ref, hn_ref, acc_ref)
    x1_ref[...] = x1
    h = _rmsnorm(x1, nm_ref[...]).astype(BF16)

    def proj(i):
        return jnp.dot(h, win_ref[:, i * ATTN_WIDTH:(i + 1) * ATTN_WIDTH], preferred_element_type=F32)

    q = proj(0) * (HEAD_DIM ** -0.5)
    k = proj(1)
    v = proj(2)
    u = jax.nn.gelu(proj(3))
    g = jax.nn.gelu(proj(4))
    k_ref[...] = k.T[None] if prompt else k
    v_ref[...] = v

    row = lax.broadcasted_iota(jnp.int32, (CHUNK, CHUNK), 0)
    col = lax.broadcasted_iota(jnp.int32, (CHUNK, CHUNK), 1)
    keep = col <= row
    if seq_in_chunk < CHUNK:
        keep = keep & ((row // seq_in_chunk) == (col // seq_in_chunk))
    for grp in range(N_GROUPS):
        lanes = slice(grp * GROUP, (grp + 1) * GROUP)
        gg = g[:, lanes]
        gvn = gg * lax.rsqrt(jnp.mean(gg * gg, axis=-1, keepdims=True) + EPS) * gn_ref[:, lanes]
        if not prompt:
            gv_ref[:, lanes] = gvn
        wm = jnp.where(keep, ws_ref[grp], 0.0).astype(BF16)
        bcol = bs_ref[:, grp:grp + 1]
        for c in range(tm // CHUNK):
            rows = slice(c * CHUNK, (c + 1) * CHUNK)
            mixed = jnp.dot(wm, gvn[rows, :].astype(BF16), preferred_element_type=F32) + bcol
            s_ref[rows, lanes] = (u[rows, lanes] * mixed).astype(BF16)

    if prompt:
        for hd in range(N_HEADS):
            lanes = slice(hd * V_DIM, (hd + 1) * V_DIM)
            qt = q[:, lanes].T
            feat = lax.broadcasted_iota(jnp.int32, qt.shape, 0)
            qtm_ref[0, hd, 0] = jnp.where(feat < HEAD_DIM, qt, 0.0).astype(BF16)
            qtm_ref[0, hd, 1] = jnp.where(feat >= HEAD_DIM, qt, 0.0).astype(BF16)
            kb_ref[0, hd] = k[:, lanes].astype(BF16)
            vt_ref[0, hd] = v[:, lanes].T.astype(BF16)
    else:
        q_ref[...] = q


def _ffn_in(x, nf, wg, wu, wd, nm, win, gn, ws, bs, *, batch, seq_in_chunk, prompt):
    n = x.shape[0]
    tm = TOKEN_BLOCK
    nb = n // tm
    per_batch = nb // batch
    tok = lambda w: pl.BlockSpec((tm, w), lambda i: (i, 0))
    in_specs = [
        tok(D_MODEL),
        _const_spec((1, D_MODEL)),
        _const_spec(wg.shape), _const_spec(wu.shape), _const_spec(wd.shape),
        _const_spec((1, D_MODEL)),
        _const_spec(win.shape),
        _const_spec((1, GMLP_WIDTH)),
        _const_spec(ws.shape), _const_spec(bs.shape),
    ]
    out_specs = [tok(D_MODEL), tok(ATTN_WIDTH), tok(ATTN_WIDTH), tok(GMLP_WIDTH)]
    out_shape = [
        jax.ShapeDtypeStruct((n, D_MODEL), F32),
        jax.ShapeDtypeStruct((n, ATTN_WIDTH), F32),
        jax.ShapeDtypeStruct((n, ATTN_WIDTH), F32),
        jax.ShapeDtypeStruct((n, GMLP_WIDTH), BF16),
    ]
    if prompt:
        t = n // batch
        out_specs[1] = pl.BlockSpec((1, ATTN_WIDTH, tm), lambda i: (i // per_batch, 0, i % per_batch))
        out_shape[1] = jax.ShapeDtypeStruct((batch, ATTN_WIDTH, t), F32)
        out_specs += [
            pl.BlockSpec((1, N_HEADS, 2, V_DIM, tm), lambda i: (i // per_batch, 0, 0, 0, i % per_batch)),
            pl.BlockSpec((1, N_HEADS, tm, V_DIM), lambda i: (i // per_batch, 0, i % per_batch, 0)),
            pl.BlockSpec((1, N_HEADS, V_DIM, tm), lambda i: (i // per_batch, 0, 0, i % per_batch)),
        ]
        out_shape += [
            jax.ShapeDtypeStruct((batch, N_HEADS, 2, V_DIM, t), BF16),
            jax.ShapeDtypeStruct((batch, N_HEADS, t, V_DIM), BF16),
            jax.ShapeDtypeStruct((batch, N_HEADS, V_DIM, t), BF16),
        ]
    else:
        out_specs += [tok(ATTN_WIDTH), tok(GMLP_WIDTH)]
        out_shape += [
            jax.ShapeDtypeStruct((n, ATTN_WIDTH), F32),
            jax.ShapeDtypeStruct((n, GMLP_WIDTH), F32),
        ]
    return pl.pallas_call(
        functools.partial(_ffn_in_kernel, prompt=prompt, seq_in_chunk=seq_in_chunk),
        grid=(nb,),
        in_specs=in_specs,
        out_specs=out_specs,
        out_shape=out_shape,
        scratch_shapes=[pltpu.VMEM((tm, D_MODEL), BF16), pltpu.VMEM((tm, D_MODEL), F32)],
        compiler_params=pltpu.CompilerParams(dimension_semantics=("arbitrary",),
                                             vmem_limit_bytes=VMEM_LIMIT),
        name="ffn_in_prompt" if prompt else "ffn_in_sample",
    )(x, nf, wg, wu, wd, nm, win, gn, ws, bs)


def _out_ffn_kernel(x1_ref, o_ref, s_ref, wout_ref, nf_ref, wg_ref, wu_ref, wd_ref, nfin_ref,
                    y_ref, hn_ref, acc_ref, *, head_major):
    if head_major:
        o = jnp.concatenate([o_ref[0, hd] for hd in range(N_HEADS)], axis=1)
    else:
        o = o_ref[...]
    x2 = (x1_ref[...]
          + jnp.dot(o, wout_ref[0:ATTN_WIDTH, :], preferred_element_type=F32)
          + jnp.dot(s_ref[...], wout_ref[ATTN_WIDTH:D_MODEL, :], preferred_element_type=F32))
    x3 = _swiglu_half_step(x2, nf_ref, wg_ref, wu_ref, wd_ref, hn_ref, acc_ref)
    y_ref[...] = _rmsnorm(x3, nfin_ref[...])


def _out_ffn(x1, o, s, wout, nf, wg, wu, wd, nfin, *, batch, head_major):
    n = x1.shape[0]
    tm = TOKEN_BLOCK
    nb = n // tm
    per_batch = nb // batch
    tok = lambda w: pl.BlockSpec((tm, w), lambda i: (i, 0))
    if head_major:
        o_spec = pl.BlockSpec((1, N_HEADS, tm, V_DIM), lambda i: (i // per_batch, 0, i % per_batch, 0))
    else:
        o_spec = tok(ATTN_WIDTH)
    return pl.pallas_call(
        functools.partial(_out_ffn_kernel, head_major=head_major),
        grid=(nb,),
        in_specs=[
            tok(D_MODEL), o_spec, tok(GMLP_WIDTH),
            _const_spec(wout.shape),
            _const_spec((1, D_MODEL)),
            _const_spec(wg.shape), _const_spec(wu.shape), _const_spec(wd.shape),
            _const_spec((1, D_MODEL)),
        ],
        out_specs=tok(D_MODEL),
        out_shape=jax.ShapeDtypeStruct((n, D_MODEL), F32),
        scratch_shapes=[pltpu.VMEM((tm, D_MODEL), BF16), pltpu.VMEM((tm, D_MODEL), F32)],
        compiler_params=pltpu.CompilerParams(dimension_semantics=("arbitrary",),
                                             vmem_limit_bytes=VMEM_LIMIT),
        name="out_ffn_prompt" if head_major else "out_ffn_sample",
    )(x1, o, s, wout, nf, wg, wu, wd, nfin)


def _attn_prompt_kernel(qi_ref, ki_ref, qtm_ref, kb_ref, vt_ref, e0_ref, e1_ref, lamv_ref, subln_ref,
                        o_ref, m_ref, l_ref, acc_ref):
    step = pl.program_id(1)
    qi = qi_ref[step]
    ki = ki_ref[step]

    @pl.when(ki == 0)
    def _():
        m_ref[...] = jnp.full_like(m_ref, NEG)
        l_ref[...] = jnp.zeros_like(l_ref)
        acc_ref[...] = jnp.zeros_like(acc_ref)

    def update(hd, mp, q_lo, q_n, key_blocks):
        idx = 2 * hd + mp
        cols = slice(q_lo, q_lo + q_n)
        qm = qtm_ref[0, hd, mp, :, cols]
        logits = []
        for k_lo, k_n, bias_ref in key_blocks:
            s = jnp.dot(kb_ref[0, hd, k_lo:k_lo + k_n, :], qm, preferred_element_type=F32)
            if bias_ref is not None:
                s = s + bias_ref[idx]
            logits.append(s)
        m_prev = m_ref[idx, :, cols]
        m_new = m_prev
        for s in logits:
            m_new = jnp.maximum(m_new, jnp.max(s, axis=0, keepdims=True))
        alpha = jnp.exp(m_prev - m_new)
        l_new = alpha * l_ref[idx, :, cols]
        pv = None
        for (k_lo, k_n, _), s in zip(key_blocks, logits):
            p = jnp.exp(s - m_new)
            l_new = l_new + jnp.sum(p, axis=0, keepdims=True)
            d = jnp.dot(vt_ref[0, hd, :, k_lo:k_lo + k_n], p.astype(BF16), preferred_element_type=F32)
            pv = d if pv is None else pv + d
        acc_ref[idx, :, cols] = alpha * acc_ref[idx, :, cols] + pv
        m_ref[idx, :, cols] = m_new
        l_ref[idx, :, cols] = l_new

    def for_heads(fn):
        def body(hd, carry):
            fn(hd)
            return carry
        lax.fori_loop(0, N_HEADS, body, 0)

    @pl.when(ki < qi - 1)
    def _():
        def far(hd):
            for mp in range(2):
                update(hd, mp, 0, TQ, [(0, TK, None)])
        for_heads(far)

    @pl.when(ki == qi - 1)
    def _():
        def near(hd):
            for mp in range(2):
                update(hd, mp, 0, SUB, [(0, SUB, None), (SUB, SUB, e1_ref)])
                update(hd, mp, SUB, SUB, [(0, TK, None)])
        for_heads(near)

    @pl.when(ki == qi)
    def _():
        lam = _lam(lamv_ref)

        def diag(hd):
            for mp in range(2):
                update(hd, mp, 0, SUB, [(0, SUB, e0_ref)])
                update(hd, mp, SUB, SUB, [(0, SUB, e1_ref), (SUB, SUB, e0_ref)])
            o1 = acc_ref[2 * hd] * (1.0 / l_ref[2 * hd])
            o2 = acc_ref[2 * hd + 1] * (1.0 / l_ref[2 * hd + 1])
            o = o1 - lam * o2
            inv = lax.rsqrt(jnp.mean(o * o, axis=0, keepdims=True) + EPS)
            on = o * inv * subln_ref[...] * (1.0 - LAM_INIT)
            o_ref[0, hd] = on.T.astype(BF16)
        for_heads(diag)


def _attn_prompt(qtm, kb, vt, e0, e1, lamv, subln_col):
    batch, _, _, _, t = qtm.shape
    nq = t // TQ
    pairs = [(q, k) for q in range(nq) for k in range(q + 1)]
    qi_arr = jnp.asarray([p[0] for p in pairs], jnp.int32)
    ki_arr = jnp.asarray([p[1] for p in pairs], jnp.int32)
    full = lambda shape: pl.BlockSpec(shape, lambda b, s, qi, ki: (0,) * len(shape),
                                      pipeline_mode=pl.Buffered(1))
    grid_spec = pltpu.PrefetchScalarGridSpec(
        num_scalar_prefetch=2,
        grid=(batch, len(pairs)),
        in_specs=[
            pl.BlockSpec((1, N_HEADS, 2, V_DIM, TQ), lambda b, s, qi, ki: (b, 0, 0, 0, qi[s])),
            pl.BlockSpec((1, N_HEADS, TK, V_DIM), lambda b, s, qi, ki: (b, 0, ki[s], 0)),
            pl.BlockSpec((1, N_HEADS, V_DIM, TK), lambda b, s, qi, ki: (b, 0, 0, ki[s])),
            full(e0.shape), full(e1.shape), full(lamv.shape), full(subln_col.shape),
        ],
        out_specs=pl.BlockSpec((1, N_HEADS, TQ, V_DIM), lambda b, s, qi, ki: (b, 0, qi[s], 0)),
        scratch_shapes=[
            pltpu.VMEM((N_MAPS, 1, TQ), F32),
            pltpu.VMEM((N_MAPS, 1, TQ), F32),
            pltpu.VMEM((N_MAPS, V_DIM, TQ), F32),
        ],
    )
    return pl.pallas_call(
        _attn_prompt_kernel,
        grid_spec=grid_spec,
        out_shape=jax.ShapeDtypeStruct((batch, N_HEADS, t, V_DIM), BF16),
        compiler_params=pltpu.CompilerParams(dimension_semantics=("arbitrary", "arbitrary"),
                                             vmem_limit_bytes=VMEM_LIMIT),
        name="attn_prompt",
    )(qi_arr, ki_arr, qtm, kb, vt, e0, e1, lamv, subln_col)


def _attn_sample_kernel(pt_ref, q_ref, kn_ref, vn_ref, *rest):
    kp = rest[:PAGES_PER_STEP]
    vp = rest[PAGES_PER_STEP:2 * PAGES_PER_STEP]
    bl_ref, bn_ref, lamv_ref, subln_ref, o_ref, qbd_ref, m_ref, l_ref, acc_ref = rest[2 * PAGES_PER_STEP:]
    j = pl.program_id(1)
    last = pl.num_programs(1) - 1
    n_tok = q_ref.shape[1]
    head_rows = 2 * n_tok

    @pl.when(j == 0)
    def _():
        q = jnp.concatenate([q_ref[0]] * N_MAPS, axis=0)
        r = lax.broadcasted_iota(jnp.int32, q.shape, 0)
        c = lax.broadcasted_iota(jnp.int32, q.shape, 1)
        qbd_ref[...] = jnp.where((r // n_tok) == (c // HEAD_DIM), q, 0.0).astype(BF16)
        m_ref[...] = jnp.full_like(m_ref, NEG)
        l_ref[...] = jnp.zeros_like(l_ref)
        acc_ref[...] = jnp.zeros_like(acc_ref)

    def update(logit_blocks, values_of_head):
        s = jnp.concatenate(logit_blocks, axis=1) if len(logit_blocks) > 1 else logit_blocks[0]
        m_prev = m_ref[...]
        m_new = jnp.maximum(m_prev, jnp.max(s, axis=1, keepdims=True))
        alpha = jnp.exp(m_prev - m_new)
        p = jnp.exp(s - m_new)
        l_ref[...] = alpha * l_ref[...] + jnp.sum(p, axis=1, keepdims=True)
        pv = [jnp.dot(p[hd * head_rows:(hd + 1) * head_rows, :].astype(BF16), values_of_head(hd),
                      preferred_element_type=F32) for hd in range(N_HEADS)]
        acc_ref[...] = alpha * acc_ref[...] + jnp.concatenate(pv, axis=0)
        m_ref[...] = m_new

    qbd = qbd_ref[...]
    logit_blocks = [jnp.dot(qbd, kp[i][0].astype(BF16), preferred_element_type=F32)
                    for i in range(PAGES_PER_STEP)]
    logit_blocks[-1] = logit_blocks[-1] + jnp.where(j == last, bl_ref[...], 0.0)

    def cached_values(hd):
        rows = pl.ds(hd, CHUNK, stride=N_HEADS)
        return jnp.concatenate([vp[i][0, rows, :] for i in range(PAGES_PER_STEP)], axis=0).astype(BF16)

    update(logit_blocks, cached_values)

    @pl.when(j == last)
    def _():
        pad = jnp.zeros((CHUNK - n_tok, ATTN_WIDTH), F32)
        k_new = jnp.concatenate([kn_ref[0], pad], axis=0).astype(BF16)
        v_new = jnp.concatenate([vn_ref[0], pad], axis=0).astype(BF16)
        s_new = lax.dot_general(qbd, k_new, (((1,), (1,)), ((), ())), preferred_element_type=F32)
        update([s_new + bn_ref[...]], lambda hd: v_new[:, hd * V_DIM:(hd + 1) * V_DIM])

        lam = _lam(lamv_ref)
        o_all = acc_ref[...] * (1.0 / l_ref[...])
        for hd in range(N_HEADS):
            r1 = slice(hd * head_rows, hd * head_rows + n_tok)
            r2 = slice(hd * head_rows + n_tok, (hd + 1) * head_rows)
            o = o_all[r1] - lam * o_all[r2]
            o_ref[0, :, hd * V_DIM:(hd + 1) * V_DIM] = (
                _rmsnorm(o, subln_ref[...]) * (1.0 - LAM_INIT)).astype(BF16)


def _attn_sample(page_table, q, k_new, v_new, cache_k, cache_v, bl, bn, lamv, subln_row):
    n_seq, n_tok, _ = q.shape
    n_pages = page_table.shape[1]
    n_steps = n_pages // PAGES_PER_STEP
    n_rows = N_MAPS * n_tok
    tok_spec = pl.BlockSpec((1, n_tok, ATTN_WIDTH), lambda b, j, pt: (b, 0, 0))

    def page_spec(i):
        return pl.BlockSpec((1, ATTN_WIDTH, CHUNK),
                            lambda b, j, pt: (pt[b, j * PAGES_PER_STEP + i], 0, 0))

    full = lambda shape: pl.BlockSpec(shape, lambda b, j, pt: (0,) * len(shape))
    grid_spec = pltpu.PrefetchScalarGridSpec(
        num_scalar_prefetch=1,
        grid=(n_seq, n_steps),
        in_specs=([tok_spec, tok_spec, tok_spec]
                  + [page_spec(i) for i in range(PAGES_PER_STEP)]
                  + [page_spec(i) for i in range(PAGES_PER_STEP)]
                  + [full(bl.shape), full(bn.shape), full(lamv.shape), full(subln_row.shape)]),
        out_specs=tok_spec,
        scratch_shapes=[
            pltpu.VMEM((n_rows, ATTN_WIDTH), BF16),
            pltpu.VMEM((n_rows, 1), F32),
            pltpu.VMEM((n_rows, 1), F32),
            pltpu.VMEM((n_rows, V_DIM), F32),
        ],
    )
    return pl.pallas_call(
        _attn_sample_kernel,
        grid_spec=grid_spec,
        out_shape=jax.ShapeDtypeStruct((n_seq, n_tok, ATTN_WIDTH), BF16),
        compiler_params=pltpu.CompilerParams(dimension_semantics=("arbitrary", "arbitrary"),
                                             vmem_limit_bytes=VMEM_LIMIT),
        name="attn_sample",
    )(page_table, q, k_new, v_new, *([cache_k] * PAGES_PER_STEP), *([cache_v] * PAGES_PER_STEP),
      bl, bn, lamv, subln_row)


def _ffn_weights(w_gu, w_down):
    w_gu = w_gu.astype(BF16)
    split = lambda w: w.reshape(D_MODEL, N_FF_CHUNKS, FF_CHUNK).transpose(1, 0, 2)
    return (split(w_gu[:, :D_FF]), split(w_gu[:, D_FF:]),
            w_down.astype(BF16).reshape(N_FF_CHUNKS, FF_CHUNK, D_MODEL))


def kernel(x_prompt, x_sample, cache_k, cache_v, page_table, rel_table, norm_ffn1, w_ffn1_gu, w_ffn1_down, norm_mix, w_in, lambda_q1, lambda_k1, lambda_q2, lambda_k2, subln, gmlp_norm, w_spatial, b_spatial, w_out, norm_ffn2, w_ffn2_gu, w_ffn2_down, norm_final):
    depth = cache_k.shape[0]
    assert depth == 1
    batch, t_p, _ = x_prompt.shape
    n_seq, t_s, _ = x_sample.shape
    n_pool, page = cache_k.shape[1], cache_k.shape[2]
    assert page == CHUNK and CHUNK % t_s == 0 and (n_seq * t_s) % TOKEN_BLOCK == 0

    row = lambda a: a.reshape(1, -1)
    wg1, wu1, wd1 = _ffn_weights(w_ffn1_gu[0], w_ffn1_down[0])
    wg2, wu2, wd2 = _ffn_weights(w_ffn2_gu[0], w_ffn2_down[0])
    win = w_in[0].astype(BF16)
    wout = w_out[0].astype(BF16)
    lamv = jnp.stack([lambda_q1[0], lambda_k1[0], lambda_q2[0], lambda_k2[0]])
    reps = CHUNK // t_s
    ws_sample = jnp.tile(w_spatial[0][:, :t_s, :t_s], (1, reps, reps))
    bs_sample = jnp.tile(b_spatial[0][:, :t_s], (1, reps)).T
    ffn1 = (row(norm_ffn1[0]), wg1, wu1, wd1)
    mix = (row(norm_mix[0]), win, row(gmlp_norm[0]))

    e0, e1, bl, bn = _bias_tables(rel_table)

    x1p, kp, vp, sp, qtm, kb, vt = _ffn_in(
        x_prompt.reshape(batch * t_p, D_MODEL), *ffn1, *mix, w_spatial[0], b_spatial[0].T,
        batch=batch, seq_in_chunk=CHUNK, prompt=True)
    x1s, ks, vs, ss, qs, gvs = _ffn_in(
        x_sample.reshape(n_seq * t_s, D_MODEL), *ffn1, *mix, ws_sample, bs_sample,
        batch=1, seq_in_chunk=t_s, prompt=False)

    op = _attn_prompt(qtm, kb, vt, e0, e1, lamv, subln[0].reshape(V_DIM, 1))
    os_ = _attn_sample(
        page_table,
        qs.reshape(n_seq, t_s, ATTN_WIDTH), ks.reshape(n_seq, t_s, ATTN_WIDTH),
        vs.reshape(n_seq, t_s, ATTN_WIDTH),
        jnp.transpose(cache_k[0], (0, 2, 3, 1)).reshape(n_pool, ATTN_WIDTH, page),
        cache_v[0].reshape(n_pool, page * N_HEADS, V_DIM),
        bl, bn, lamv, row(subln[0]))

    ffn2 = (wout, row(norm_ffn2[0]), wg2, wu2, wd2, row(norm_final))
    yp = _out_ffn(x1p, op, sp, *ffn2, batch=batch, head_major=True)
    ys = _out_ffn(x1s, os_.reshape(n_seq * t_s, ATTN_WIDTH), ss, *ffn2, batch=1, head_major=False)

    return (
        yp.reshape(batch, t_p, D_MODEL),
        ys.reshape(n_seq, t_s, D_MODEL),
        jnp.transpose(kp.reshape(depth, batch, N_MAPS, HEAD_DIM, t_p), (0, 1, 4, 2, 3)),
        vp.reshape(depth, batch, t_p, N_HEADS, V_DIM),
        ks.reshape(depth, n_seq, t_s, N_MAPS, HEAD_DIM),
        vs.reshape(depth, n_seq, t_s, N_HEADS, V_DIM),
        gvs.reshape(depth, n_seq, t_s, GMLP_WIDTH),
    )
```

```python
import functools
import math

import jax
import jax.numpy as jnp
from jax import lax
from jax.experimental import pallas as pl
from jax.experimental.pallas import tpu as pltpu

F32 = jnp.float32
BF16 = jnp.bfloat16

D_MODEL = 1024
D_FF = 2816
FF_CHUNK = 256
N_FF_CHUNKS = D_FF // FF_CHUNK
ATTN_WIDTH = 512
HEAD_DIM = 64
N_HEADS = 4
N_MAPS = 2 * N_HEADS
V_DIM = 2 * HEAD_DIM
GMLP_WIDTH = 512
N_GROUPS = 4
GROUP = GMLP_WIDTH // N_GROUPS
CHUNK = 128
N_BUCKETS = 32
MAX_EXACT = N_BUCKETS // 2
MAX_DISTANCE = 128
EPS = 1e-6
NEG = -1e30
LAM_INIT = 0.8 - 0.6 * math.exp(-0.3 * 0)
LOG2E = math.log2(math.e)

TOKEN_BLOCK = 256
TQ = 512
TK = 512
SUB = 256
VT_ROWS = V_DIM + 16
PAGES_PER_STEP = 8
VMEM_LIMIT = 52 * 1024 * 1024


def _const_spec(shape):
    nd = len(shape)
    return pl.BlockSpec(shape, lambda *_: (0,) * nd, pipeline_mode=pl.Buffered(1))


def _rmsnorm(x, g):
    return x * lax.rsqrt(jnp.mean(x * x, axis=-1, keepdims=True) + EPS) * g


def _lam(lamv_ref):
    a = jnp.sum(lamv_ref[0:1, :] * lamv_ref[1:2, :], axis=1, keepdims=True)
    b = jnp.sum(lamv_ref[2:3, :] * lamv_ref[3:4, :], axis=1, keepdims=True)
    return jnp.exp(a) - jnp.exp(b) + LAM_INIT


def _bias_of_distance(d, tab_ref, m):
    n = jnp.maximum(d, 0)
    nf = jnp.maximum(n, 1).astype(F32)
    large = MAX_EXACT + (jnp.log(nf / MAX_EXACT) / math.log(MAX_DISTANCE / MAX_EXACT)
                         * (N_BUCKETS - MAX_EXACT)).astype(jnp.int32)
    large = jnp.minimum(large, N_BUCKETS - 1)
    bucket = jnp.where(n < MAX_EXACT, n, large)
    val = jnp.zeros(d.shape, F32)
    for b in range(N_BUCKETS):
        val = jnp.where(bucket == b, tab_ref[b, m], val)
    val = (val - tab_ref[N_BUCKETS - 1, m]) * LOG2E
    return jnp.where(d >= 0, val, NEG)


def _bias_kernel(tab_ref, e0_ref, e1_ref, bl_ref, bn_ref):
    m = pl.program_id(0)
    key = lax.broadcasted_iota(jnp.int32, (SUB, SUB), 0)
    qry = lax.broadcasted_iota(jnp.int32, (SUB, SUB), 1)
    e0_ref[0] = _bias_of_distance(qry - key, tab_ref, m)
    e1_ref[0] = _bias_of_distance(SUB + qry - key, tab_ref, m)
    t = lax.broadcasted_iota(jnp.int32, (8, CHUNK), 0)
    c = lax.broadcasted_iota(jnp.int32, (8, CHUNK), 1)
    bl_ref[...] = _bias_of_distance(CHUNK + t - c, tab_ref, m)
    bn_ref[...] = _bias_of_distance(t - c, tab_ref, m)


def _bias_tables(rel_table):
    return pl.pallas_call(
        _bias_kernel,
        grid=(N_MAPS,),
        in_specs=[pl.BlockSpec(memory_space=pltpu.SMEM)],
        out_specs=[
            pl.BlockSpec((1, SUB, SUB), lambda m: (m, 0, 0)),
            pl.BlockSpec((1, SUB, SUB), lambda m: (m, 0, 0)),
            pl.BlockSpec((8, CHUNK), lambda m: (m, 0)),
            pl.BlockSpec((8, CHUNK), lambda m: (m, 0)),
        ],
        out_shape=[
            jax.ShapeDtypeStruct((N_MAPS, SUB, SUB), F32),
            jax.ShapeDtypeStruct((N_MAPS, SUB, SUB), F32),
            jax.ShapeDtypeStruct((N_MAPS * 8, CHUNK), F32),
            jax.ShapeDtypeStruct((N_MAPS * 8, CHUNK), F32),
        ],
        name="bias_tables",
    )(rel_table)


def _swiglu_half_step(x, norm_ref, wg_ref, wu_ref, wd_ref, hn_ref, acc_ref):
    hn_ref[...] = _rmsnorm(x, norm_ref[...]).astype(BF16)
    acc_ref[...] = jnp.zeros_like(acc_ref)

    def body(c, carry):
        hn = hn_ref[...]
        gate = jnp.dot(hn, wg_ref[c], preferred_element_type=F32)
        up = jnp.dot(hn, wu_ref[c], preferred_element_type=F32)
        act = (jax.nn.silu(gate) * up).astype(BF16)
        acc_ref[...] += jnp.dot(act, wd_ref[c], preferred_element_type=F32)
        return carry

    lax.fori_loop(0, N_FF_CHUNKS, body, 0)
    return x + 0.5 * acc_ref[...]


def _ffn_in_kernel(x_ref, nf_ref, wg_ref, wu_ref, wd_ref, nm_ref, win_ref, gn_ref, ws_ref, bs_ref,
                   *rest, prompt, seq_in_chunk):
    if prompt:
        x1_ref, k_ref, v_ref, s_ref, qtm_ref, kb_ref, vt_ref, hn_ref, acc_ref = rest
    else:
        x1_ref, k_ref, v_ref, s_ref, q_ref, gv_ref, hn_ref, acc_ref = rest
    tm = x_ref.shape[0]

    x1 = _swiglu_half_step(x_ref[...], nf_ref, wg_ref, wu_ref, wd_ref, hn_ref, acc_ref)
    x1_ref[...] = x1
    h = _rmsnorm(x1, nm_ref[...]).astype(BF16)

    def proj(i):
        return jnp.dot(h, win_ref[:, i * ATTN_WIDTH:(i + 1) * ATTN_WIDTH], preferred_element_type=F32)

    q = proj(0) * (HEAD_DIM ** -0.5 * LOG2E)
    k = proj(1)
    v = proj(2)
    u = jax.nn.gelu(proj(3))
    g = jax.nn.gelu(proj(4))
    k_ref[...] = k.T[None] if prompt else k
    v_ref[...] = v

    row = lax.broadcasted_iota(jnp.int32, (CHUNK, CHUNK), 0)
    col = lax.broadcasted_iota(jnp.int32, (CHUNK, CHUNK), 1)
    keep = col <= row
    if seq_in_chunk < CHUNK:
        keep = keep & ((row // seq_in_chunk) == (col // seq_in_chunk))
    for grp in range(N_GROUPS):
        lanes = slice(grp * GROUP, (grp + 1) * GROUP)
        gg = g[:, lanes]
        gvn = gg * lax.rsqrt(jnp.mean(gg * gg, axis=-1, keepdims=True) + EPS) * gn_ref[:, lanes]
        if not prompt:
            gv_ref[:, lanes] = gvn
        wm = jnp.where(keep, ws_ref[grp], 0.0).astype(BF16)
        bcol = bs_ref[:, grp:grp + 1]
        for c in range(tm // CHUNK):
            rows = slice(c * CHUNK, (c + 1) * CHUNK)
            mixed = jnp.dot(wm, gvn[rows, :].astype(BF16), preferred_element_type=F32) + bcol
            s_ref[rows, lanes] = (u[rows, lanes] * mixed).astype(BF16)

    if prompt:
        for hd in range(N_HEADS):
            lanes = slice(hd * V_DIM, (hd + 1) * V_DIM)
            qt = q[:, lanes].T
            feat = lax.broadcasted_iota(jnp.int32, qt.shape, 0)
            qtm_ref[0, hd, 0] = jnp.where(feat < HEAD_DIM, qt, 0.0).astype(BF16)
            qtm_ref[0, hd, 1] = jnp.where(feat >= HEAD_DIM, qt, 0.0).astype(BF16)
            kb_ref[0, hd] = k[:, lanes].astype(BF16)
            vt_ref[0, hd, 0:V_DIM] = v[:, lanes].T.astype(BF16)
            ones_row = lax.broadcasted_iota(jnp.int32, (VT_ROWS - V_DIM, tm), 0) == 0
            vt_ref[0, hd, V_DIM:VT_ROWS] = ones_row.astype(F32).astype(BF16)
    else:
        q_ref[...] = q


def _ffn_in(x, nf, wg, wu, wd, nm, win, gn, ws, bs, *, batch, seq_in_chunk, prompt):
    n = x.shape[0]
    tm = TOKEN_BLOCK
    nb = n // tm
    per_batch = nb // batch
    tok = lambda w: pl.BlockSpec((tm, w), lambda i: (i, 0))
    in_specs = [
        tok(D_MODEL),
        _const_spec((1, D_MODEL)),
        _const_spec(wg.shape), _const_spec(wu.shape), _const_spec(wd.shape),
        _const_spec((1, D_MODEL)),
        _const_spec(win.shape),
        _const_spec((1, GMLP_WIDTH)),
        _const_spec(ws.shape), _const_spec(bs.shape),
    ]
    out_specs = [tok(D_MODEL), tok(ATTN_WIDTH), tok(ATTN_WIDTH), tok(GMLP_WIDTH)]
    out_shape = [
        jax.ShapeDtypeStruct((n, D_MODEL), F32),
        jax.ShapeDtypeStruct((n, ATTN_WIDTH), F32),
        jax.ShapeDtypeStruct((n, ATTN_WIDTH), F32),
        jax.ShapeDtypeStruct((n, GMLP_WIDTH), BF16),
    ]
    if prompt:
        t = n // batch
        out_specs[1] = pl.BlockSpec((1, ATTN_WIDTH, tm), lambda i: (i // per_batch, 0, i % per_batch))
        out_shape[1] = jax.ShapeDtypeStruct((batch, ATTN_WIDTH, t), F32)
        out_specs += [
            pl.BlockSpec((1, N_HEADS, 2, V_DIM, tm), lambda i: (i // per_batch, 0, 0, 0, i % per_batch)),
            pl.BlockSpec((1, N_HEADS, tm, V_DIM), lambda i: (i // per_batch, 0, i % per_batch, 0)),
            pl.BlockSpec((1, N_HEADS, VT_ROWS, tm), lambda i: (i // per_batch, 0, 0, i % per_batch)),
        ]
        out_shape += [
            jax.ShapeDtypeStruct((batch, N_HEADS, 2, V_DIM, t), BF16),
            jax.ShapeDtypeStruct((batch, N_HEADS, t, V_DIM), BF16),
            jax.ShapeDtypeStruct((batch, N_HEADS, VT_ROWS, t), BF16),
        ]
    else:
        out_specs += [tok(ATTN_WIDTH), tok(GMLP_WIDTH)]
        out_shape += [
            jax.ShapeDtypeStruct((n, ATTN_WIDTH), F32),
            jax.ShapeDtypeStruct((n, GMLP_WIDTH), F32),
        ]
    return pl.pallas_call(
        functools.partial(_ffn_in_kernel, prompt=prompt, seq_in_chunk=seq_in_chunk),
        grid=(nb,),
        in_specs=in_specs,
        out_specs=out_specs,
        out_shape=out_shape,
        scratch_shapes=[pltpu.VMEM((tm, D_MODEL), BF16), pltpu.VMEM((tm, D_MODEL), F32)],
        compiler_params=pltpu.CompilerParams(dimension_semantics=("arbitrary",),
                                             vmem_limit_bytes=VMEM_LIMIT),
        name="ffn_in_prompt" if prompt else "ffn_in_sample",
    )(x, nf, wg, wu, wd, nm, win, gn, ws, bs)


def _out_ffn_kernel(x1_ref, o_ref, s_ref, wout_ref, nf_ref, wg_ref, wu_ref, wd_ref, nfin_ref,
                    y_ref, hn_ref, acc_ref, *, head_major):
    if head_major:
        o = jnp.concatenate([o_ref[0, hd] for hd in range(N_HEADS)], axis=1)
    else:
        o = o_ref[...]
    x2 = (x1_ref[...]
          + jnp.dot(o, wout_ref[0:ATTN_WIDTH, :], preferred_element_type=F32)
          + jnp.dot(s_ref[...], wout_ref[ATTN_WIDTH:D_MODEL, :], preferred_element_type=F32))
    x3 = _swiglu_half_step(x2, nf_ref, wg_ref, wu_ref, wd_ref, hn_ref, acc_ref)
    y_ref[...] = _rmsnorm(x3, nfin_ref[...])


def _out_ffn(x1, o, s, wout, nf, wg, wu, wd, nfin, *, batch, head_major):
    n = x1.shape[0]
    tm = TOKEN_BLOCK
    nb = n // tm
    per_batch = nb // batch
    tok = lambda w: pl.BlockSpec((tm, w), lambda i: (i, 0))
    if head_major:
        o_spec = pl.BlockSpec((1, N_HEADS, tm, V_DIM), lambda i: (i // per_batch, 0, i % per_batch, 0))
    else:
        o_spec = tok(ATTN_WIDTH)
    return pl.pallas_call(
        functools.partial(_out_ffn_kernel, head_major=head_major),
        grid=(nb,),
        in_specs=[
            tok(D_MODEL), o_spec, tok(GMLP_WIDTH),
            _const_spec(wout.shape),
            _const_spec((1, D_MODEL)),
            _const_spec(wg.shape), _const_spec(wu.shape), _const_spec(wd.shape),
            _const_spec((1, D_MODEL)),
        ],
        out_specs=tok(D_MODEL),
        out_shape=jax.ShapeDtypeStruct((n, D_MODEL), F32),
        scratch_shapes=[pltpu.VMEM((tm, D_MODEL), BF16), pltpu.VMEM((tm, D_MODEL), F32)],
        compiler_params=pltpu.CompilerParams(dimension_semantics=("arbitrary",),
                                             vmem_limit_bytes=VMEM_LIMIT),
        name="out_ffn_prompt" if head_major else "out_ffn_sample",
    )(x1, o, s, wout, nf, wg, wu, wd, nfin)


def _attn_prompt_kernel(qi_ref, ki_ref, qtm_ref, kb_ref, vt_ref, e0_ref, e1_ref, lamv_ref, subln_ref,
                        o_ref, m_ref, acc_ref):
    step = pl.program_id(1)
    qi = qi_ref[step]
    ki = ki_ref[step]

    @pl.when(ki == 0)
    def _():
        m_ref[...] = jnp.full_like(m_ref, NEG)
        acc_ref[...] = jnp.zeros_like(acc_ref)


    def unit_logits(unit):
        hd, mp, q_lo, q_n, key_blocks = unit
        idx = 2 * hd + mp
        qm = qtm_ref[0, hd, mp, :, q_lo:q_lo + q_n]
        logits = []
        for k_lo, k_n, bias_ref in key_blocks:
            s = jnp.dot(kb_ref[0, hd, k_lo:k_lo + k_n, :], qm, preferred_element_type=F32)
            if bias_ref is not None:
                s = s + bias_ref[idx]
            logits.append(s)
        return logits

    def unit_update(unit, logits):
        hd, mp, q_lo, q_n, key_blocks = unit
        idx = 2 * hd + mp
        cols = slice(q_lo, q_lo + q_n)
        m_prev = m_ref[idx, :, cols]
        m_new = m_prev
        for s in logits:
            m_new = jnp.maximum(m_new, jnp.max(s, axis=0, keepdims=True))
        alpha = jnp.exp2(m_prev - m_new)
        pv = None
        for (k_lo, k_n, _), s in zip(key_blocks, logits):
            p = jnp.exp2(s - m_new)
            d = jnp.dot(vt_ref[0, hd, :, k_lo:k_lo + k_n], p.astype(BF16), preferred_element_type=F32)
            pv = d if pv is None else pv + d
        acc_ref[idx, :, cols] = alpha * acc_ref[idx, :, cols] + pv
        m_ref[idx, :, cols] = m_new

    def run(units):
        pending = None
        for unit in units:
            logits = unit_logits(unit)
            if pending is not None:
                unit_update(*pending)
            pending = (unit, logits)
        unit_update(*pending)

    maps = [(hd, mp) for hd in range(N_HEADS) for mp in range(2)]

    @pl.when(ki < qi - 1)
    def _():
        run([(hd, mp, 0, TQ, [(0, TK, None)]) for hd, mp in maps])

    @pl.when(ki == qi - 1)
    def _():
        run([u for hd, mp in maps for u in (
            (hd, mp, 0, SUB, [(0, SUB, None), (SUB, SUB, e1_ref)]),
            (hd, mp, SUB, SUB, [(0, TK, None)]))])

    @pl.when(ki == qi)
    def _():
        run([u for hd, mp in maps for u in (
            (hd, mp, 0, SUB, [(0, SUB, e0_ref)]),
            (hd, mp, SUB, SUB, [(0, SUB, e1_ref), (SUB, SUB, e0_ref)]))])
        lam = _lam(lamv_ref)
        for hd in range(N_HEADS):
            o1 = acc_ref[2 * hd, 0:V_DIM] * (1.0 / acc_ref[2 * hd, V_DIM:V_DIM + 1])
            o2 = acc_ref[2 * hd + 1, 0:V_DIM] * (1.0 / acc_ref[2 * hd + 1, V_DIM:V_DIM + 1])
            o = o1 - lam * o2
            inv = lax.rsqrt(jnp.mean(o * o, axis=0, keepdims=True) + EPS)
            on = o * inv * subln_ref[...] * (1.0 - LAM_INIT)
            o_ref[0, hd] = on.T.astype(BF16)


def _attn_prompt(qtm, kb, vt, e0, e1, lamv, subln_col):
    batch, _, _, _, t = qtm.shape
    nq = t // TQ
    pairs = [(q, k) for q in range(nq) for k in range(q + 1)]
    qi_arr = jnp.asarray([p[0] for p in pairs], jnp.int32)
    ki_arr = jnp.asarray([p[1] for p in pairs], jnp.int32)
    full = lambda shape: pl.BlockSpec(shape, lambda b, s, qi, ki: (0,) * len(shape),
                                      pipeline_mode=pl.Buffered(1))
    grid_spec = pltpu.PrefetchScalarGridSpec(
        num_scalar_prefetch=2,
        grid=(batch, len(pairs)),
        in_specs=[
            pl.BlockSpec((1, N_HEADS, 2, V_DIM, TQ), lambda b, s, qi, ki: (b, 0, 0, 0, qi[s])),
            pl.BlockSpec((1, N_HEADS, TK, V_DIM), lambda b, s, qi, ki: (b, 0, ki[s], 0)),
            pl.BlockSpec((1, N_HEADS, VT_ROWS, TK), lambda b, s, qi, ki: (b, 0, 0, ki[s])),
            full(e0.shape), full(e1.shape), full(lamv.shape), full(subln_col.shape),
        ],
        out_specs=pl.BlockSpec((1, N_HEADS, TQ, V_DIM), lambda b, s, qi, ki: (b, 0, qi[s], 0)),
        scratch_shapes=[
            pltpu.VMEM((N_MAPS, 1, TQ), F32),
            pltpu.VMEM((N_MAPS, VT_ROWS, TQ), F32),
        ],
    )
    return pl.pallas_call(
        _attn_prompt_kernel,
        grid_spec=grid_spec,
        out_shape=jax.ShapeDtypeStruct((batch, N_HEADS, t, V_DIM), BF16),
        compiler_params=pltpu.CompilerParams(dimension_semantics=("arbitrary", "arbitrary"),
                                             vmem_limit_bytes=VMEM_LIMIT),
        name="attn_prompt",
    )(qi_arr, ki_arr, qtm, kb, vt, e0, e1, lamv, subln_col)


def _attn_sample_kernel(pt_ref, q_ref, kn_ref, vn_ref, *rest):
    kp = rest[:PAGES_PER_STEP]
    vp = rest[PAGES_PER_STEP:2 * PAGES_PER_STEP]
    bl_ref, bn_ref, lamv_ref, subln_ref, o_ref, qbd_ref, m_ref, l_ref, acc_ref = rest[2 * PAGES_PER_STEP:]
    j = pl.program_id(1)
    last = pl.num_programs(1) - 1
    n_tok = q_ref.shape[1]
    head_rows = 2 * n_tok

    @pl.when(j == 0)
    def _():
        q = jnp.concatenate([q_ref[0]] * N_MAPS, axis=0)
        r = lax.broadcasted_iota(jnp.int32, q.shape, 0)
        c = lax.broadcasted_iota(jnp.int32, q.shape, 1)
        qbd_ref[...] = jnp.where((r // n_tok) == (c // HEAD_DIM), q, 0.0).astype(BF16)
        m_ref[...] = jnp.full_like(m_ref, NEG)
        l_ref[...] = jnp.zeros_like(l_ref)
        acc_ref[...] = jnp.zeros_like(acc_ref)

    def update(logit_blocks, values_of_head):
        s = jnp.concatenate(logit_blocks, axis=1) if len(logit_blocks) > 1 else logit_blocks[0]
        m_prev = m_ref[...]
        m_new = jnp.maximum(m_prev, jnp.max(s, axis=1, keepdims=True))
        alpha = jnp.exp2(m_prev - m_new)
        p = jnp.exp2(s - m_new)
        l_ref[...] = alpha * l_ref[...] + jnp.sum(p, axis=1, keepdims=True)
        pv = [jnp.dot(p[hd * head_rows:(hd + 1) * head_rows, :].astype(BF16), values_of_head(hd),
                      preferred_element_type=F32) for hd in range(N_HEADS)]
        acc_ref[...] = alpha * acc_ref[...] + jnp.concatenate(pv, axis=0)
        m_ref[...] = m_new

    qbd = qbd_ref[...]
    logit_blocks = [jnp.dot(qbd, kp[i][0].astype(BF16), preferred_element_type=F32)
                    for i in range(PAGES_PER_STEP)]
    logit_blocks[-1] = logit_blocks[-1] + jnp.where(j == last, bl_ref[...], 0.0)

    def cached_values(hd):
        rows = pl.ds(hd, CHUNK, stride=N_HEADS)
        return jnp.concatenate([vp[i][0, rows, :] for i in range(PAGES_PER_STEP)], axis=0).astype(BF16)

    update(logit_blocks, cached_values)

    @pl.when(j == last)
    def _():
        pad = jnp.zeros((CHUNK - n_tok, ATTN_WIDTH), F32)
        k_new = jnp.concatenate([kn_ref[0], pad], axis=0).astype(BF16)
        v_new = jnp.concatenate([vn_ref[0], pad], axis=0).astype(BF16)
        s_new = lax.dot_general(qbd, k_new, (((1,), (1,)), ((), ())), preferred_element_type=F32)
        update([s_new + bn_ref[...]], lambda hd: v_new[:, hd * V_DIM:(hd + 1) * V_DIM])

        lam = _lam(lamv_ref)
        o_all = acc_ref[...] * (1.0 / l_ref[...])
        for hd in range(N_HEADS):
            r1 = slice(hd * head_rows, hd * head_rows + n_tok)
            r2 = slice(hd * head_rows + n_tok, (hd + 1) * head_rows)
            o = o_all[r1] - lam * o_all[r2]
            o_ref[0, :, hd * V_DIM:(hd + 1) * V_DIM] = (
                _rmsnorm(o, subln_ref[...]) * (1.0 - LAM_INIT)).astype(BF16)


def _attn_sample(page_table, q, k_new, v_new, cache_k, cache_v, bl, bn, lamv, subln_row):
    n_seq, n_tok, _ = q.shape
    n_pages = page_table.shape[1]
    n_steps = n_pages // PAGES_PER_STEP
    n_rows = N_MAPS * n_tok
    tok_spec = pl.BlockSpec((1, n_tok, ATTN_WIDTH), lambda b, j, pt: (b, 0, 0))

    def page_spec(i):
        return pl.BlockSpec((1, ATTN_WIDTH, CHUNK),
                            lambda b, j, pt: (pt[b, j * PAGES_PER_STEP + i], 0, 0))

    full = lambda shape: pl.BlockSpec(shape, lambda b, j, pt: (0,) * len(shape))
    grid_spec = pltpu.PrefetchScalarGridSpec(
        num_scalar_prefetch=1,
        grid=(n_seq, n_steps),
        in_specs=([tok_spec, tok_spec, tok_spec]
                  + [page_spec(i) for i in range(PAGES_PER_STEP)]
                  + [page_spec(i) for i in range(PAGES_PER_STEP)]
                  + [full(bl.shape), full(bn.shape), full(lamv.shape), full(subln_row.shape)]),
        out_specs=tok_spec,
        scratch_shapes=[
            pltpu.VMEM((n_rows, ATTN_WIDTH), BF16),
            pltpu.VMEM((n_rows, 1), F32),
            pltpu.VMEM((n_rows, 1), F32),
            pltpu.VMEM((n_rows, V_DIM), F32),
        ],
    )
    return pl.pallas_call(
        _attn_sample_kernel,
        grid_spec=grid_spec,
        out_shape=jax.ShapeDtypeStruct((n_seq, n_tok, ATTN_WIDTH), BF16),
        compiler_params=pltpu.CompilerParams(dimension_semantics=("arbitrary", "arbitrary"),
                                             vmem_limit_bytes=VMEM_LIMIT),
        name="attn_sample",
    )(page_table, q, k_new, v_new, *([cache_k] * PAGES_PER_STEP), *([cache_v] * PAGES_PER_STEP),
      bl, bn, lamv, subln_row)


def _ffn_weights(w_gu, w_down):
    w_gu = w_gu.astype(BF16)
    split = lambda w: w.reshape(D_MODEL, N_FF_CHUNKS, FF_CHUNK).transpose(1, 0, 2)
    return (split(w_gu[:, :D_FF]), split(w_gu[:, D_FF:]),
            w_down.astype(BF16).reshape(N_FF_CHUNKS, FF_CHUNK, D_MODEL))


def kernel(x_prompt, x_sample, cache_k, cache_v, page_table, rel_table, norm_ffn1, w_ffn1_gu, w_ffn1_down, norm_mix, w_in, lambda_q1, lambda_k1, lambda_q2, lambda_k2, subln, gmlp_norm, w_spatial, b_spatial, w_out, norm_ffn2, w_ffn2_gu, w_ffn2_down, norm_final):
    depth = cache_k.shape[0]
    assert depth == 1
    batch, t_p, _ = x_prompt.shape
    n_seq, t_s, _ = x_sample.shape
    n_pool, page = cache_k.shape[1], cache_k.shape[2]
    assert page == CHUNK and CHUNK % t_s == 0 and (n_seq * t_s) % TOKEN_BLOCK == 0

    row = lambda a: a.reshape(1, -1)
    wg1, wu1, wd1 = _ffn_weights(w_ffn1_gu[0], w_ffn1_down[0])
    wg2, wu2, wd2 = _ffn_weights(w_ffn2_gu[0], w_ffn2_down[0])
    win = w_in[0].astype(BF16)
    wout = w_out[0].astype(BF16)
    lamv = jnp.stack([lambda_q1[0], lambda_k1[0], lambda_q2[0], lambda_k2[0]])
    reps = CHUNK // t_s
    ws_sample = jnp.tile(w_spatial[0][:, :t_s, :t_s], (1, reps, reps))
    bs_sample = jnp.tile(b_spatial[0][:, :t_s], (1, reps)).T
    ffn1 = (row(norm_ffn1[0]), wg1, wu1, wd1)
    mix = (row(norm_mix[0]), win, row(gmlp_norm[0]))

    e0, e1, bl, bn = _bias_tables(rel_table)

    x1p, kp, vp, sp, qtm, kb, vt = _ffn_in(
        x_prompt.reshape(batch * t_p, D_MODEL), *ffn1, *mix, w_spatial[0], b_spatial[0].T,
        batch=batch, seq_in_chunk=CHUNK, prompt=True)
    x1s, ks, vs, ss, qs, gvs = _ffn_in(
        x_sample.reshape(n_seq * t_s, D_MODEL), *ffn1, *mix, ws_sample, bs_sample,
        batch=1, seq_in_chunk=t_s, prompt=False)

    op = _attn_prompt(qtm, kb, vt, e0, e1, lamv, subln[0].reshape(V_DIM, 1))
    os_ = _attn_sample(
        page_table,
        qs.reshape(n_seq, t_s, ATTN_WIDTH), ks.reshape(n_seq, t_s, ATTN_WIDTH),
        vs.reshape(n_seq, t_s, ATTN_WIDTH),
        jnp.transpose(cache_k[0], (0, 2, 3, 1)).reshape(n_pool, ATTN_WIDTH, page),
        cache_v[0].reshape(n_pool, page * N_HEADS, V_DIM),
        bl, bn, lamv, row(subln[0]))

    ffn2 = (wout, row(norm_ffn2[0]), wg2, wu2, wd2, row(norm_final))
    yp = _out_ffn(x1p, op, sp, *ffn2, batch=batch, head_major=True)
    ys = _out_ffn(x1s, os_.reshape(n_seq * t_s, ATTN_WIDTH), ss, *ffn2, batch=1, head_major=False)

    return (
        yp.reshape(batch, t_p, D_MODEL),
        ys.reshape(n_seq, t_s, D_MODEL),
        jnp.transpose(kp.reshape(depth, batch, N_MAPS, HEAD_DIM, t_p), (0, 1, 4, 2, 3)),
        vp.reshape(depth, batch, t_p, N_HEADS, V_DIM),
        ks.reshape(depth, n_seq, t_s, N_MAPS, HEAD_DIM),
        vs.reshape(depth, n_seq, t_s, N_HEADS, V_DIM),
        gvs.reshape(depth, n_seq, t_s, GMLP_WIDTH),
    )
```

```python
import functools
import math

import jax
import jax.numpy as jnp
from jax import lax
from jax.experimental import pallas as pl
from jax.experimental.pallas import tpu as pltpu

F32 = jnp.float32
BF16 = jnp.bfloat16

D_MODEL = 1024
D_FF = 2816
FF_CHUNK = 256
N_FF_CHUNKS = D_FF // FF_CHUNK
ATTN_WIDTH = 512
HEAD_DIM = 64
N_HEADS = 4
N_MAPS = 2 * N_HEADS
V_DIM = 2 * HEAD_DIM
GMLP_WIDTH = 512
N_GROUPS = 4
GROUP = GMLP_WIDTH // N_GROUPS
CHUNK = 128
N_BUCKETS = 32
MAX_EXACT = N_BUCKETS // 2
MAX_DISTANCE = 128
EPS = 1e-6
NEG = -1e30
LAM_INIT = 0.8 - 0.6 * math.exp(-0.3 * 0)
LOG2E = math.log2(math.e)

TOKEN_BLOCK = 256
TQ = 512
TK = 512
SUB = 256
VT_ROWS = V_DIM + 16
PAGES_PER_STEP = 16
VMEM_LIMIT = 52 * 1024 * 1024


def _const_spec(shape):
    nd = len(shape)
    return pl.BlockSpec(shape, lambda *_: (0,) * nd, pipeline_mode=pl.Buffered(1))


def _gate_up_specs():
    half = lambda j: pl.BlockSpec((D_MODEL, D_FF), lambda *_: (0, j), pipeline_mode=pl.Buffered(1))
    return [half(0), half(1)]


def _rmsnorm(x, g):
    return x * lax.rsqrt(jnp.mean(x * x, axis=-1, keepdims=True) + EPS) * g


def _lam(lamv_ref):
    a = jnp.sum(lamv_ref[0:1, :] * lamv_ref[1:2, :], axis=1, keepdims=True)
    b = jnp.sum(lamv_ref[2:3, :] * lamv_ref[3:4, :], axis=1, keepdims=True)
    return jnp.exp(a) - jnp.exp(b) + LAM_INIT


def _bias_of_distance(d, tab_ref, m):
    n = jnp.maximum(d, 0)
    nf = jnp.maximum(n, 1).astype(F32)
    large = MAX_EXACT + (jnp.log(nf / MAX_EXACT) / math.log(MAX_DISTANCE / MAX_EXACT)
                         * (N_BUCKETS - MAX_EXACT)).astype(jnp.int32)
    large = jnp.minimum(large, N_BUCKETS - 1)
    bucket = jnp.where(n < MAX_EXACT, n, large)
    val = jnp.zeros(d.shape, F32)
    for b in range(N_BUCKETS):
        val = jnp.where(bucket == b, tab_ref[b, m], val)
    val = (val - tab_ref[N_BUCKETS - 1, m]) * LOG2E
    return jnp.where(d >= 0, val, NEG)


def _bias_kernel(tab_ref, e0_ref, e1_ref, bl_ref, bn_ref):
    m = pl.program_id(0)
    key = lax.broadcasted_iota(jnp.int32, (SUB, SUB), 0)
    qry = lax.broadcasted_iota(jnp.int32, (SUB, SUB), 1)
    e0_ref[0] = _bias_of_distance(qry - key, tab_ref, m)
    e1_ref[0] = _bias_of_distance(SUB + qry - key, tab_ref, m)
    t = lax.broadcasted_iota(jnp.int32, (8, CHUNK), 0)
    c = lax.broadcasted_iota(jnp.int32, (8, CHUNK), 1)
    bl_ref[...] = _bias_of_distance(CHUNK + t - c, tab_ref, m)
    bn_ref[...] = _bias_of_distance(t - c, tab_ref, m)


def _bias_tables(rel_table):
    return pl.pallas_call(
        _bias_kernel,
        grid=(N_MAPS,),
        in_specs=[pl.BlockSpec(memory_space=pltpu.SMEM)],
        out_specs=[
            pl.BlockSpec((1, SUB, SUB), lambda m: (m, 0, 0)),
            pl.BlockSpec((1, SUB, SUB), lambda m: (m, 0, 0)),
            pl.BlockSpec((8, CHUNK), lambda m: (m, 0)),
            pl.BlockSpec((8, CHUNK), lambda m: (m, 0)),
        ],
        out_shape=[
            jax.ShapeDtypeStruct((N_MAPS, SUB, SUB), F32),
            jax.ShapeDtypeStruct((N_MAPS, SUB, SUB), F32),
            jax.ShapeDtypeStruct((N_MAPS * 8, CHUNK), F32),
            jax.ShapeDtypeStruct((N_MAPS * 8, CHUNK), F32),
        ],
        name="bias_tables",
    )(rel_table)


def _swiglu_half_step(x, norm_ref, wg_ref, wu_ref, wd_ref, hn_ref, acc_ref):
    hn_ref[...] = _rmsnorm(x, norm_ref[...]).astype(BF16)

    def gate_up(c):
        hn = hn_ref[...]
        cols = slice(c * FF_CHUNK, (c + 1) * FF_CHUNK)
        return (jnp.dot(hn, wg_ref[:, cols], preferred_element_type=F32),
                jnp.dot(hn, wu_ref[:, cols], preferred_element_type=F32))

    def down(c, gate, up):
        act = (jax.nn.silu(gate) * up).astype(BF16)
        d = jnp.dot(act, wd_ref[c * FF_CHUNK:(c + 1) * FF_CHUNK, :], preferred_element_type=F32)
        if c == 0:
            acc_ref[...] = d
        else:
            acc_ref[...] += d

    pending = gate_up(0)
    for c in range(N_FF_CHUNKS):
        nxt = gate_up(c + 1) if c + 1 < N_FF_CHUNKS else None
        down(c, *pending)
        pending = nxt
    return x + 0.5 * acc_ref[...]


def _ffn_in_kernel(x_ref, nf_ref, wg_ref, wu_ref, wd_ref, nm_ref, win_ref, gn_ref, ws_ref, bs_ref,
                   *rest, prompt, seq_in_chunk):
    if prompt:
        x1_ref, k_ref, v_ref, s_ref, qtm_ref, kb_ref, vt_ref, hn_ref, acc_ref = rest
    else:
        x1_ref, k_ref, v_ref, s_ref, q_ref, gv_ref, hn_ref, acc_ref = rest
    tm = x_ref.shape[0]

    x1 = _swiglu_half_step(x_ref[...], nf_ref, wg_ref, wu_ref, wd_ref, hn_ref, acc_ref)
    x1_ref[...] = x1
    h = _rmsnorm(x1, nm_ref[...]).astype(BF16)

    def proj(i):
        return jnp.dot(h, win_ref[:, i * ATTN_WIDTH:(i + 1) * ATTN_WIDTH], preferred_element_type=F32)

    q = proj(0) * (HEAD_DIM ** -0.5 * LOG2E)
    k = proj(1)
    v = proj(2)
    u = jax.nn.gelu(proj(3))
    g = jax.nn.gelu(proj(4))
    k_ref[...] = k.T[None] if prompt else k
    if prompt:
        for hd in range(N_HEADS):
            v_ref[pl.ds(hd, tm, stride=N_HEADS), :] = v[:, hd * V_DIM:(hd + 1) * V_DIM]
    else:
        v_ref[...] = v

    row = lax.broadcasted_iota(jnp.int32, (CHUNK, CHUNK), 0)
    col = lax.broadcasted_iota(jnp.int32, (CHUNK, CHUNK), 1)
    keep = col <= row
    if seq_in_chunk < CHUNK:
        keep = keep & ((row // seq_in_chunk) == (col // seq_in_chunk))
    for grp in range(N_GROUPS):
        lanes = slice(grp * GROUP, (grp + 1) * GROUP)
        gg = g[:, lanes]
        gvn = gg * lax.rsqrt(jnp.mean(gg * gg, axis=-1, keepdims=True) + EPS) * gn_ref[:, lanes]
        if not prompt:
            gv_ref[:, lanes] = gvn
        wm = jnp.where(keep, ws_ref[grp], 0.0).astype(BF16)
        bcol = bs_ref[:, grp:grp + 1]
        for c in range(tm // CHUNK):
            rows = slice(c * CHUNK, (c + 1) * CHUNK)
            mixed = jnp.dot(wm, gvn[rows, :].astype(BF16), preferred_element_type=F32) + bcol
            s_ref[rows, lanes] = (u[rows, lanes] * mixed).astype(BF16)

    if prompt:
        for hd in range(N_HEADS):
            lanes = slice(hd * V_DIM, (hd + 1) * V_DIM)
            qt = q[:, lanes].T
            feat = lax.broadcasted_iota(jnp.int32, qt.shape, 0)
            qtm_ref[0, hd, 0] = jnp.where(feat < HEAD_DIM, qt, 0.0).astype(BF16)
            qtm_ref[0, hd, 1] = jnp.where(feat >= HEAD_DIM, qt, 0.0).astype(BF16)
            kb_ref[0, hd] = k[:, lanes].astype(BF16)
            vt_ref[0, hd, 0:V_DIM] = v[:, lanes].T.astype(BF16)
            ones_row = lax.broadcasted_iota(jnp.int32, (VT_ROWS - V_DIM, tm), 0) == 0
            vt_ref[0, hd, V_DIM:VT_ROWS] = ones_row.astype(F32).astype(BF16)
    else:
        q_ref[...] = q


def _ffn_in(x, nf, wg, wu, wd, nm, win, gn, ws, bs, *, batch, seq_in_chunk, prompt):
    n = x.shape[0]
    tm = TOKEN_BLOCK
    nb = n // tm
    per_batch = nb // batch
    tok = lambda w: pl.BlockSpec((tm, w), lambda i: (i, 0))
    in_specs = [
        tok(D_MODEL),
        _const_spec((1, D_MODEL)),
        *_gate_up_specs(), _const_spec(wd.shape),
        _const_spec((1, D_MODEL)),
        _const_spec(win.shape),
        _const_spec((1, GMLP_WIDTH)),
        _const_spec(ws.shape), _const_spec(bs.shape),
    ]
    out_specs = [tok(D_MODEL), tok(ATTN_WIDTH), tok(ATTN_WIDTH), tok(GMLP_WIDTH)]
    out_shape = [
        jax.ShapeDtypeStruct((n, D_MODEL), F32),
        jax.ShapeDtypeStruct((n, ATTN_WIDTH), F32),
        jax.ShapeDtypeStruct((n, ATTN_WIDTH), F32),
        jax.ShapeDtypeStruct((n, GMLP_WIDTH), BF16),
    ]
    if prompt:
        t = n // batch
        out_specs[1] = pl.BlockSpec((1, ATTN_WIDTH, tm), lambda i: (i // per_batch, 0, i % per_batch))
        out_shape[1] = jax.ShapeDtypeStruct((batch, ATTN_WIDTH, t), F32)
        out_specs[2] = pl.BlockSpec((tm * N_HEADS, V_DIM), lambda i: (i, 0))
        out_shape[2] = jax.ShapeDtypeStruct((n * N_HEADS, V_DIM), F32)
        out_specs += [
            pl.BlockSpec((1, N_HEADS, 2, V_DIM, tm), lambda i: (i // per_batch, 0, 0, 0, i % per_batch)),
            pl.BlockSpec((1, N_HEADS, tm, V_DIM), lambda i: (i // per_batch, 0, i % per_batch, 0)),
            pl.BlockSpec((1, N_HEADS, VT_ROWS, tm), lambda i: (i // per_batch, 0, 0, i % per_batch)),
        ]
        out_shape += [
            jax.ShapeDtypeStruct((batch, N_HEADS, 2, V_DIM, t), BF16),
            jax.ShapeDtypeStruct((batch, N_HEADS, t, V_DIM), BF16),
            jax.ShapeDtypeStruct((batch, N_HEADS, VT_ROWS, t), BF16),
        ]
    else:
        out_specs += [tok(ATTN_WIDTH), tok(GMLP_WIDTH)]
        out_shape += [
            jax.ShapeDtypeStruct((n, ATTN_WIDTH), F32),
            jax.ShapeDtypeStruct((n, GMLP_WIDTH), F32),
        ]
    return pl.pallas_call(
        functools.partial(_ffn_in_kernel, prompt=prompt, seq_in_chunk=seq_in_chunk),
        grid=(nb,),
        in_specs=in_specs,
        out_specs=out_specs,
        out_shape=out_shape,
        scratch_shapes=[pltpu.VMEM((tm, D_MODEL), BF16), pltpu.VMEM((tm, D_MODEL), F32)],
        compiler_params=pltpu.CompilerParams(dimension_semantics=("arbitrary",),
                                             vmem_limit_bytes=VMEM_LIMIT),
        name="ffn_in_prompt" if prompt else "ffn_in_sample",
    )(x, nf, wg, wu, wd, nm, win, gn, ws, bs)


def _out_ffn_kernel(x1_ref, o_ref, s_ref, wout_ref, nf_ref, wg_ref, wu_ref, wd_ref, nfin_ref,
                    y_ref, hn_ref, acc_ref, *, head_major):
    if head_major:
        o = jnp.concatenate([o_ref[0, hd] for hd in range(N_HEADS)], axis=1)
    else:
        o = o_ref[...]
    x2 = (x1_ref[...]
          + jnp.dot(o, wout_ref[0:ATTN_WIDTH, :], preferred_element_type=F32)
          + jnp.dot(s_ref[...], wout_ref[ATTN_WIDTH:D_MODEL, :], preferred_element_type=F32))
    x3 = _swiglu_half_step(x2, nf_ref, wg_ref, wu_ref, wd_ref, hn_ref, acc_ref)
    y_ref[...] = _rmsnorm(x3, nfin_ref[...])


def _out_ffn(x1, o, s, wout, nf, wg, wu, wd, nfin, *, batch, head_major):
    n = x1.shape[0]
    tm = TOKEN_BLOCK
    nb = n // tm
    per_batch = nb // batch
    tok = lambda w: pl.BlockSpec((tm, w), lambda i: (i, 0))
    if head_major:
        o_spec = pl.BlockSpec((1, N_HEADS, tm, V_DIM), lambda i: (i // per_batch, 0, i % per_batch, 0))
    else:
        o_spec = tok(ATTN_WIDTH)
    return pl.pallas_call(
        functools.partial(_out_ffn_kernel, head_major=head_major),
        grid=(nb,),
        in_specs=[
            tok(D_MODEL), o_spec, tok(GMLP_WIDTH),
            _const_spec(wout.shape),
            _const_spec((1, D_MODEL)),
            *_gate_up_specs(), _const_spec(wd.shape),
            _const_spec((1, D_MODEL)),
        ],
        out_specs=tok(D_MODEL),
        out_shape=jax.ShapeDtypeStruct((n, D_MODEL), F32),
        scratch_shapes=[pltpu.VMEM((tm, D_MODEL), BF16), pltpu.VMEM((tm, D_MODEL), F32)],
        compiler_params=pltpu.CompilerParams(dimension_semantics=("arbitrary",),
                                             vmem_limit_bytes=VMEM_LIMIT),
        name="out_ffn_prompt" if head_major else "out_ffn_sample",
    )(x1, o, s, wout, nf, wg, wu, wd, nfin)


def _attn_prompt_kernel(qi_ref, ki_ref, qtm_ref, kb_ref, vt_ref, e0_ref, e1_ref, lamv_ref, subln_ref,
                        o_ref, m_ref, acc_ref):
    step = pl.program_id(1)
    qi = qi_ref[step]
    ki = ki_ref[step]

    @pl.when(ki == 0)
    def _():
        m_ref[...] = jnp.full_like(m_ref, NEG)
        acc_ref[...] = jnp.zeros_like(acc_ref)


    def unit_logits(unit):
        hd, mp, q_lo, q_n, key_blocks = unit
        idx = 2 * hd + mp
        qm = qtm_ref[0, hd, mp, :, q_lo:q_lo + q_n]
        logits = []
        for k_lo, k_n, bias_ref in key_blocks:
            s = jnp.dot(kb_ref[0, hd, k_lo:k_lo + k_n, :], qm, preferred_element_type=F32)
            if bias_ref is not None:
                s = s + bias_ref[idx]
            logits.append(s)
        return logits

    def unit_update(unit, logits):
        hd, mp, q_lo, q_n, key_blocks = unit
        idx = 2 * hd + mp
        cols = slice(q_lo, q_lo + q_n)
        m_prev = m_ref[idx, :, cols]
        m_new = m_prev
        for s in logits:
            m_new = jnp.maximum(m_new, jnp.max(s, axis=0, keepdims=True))
        alpha = jnp.exp2(m_prev - m_new)
        pv = None
        for (k_lo, k_n, _), s in zip(key_blocks, logits):
            p = jnp.exp2(s - m_new)
            d = jnp.dot(vt_ref[0, hd, :, k_lo:k_lo + k_n], p.astype(BF16), preferred_element_type=F32)
            pv = d if pv is None else pv + d
        acc_ref[idx, :, cols] = alpha * acc_ref[idx, :, cols] + pv
        m_ref[idx, :, cols] = m_new

    def run(units):
        pending = None
        for unit in units:
            logits = unit_logits(unit)
            if pending is not None:
                unit_update(*pending)
            pending = (unit, logits)
        unit_update(*pending)

    maps = [(hd, mp) for hd in range(N_HEADS) for mp in range(2)]

    @pl.when(ki < qi - 1)
    def _():
        run([(hd, mp, 0, TQ, [(0, TK, None)]) for hd, mp in maps])

    @pl.when(ki == qi - 1)
    def _():
        run([u for hd, mp in maps for u in (
            (hd, mp, 0, SUB, [(0, SUB, None), (SUB, SUB, e1_ref)]),
            (hd, mp, SUB, SUB, [(0, TK, None)]))])

    @pl.when(ki == qi)
    def _():
        run([u for hd, mp in maps for u in (
            (hd, mp, 0, SUB, [(0, SUB, e0_ref)]),
            (hd, mp, SUB, SUB, [(0, SUB, e1_ref), (SUB, SUB, e0_ref)]))])
        lam = _lam(lamv_ref)
        for hd in range(N_HEADS):
            o1 = acc_ref[2 * hd, 0:V_DIM] * (1.0 / acc_ref[2 * hd, V_DIM:V_DIM + 1])
            o2 = acc_ref[2 * hd + 1, 0:V_DIM] * (1.0 / acc_ref[2 * hd + 1, V_DIM:V_DIM + 1])
            o = o1 - lam * o2
            inv = lax.rsqrt(jnp.mean(o * o, axis=0, keepdims=True) + EPS)
            on = o * inv * subln_ref[...] * (1.0 - LAM_INIT)
            o_ref[0, hd] = on.T.astype(BF16)


def _attn_prompt(qtm, kb, vt, e0, e1, lamv, subln_col):
    batch, _, _, _, t = qtm.shape
    nq = t // TQ
    pairs = [(q, k) for q in range(nq) for k in range(q + 1)]
    qi_arr = jnp.asarray([p[0] for p in pairs], jnp.int32)
    ki_arr = jnp.asarray([p[1] for p in pairs], jnp.int32)
    full = lambda shape: pl.BlockSpec(shape, lambda b, s, qi, ki: (0,) * len(shape),
                                      pipeline_mode=pl.Buffered(1))
    grid_spec = pltpu.PrefetchScalarGridSpec(
        num_scalar_prefetch=2,
        grid=(batch, len(pairs)),
        in_specs=[
            pl.BlockSpec((1, N_HEADS, 2, V_DIM, TQ), lambda b, s, qi, ki: (b, 0, 0, 0, qi[s])),
            pl.BlockSpec((1, N_HEADS, TK, V_DIM), lambda b, s, qi, ki: (b, 0, ki[s], 0)),
            pl.BlockSpec((1, N_HEADS, VT_ROWS, TK), lambda b, s, qi, ki: (b, 0, 0, ki[s])),
            full(e0.shape), full(e1.shape), full(lamv.shape), full(subln_col.shape),
        ],
        out_specs=pl.BlockSpec((1, N_HEADS, TQ, V_DIM), lambda b, s, qi, ki: (b, 0, qi[s], 0)),
        scratch_shapes=[
            pltpu.VMEM((N_MAPS, 1, TQ), F32),
            pltpu.VMEM((N_MAPS, VT_ROWS, TQ), F32),
        ],
    )
    return pl.pallas_call(
        _attn_prompt_kernel,
        grid_spec=grid_spec,
        out_shape=jax.ShapeDtypeStruct((batch, N_HEADS, t, V_DIM), BF16),
        compiler_params=pltpu.CompilerParams(dimension_semantics=("arbitrary", "arbitrary"),
                                             vmem_limit_bytes=VMEM_LIMIT),
        name="attn_prompt",
    )(qi_arr, ki_arr, qtm, kb, vt, e0, e1, lamv, subln_col)


def _attn_sample_kernel(pt_ref, q_ref, kn_ref, vn_ref, *rest):
    kp = rest[:PAGES_PER_STEP]
    vp = rest[PAGES_PER_STEP:2 * PAGES_PER_STEP]
    bl_ref, bn_ref, lamv_ref, subln_ref, o_ref, qbd_ref, m_ref, l_ref, acc_ref = rest[2 * PAGES_PER_STEP:]
    j = pl.program_id(1)
    last = pl.num_programs(1) - 1
    n_tok = q_ref.shape[1]
    head_rows = 2 * n_tok

    @pl.when(j == 0)
    def _():
        q = jnp.concatenate([q_ref[0]] * N_MAPS, axis=0)
        r = lax.broadcasted_iota(jnp.int32, q.shape, 0)
        c = lax.broadcasted_iota(jnp.int32, q.shape, 1)
        qbd_ref[...] = jnp.where((r // n_tok) == (c // HEAD_DIM), q, 0.0).astype(BF16)
        m_ref[...] = jnp.full_like(m_ref, NEG)
        l_ref[...] = jnp.zeros_like(l_ref)
        acc_ref[...] = jnp.zeros_like(acc_ref)

    def update(logit_blocks, values_of_head):
        s = jnp.concatenate(logit_blocks, axis=1) if len(logit_blocks) > 1 else logit_blocks[0]
        m_prev = m_ref[...]
        m_new = jnp.maximum(m_prev, jnp.max(s, axis=1, keepdims=True))
        alpha = jnp.exp2(m_prev - m_new)
        p = jnp.exp2(s - m_new)
        l_ref[...] = alpha * l_ref[...] + jnp.sum(p, axis=1, keepdims=True)
        pv = [jnp.dot(p[hd * head_rows:(hd + 1) * head_rows, :].astype(BF16), values_of_head(hd),
                      preferred_element_type=F32) for hd in range(N_HEADS)]
        acc_ref[...] = alpha * acc_ref[...] + jnp.concatenate(pv, axis=0)
        m_ref[...] = m_new

    qbd = qbd_ref[...]
    logit_blocks = [jnp.dot(qbd, kp[i][0].astype(BF16), preferred_element_type=F32)
                    for i in range(PAGES_PER_STEP)]
    logit_blocks[-1] = logit_blocks[-1] + jnp.where(j == last, bl_ref[...], 0.0)

    def cached_values(hd):
        rows = pl.ds(hd, CHUNK, stride=N_HEADS)
        return jnp.concatenate([vp[i][0, rows, :] for i in range(PAGES_PER_STEP)], axis=0).astype(BF16)

    update(logit_blocks, cached_values)

    @pl.when(j == last)
    def _():
        pad = jnp.zeros((CHUNK - n_tok, ATTN_WIDTH), F32)
        k_new = jnp.concatenate([kn_ref[0], pad], axis=0).astype(BF16)
        v_new = jnp.concatenate([vn_ref[0], pad], axis=0).astype(BF16)
        s_new = lax.dot_general(qbd, k_new, (((1,), (1,)), ((), ())), preferred_element_type=F32)
        update([s_new + bn_ref[...]], lambda hd: v_new[:, hd * V_DIM:(hd + 1) * V_DIM])

        lam = _lam(lamv_ref)
        o_all = acc_ref[...] * (1.0 / l_ref[...])
        for hd in range(N_HEADS):
            r1 = slice(hd * head_rows, hd * head_rows + n_tok)
            r2 = slice(hd * head_rows + n_tok, (hd + 1) * head_rows)
            o = o_all[r1] - lam * o_all[r2]
            o_ref[0, :, hd * V_DIM:(hd + 1) * V_DIM] = (
                _rmsnorm(o, subln_ref[...]) * (1.0 - LAM_INIT)).astype(BF16)


def _attn_sample(page_table, q, k_new, v_new, cache_k, cache_v, bl, bn, lamv, subln_row):
    n_seq, n_tok, _ = q.shape
    n_pages = page_table.shape[1]
    n_steps = n_pages // PAGES_PER_STEP
    n_rows = N_MAPS * n_tok
    tok_spec = pl.BlockSpec((1, n_tok, ATTN_WIDTH), lambda b, j, pt: (b, 0, 0))

    def page_spec(i):
        return pl.BlockSpec((1, ATTN_WIDTH, CHUNK),
                            lambda b, j, pt: (pt[b, j * PAGES_PER_STEP + i], 0, 0))

    full = lambda shape: pl.BlockSpec(shape, lambda b, j, pt: (0,) * len(shape))
    grid_spec = pltpu.PrefetchScalarGridSpec(
        num_scalar_prefetch=1,
        grid=(n_seq, n_steps),
        in_specs=([tok_spec, tok_spec, tok_spec]
                  + [page_spec(i) for i in range(PAGES_PER_STEP)]
                  + [page_spec(i) for i in range(PAGES_PER_STEP)]
                  + [full(bl.shape), full(bn.shape), full(lamv.shape), full(subln_row.shape)]),
        out_specs=tok_spec,
        scratch_shapes=[
            pltpu.VMEM((n_rows, ATTN_WIDTH), BF16),
            pltpu.VMEM((n_rows, 1), F32),
            pltpu.VMEM((n_rows, 1), F32),
            pltpu.VMEM((n_rows, V_DIM), F32),
        ],
    )
    return pl.pallas_call(
        _attn_sample_kernel,
        grid_spec=grid_spec,
        out_shape=jax.ShapeDtypeStruct((n_seq, n_tok, ATTN_WIDTH), BF16),
        compiler_params=pltpu.CompilerParams(dimension_semantics=("arbitrary", "arbitrary"),
                                             vmem_limit_bytes=VMEM_LIMIT),
        name="attn_sample",
    )(page_table, q, k_new, v_new, *([cache_k] * PAGES_PER_STEP), *([cache_v] * PAGES_PER_STEP),
      bl, bn, lamv, subln_row)


def _ffn_weights(w_gu, w_down):
    w_gu = w_gu.astype(BF16)
    return w_gu, w_gu, w_down.astype(BF16)


def kernel(x_prompt, x_sample, cache_k, cache_v, page_table, rel_table, norm_ffn1, w_ffn1_gu, w_ffn1_down, norm_mix, w_in, lambda_q1, lambda_k1, lambda_q2, lambda_k2, subln, gmlp_norm, w_spatial, b_spatial, w_out, norm_ffn2, w_ffn2_gu, w_ffn2_down, norm_final):
    depth = cache_k.shape[0]
    assert depth == 1
    batch, t_p, _ = x_prompt.shape
    n_seq, t_s, _ = x_sample.shape
    n_pool, page = cache_k.shape[1], cache_k.shape[2]
    assert page == CHUNK and CHUNK % t_s == 0 and (n_seq * t_s) % TOKEN_BLOCK == 0

    row = lambda a: a.reshape(1, -1)
    wg1, wu1, wd1 = _ffn_weights(w_ffn1_gu[0], w_ffn1_down[0])
    wg2, wu2, wd2 = _ffn_weights(w_ffn2_gu[0], w_ffn2_down[0])
    win = w_in[0].astype(BF16)
    wout = w_out[0].astype(BF16)
    lamv = jnp.stack([lambda_q1[0], lambda_k1[0], lambda_q2[0], lambda_k2[0]])
    reps = CHUNK // t_s
    ws_sample = jnp.tile(w_spatial[0][:, :t_s, :t_s], (1, reps, reps))
    bs_sample = jnp.tile(b_spatial[0][:, :t_s], (1, reps)).T
    ffn1 = (row(norm_ffn1[0]), wg1, wu1, wd1)
    mix = (row(norm_mix[0]), win, row(gmlp_norm[0]))

    e0, e1, bl, bn = _bias_tables(rel_table)

    x1p, kp, vp, sp, qtm, kb, vt = _ffn_in(
        x_prompt.reshape(batch * t_p, D_MODEL), *ffn1, *mix, w_spatial[0], b_spatial[0].T,
        batch=batch, seq_in_chunk=CHUNK, prompt=True)
    x1s, ks, vs, ss, qs, gvs = _ffn_in(
        x_sample.reshape(n_seq * t_s, D_MODEL), *ffn1, *mix, ws_sample, bs_sample,
        batch=1, seq_in_chunk=t_s, prompt=False)

    op = _attn_prompt(qtm, kb, vt, e0, e1, lamv, subln[0].reshape(V_DIM, 1))
    os_ = _attn_sample(
        page_table,
        qs.reshape(n_seq, t_s, ATTN_WIDTH), ks.reshape(n_seq, t_s, ATTN_WIDTH),
        vs.reshape(n_seq, t_s, ATTN_WIDTH),
        jnp.transpose(cache_k[0], (0, 2, 3, 1)).reshape(n_pool, ATTN_WIDTH, page),
        cache_v[0].reshape(n_pool, page * N_HEADS, V_DIM),
        bl, bn, lamv, row(subln[0]))

    ffn2 = (wout, row(norm_ffn2[0]), wg2, wu2, wd2, row(norm_final))
    yp = _out_ffn(x1p, op, sp, *ffn2, batch=batch, head_major=True)
    ys = _out_ffn(x1s, os_.reshape(n_seq * t_s, ATTN_WIDTH), ss, *ffn2, batch=1, head_major=False)

    return (
        yp.reshape(batch, t_p, D_MODEL),
        ys.reshape(n_seq, t_s, D_MODEL),
        jnp.transpose(kp.reshape(depth, batch, N_MAPS, HEAD_DIM, t_p), (0, 1, 4, 2, 3)),
        vp.reshape(depth, batch, t_p, N_HEADS, V_DIM),
        ks.reshape(depth, n_seq, t_s, N_MAPS, HEAD_DIM),
        vs.reshape(depth, n_seq, t_s, N_HEADS, V_DIM),
        gvs.reshape(depth, n_seq, t_s, GMLP_WIDTH),
    )
```

```python
import functools
import math

import jax
import jax.numpy as jnp
from jax import lax
from jax.experimental import pallas as pl
from jax.experimental.pallas import tpu as pltpu

F32 = jnp.float32
BF16 = jnp.bfloat16

D_MODEL = 1024
D_FF = 2816
FF_CHUNK = 256
N_FF_CHUNKS = D_FF // FF_CHUNK
ATTN_WIDTH = 512
HEAD_DIM = 64
N_HEADS = 4
N_MAPS = 2 * N_HEADS
V_DIM = 2 * HEAD_DIM
GMLP_WIDTH = 512
N_GROUPS = 4
GROUP = GMLP_WIDTH // N_GROUPS
CHUNK = 128
N_BUCKETS = 32
MAX_EXACT = N_BUCKETS // 2
MAX_DISTANCE = 128
EPS = 1e-6
NEG = -1e30
BUCKET_THRESHOLDS = tuple(
    math.ceil(MAX_EXACT * (MAX_DISTANCE / MAX_EXACT) ** (k / (N_BUCKETS - MAX_EXACT)))
    for k in range(1, N_BUCKETS - MAX_EXACT))
LAM_INIT = 0.8 - 0.6 * math.exp(-0.3 * 0)
LOG2E = math.log2(math.e)

TOKEN_BLOCK = 512
TQ = 512
TK = 512
SUB = 256
PIPE_DEPTH = 4
VT_ROWS = V_DIM + 16
PAGES_PER_STEP = 16
VMEM_LIMIT = 52 * 1024 * 1024


def _const_spec(shape):
    nd = len(shape)
    return pl.BlockSpec(shape, lambda *_: (0,) * nd, pipeline_mode=pl.Buffered(1))


def _gate_up_specs():
    half = lambda j: pl.BlockSpec((D_MODEL, D_FF), lambda *_: (0, j), pipeline_mode=pl.Buffered(1))
    return [half(0), half(1)]


def _rmsnorm(x, g):
    return x * lax.rsqrt(jnp.mean(x * x, axis=-1, keepdims=True) + EPS) * g


def _lam(lamv_ref):
    a = jnp.sum(lamv_ref[0:1, :] * lamv_ref[1:2, :], axis=1, keepdims=True)
    b = jnp.sum(lamv_ref[2:3, :] * lamv_ref[3:4, :], axis=1, keepdims=True)
    return jnp.exp(a) - jnp.exp(b) + LAM_INIT


def _bias_of_distance(d, tab_ref, m):
    n = jnp.maximum(d, 0)
    large = MAX_EXACT
    for thr in BUCKET_THRESHOLDS:
        large = large + (n >= thr).astype(jnp.int32)
    bucket = jnp.where(n < MAX_EXACT, n, large)
    val = jnp.zeros(d.shape, F32)
    for b in range(N_BUCKETS):
        val = jnp.where(bucket == b, tab_ref[b, m], val)
    val = (val - tab_ref[N_BUCKETS - 1, m]) * LOG2E
    return jnp.where(d >= 0, val, NEG)


def _bias_kernel(tab_ref, e0_ref, e1_ref, bl_ref, bn_ref):
    m = pl.program_id(0)
    key = lax.broadcasted_iota(jnp.int32, (SUB, SUB), 0)
    qry = lax.broadcasted_iota(jnp.int32, (SUB, SUB), 1)
    e0_ref[0] = _bias_of_distance(qry - key, tab_ref, m)
    e1_ref[0] = _bias_of_distance(SUB + qry - key, tab_ref, m)
    t = lax.broadcasted_iota(jnp.int32, (8, CHUNK), 0)
    c = lax.broadcasted_iota(jnp.int32, (8, CHUNK), 1)
    bl_ref[...] = _bias_of_distance(CHUNK + t - c, tab_ref, m)
    bn_ref[...] = _bias_of_distance(t - c, tab_ref, m)


def _bias_tables(rel_table):
    return pl.pallas_call(
        _bias_kernel,
        grid=(N_MAPS,),
        in_specs=[pl.BlockSpec(memory_space=pltpu.SMEM)],
        out_specs=[
            pl.BlockSpec((1, SUB, SUB), lambda m: (m, 0, 0)),
            pl.BlockSpec((1, SUB, SUB), lambda m: (m, 0, 0)),
            pl.BlockSpec((8, CHUNK), lambda m: (m, 0)),
            pl.BlockSpec((8, CHUNK), lambda m: (m, 0)),
        ],
        out_shape=[
            jax.ShapeDtypeStruct((N_MAPS, SUB, SUB), F32),
            jax.ShapeDtypeStruct((N_MAPS, SUB, SUB), F32),
            jax.ShapeDtypeStruct((N_MAPS * 8, CHUNK), F32),
            jax.ShapeDtypeStruct((N_MAPS * 8, CHUNK), F32),
        ],
        name="bias_tables",
    )(rel_table)


def _swiglu_half_step(x, norm_ref, wg_ref, wu_ref, wd_ref, hn_ref, acc_ref):
    hn_ref[...] = _rmsnorm(x, norm_ref[...]).astype(BF16)

    def gate_up(c):
        hn = hn_ref[...]
        cols = slice(c * FF_CHUNK, (c + 1) * FF_CHUNK)
        return (jnp.dot(hn, wg_ref[:, cols], preferred_element_type=F32),
                jnp.dot(hn, wu_ref[:, cols], preferred_element_type=F32))

    def down(c, gate, up):
        act = (jax.nn.silu(gate) * up).astype(BF16)
        d = jnp.dot(act, wd_ref[c * FF_CHUNK:(c + 1) * FF_CHUNK, :], preferred_element_type=F32)
        if c == 0:
            acc_ref[...] = d
        else:
            acc_ref[...] += d

    pending = gate_up(0)
    for c in range(N_FF_CHUNKS):
        nxt = gate_up(c + 1) if c + 1 < N_FF_CHUNKS else None
        down(c, *pending)
        pending = nxt
    return x + 0.5 * acc_ref[...]


def _ffn_in_kernel(x_ref, nf_ref, wg_ref, wu_ref, wd_ref, nm_ref, win_ref, gn_ref, ws_ref, bs_ref,
                   *rest, prompt, seq_in_chunk):
    if prompt:
        x1_ref, k_ref, v_ref, s_ref, qtm_ref, kb_ref, vt_ref, hn_ref, acc_ref = rest
    else:
        x1_ref, k_ref, v_ref, s_ref, q_ref, gv_ref, hn_ref, acc_ref = rest
    tm = x_ref.shape[0]

    x1 = _swiglu_half_step(x_ref[...], nf_ref, wg_ref, wu_ref, wd_ref, hn_ref, acc_ref)
    x1_ref[...] = x1
    h = _rmsnorm(x1, nm_ref[...]).astype(BF16)

    def proj(i):
        return jnp.dot(h, win_ref[:, i * ATTN_WIDTH:(i + 1) * ATTN_WIDTH], preferred_element_type=F32)

    q = proj(0) * (HEAD_DIM ** -0.5 * LOG2E)
    k = proj(1)
    v = proj(2)
    u = jax.nn.gelu(proj(3))
    g = jax.nn.gelu(proj(4))
    k_ref[...] = k.T[None] if prompt else k
    if prompt:
        for hd in range(N_HEADS):
            v_ref[pl.ds(hd, tm, stride=N_HEADS), :] = v[:, hd * V_DIM:(hd + 1) * V_DIM]
    else:
        v_ref[...] = v

    row = lax.broadcasted_iota(jnp.int32, (CHUNK, CHUNK), 0)
    col = lax.broadcasted_iota(jnp.int32, (CHUNK, CHUNK), 1)
    keep = col <= row
    if seq_in_chunk < CHUNK:
        keep = keep & ((row // seq_in_chunk) == (col // seq_in_chunk))
    for grp in range(N_GROUPS):
        lanes = slice(grp * GROUP, (grp + 1) * GROUP)
        gg = g[:, lanes]
        gvn = gg * lax.rsqrt(jnp.mean(gg * gg, axis=-1, keepdims=True) + EPS) * gn_ref[:, lanes]
        if not prompt:
            gv_ref[:, lanes] = gvn
        wm = jnp.where(keep, ws_ref[grp], 0.0).astype(BF16)
        bcol = bs_ref[:, grp:grp + 1]
        for c in range(tm // CHUNK):
            rows = slice(c * CHUNK, (c + 1) * CHUNK)
            mixed = jnp.dot(wm, gvn[rows, :].astype(BF16), preferred_element_type=F32) + bcol
            s_ref[rows, lanes] = (u[rows, lanes] * mixed).astype(BF16)

    if prompt:
        for hd in range(N_HEADS):
            lanes = slice(hd * V_DIM, (hd + 1) * V_DIM)
            qt = q[:, lanes].T
            feat = lax.broadcasted_iota(jnp.int32, qt.shape, 0)
            qtm_ref[0, hd, 0] = jnp.where(feat < HEAD_DIM, qt, 0.0).astype(BF16)
            qtm_ref[0, hd, 1] = jnp.where(feat >= HEAD_DIM, qt, 0.0).astype(BF16)
            kb_ref[0, hd] = k[:, lanes].astype(BF16)
            vt_ref[0, hd, 0:V_DIM] = v[:, lanes].T.astype(BF16)
            ones_row = lax.broadcasted_iota(jnp.int32, (VT_ROWS - V_DIM, tm), 0) == 0
            vt_ref[0, hd, V_DIM:VT_ROWS] = ones_row.astype(F32).astype(BF16)
    else:
        q_ref[...] = q


def _ffn_in(x, nf, wg, wu, wd, nm, win, gn, ws, bs, *, batch, seq_in_chunk, prompt):
    n = x.shape[0]
    tm = min(TOKEN_BLOCK, n)
    nb = n // tm
    per_batch = nb // batch
    tok = lambda w: pl.BlockSpec((tm, w), lambda i: (i, 0))
    in_specs = [
        tok(D_MODEL),
        _const_spec((1, D_MODEL)),
        *_gate_up_specs(), _const_spec(wd.shape),
        _const_spec((1, D_MODEL)),
        _const_spec(win.shape),
        _const_spec((1, GMLP_WIDTH)),
        _const_spec(ws.shape), _const_spec(bs.shape),
    ]
    out_specs = [tok(D_MODEL), tok(ATTN_WIDTH), tok(ATTN_WIDTH), tok(GMLP_WIDTH)]
    out_shape = [
        jax.ShapeDtypeStruct((n, D_MODEL), F32),
        jax.ShapeDtypeStruct((n, ATTN_WIDTH), F32),
        jax.ShapeDtypeStruct((n, ATTN_WIDTH), F32),
        jax.ShapeDtypeStruct((n, GMLP_WIDTH), BF16),
    ]
    if prompt:
        t = n // batch
        out_specs[1] = pl.BlockSpec((1, ATTN_WIDTH, tm), lambda i: (i // per_batch, 0, i % per_batch))
        out_shape[1] = jax.ShapeDtypeStruct((batch, ATTN_WIDTH, t), F32)
        out_specs[2] = pl.BlockSpec((tm * N_HEADS, V_DIM), lambda i: (i, 0))
        out_shape[2] = jax.ShapeDtypeStruct((n * N_HEADS, V_DIM), F32)
        out_specs += [
            pl.BlockSpec((1, N_HEADS, 2, V_DIM, tm), lambda i: (i // per_batch, 0, 0, 0, i % per_batch)),
            pl.BlockSpec((1, N_HEADS, tm, V_DIM), lambda i: (i // per_batch, 0, i % per_batch, 0)),
            pl.BlockSpec((1, N_HEADS, VT_ROWS, tm), lambda i: (i // per_batch, 0, 0, i % per_batch)),
        ]
        out_shape += [
            jax.ShapeDtypeStruct((batch, N_HEADS, 2, V_DIM, t), BF16),
            jax.ShapeDtypeStruct((batch, N_HEADS, t, V_DIM), BF16),
            jax.ShapeDtypeStruct((batch, N_HEADS, VT_ROWS, t), BF16),
        ]
    else:
        out_specs += [tok(ATTN_WIDTH), tok(GMLP_WIDTH)]
        out_shape += [
            jax.ShapeDtypeStruct((n, ATTN_WIDTH), F32),
            jax.ShapeDtypeStruct((n, GMLP_WIDTH), F32),
        ]
    return pl.pallas_call(
        functools.partial(_ffn_in_kernel, prompt=prompt, seq_in_chunk=seq_in_chunk),
        grid=(nb,),
        in_specs=in_specs,
        out_specs=out_specs,
        out_shape=out_shape,
        scratch_shapes=[pltpu.VMEM((tm, D_MODEL), BF16), pltpu.VMEM((tm, D_MODEL), F32)],
        compiler_params=pltpu.CompilerParams(dimension_semantics=("arbitrary",),
                                             vmem_limit_bytes=VMEM_LIMIT),
        name="ffn_in_prompt" if prompt else "ffn_in_sample",
    )(x, nf, wg, wu, wd, nm, win, gn, ws, bs)


def _out_ffn_kernel(x1_ref, o_ref, s_ref, wout_ref, nf_ref, wg_ref, wu_ref, wd_ref, nfin_ref,
                    y_ref, hn_ref, acc_ref, *, head_major):
    if head_major:
        o = jnp.concatenate([o_ref[0, hd] for hd in range(N_HEADS)], axis=1)
    else:
        o = o_ref[...]
    x2 = (x1_ref[...]
          + jnp.dot(o, wout_ref[0:ATTN_WIDTH, :], preferred_element_type=F32)
          + jnp.dot(s_ref[...], wout_ref[ATTN_WIDTH:D_MODEL, :], preferred_element_type=F32))
    x3 = _swiglu_half_step(x2, nf_ref, wg_ref, wu_ref, wd_ref, hn_ref, acc_ref)
    y_ref[...] = _rmsnorm(x3, nfin_ref[...])


def _out_ffn(x1, o, s, wout, nf, wg, wu, wd, nfin, *, batch, head_major):
    n = x1.shape[0]
    tm = min(TOKEN_BLOCK, n)
    nb = n // tm
    per_batch = nb // batch
    tok = lambda w: pl.BlockSpec((tm, w), lambda i: (i, 0))
    if head_major:
        o_spec = pl.BlockSpec((1, N_HEADS, tm, V_DIM), lambda i: (i // per_batch, 0, i % per_batch, 0))
    else:
        o_spec = tok(ATTN_WIDTH)
    return pl.pallas_call(
        functools.partial(_out_ffn_kernel, head_major=head_major),
        grid=(nb,),
        in_specs=[
            tok(D_MODEL), o_spec, tok(GMLP_WIDTH),
            _const_spec(wout.shape),
            _const_spec((1, D_MODEL)),
            *_gate_up_specs(), _const_spec(wd.shape),
            _const_spec((1, D_MODEL)),
        ],
        out_specs=tok(D_MODEL),
        out_shape=jax.ShapeDtypeStruct((n, D_MODEL), F32),
        scratch_shapes=[pltpu.VMEM((tm, D_MODEL), BF16), pltpu.VMEM((tm, D_MODEL), F32)],
        compiler_params=pltpu.CompilerParams(dimension_semantics=("arbitrary",),
                                             vmem_limit_bytes=VMEM_LIMIT),
        name="out_ffn_prompt" if head_major else "out_ffn_sample",
    )(x1, o, s, wout, nf, wg, wu, wd, nfin)


def _attn_prompt_kernel(qi_ref, ki_ref, qtm_ref, kb_ref, vt_ref, e0_ref, e1_ref, lamv_ref, subln_ref,
                        o_ref, m_ref, acc_ref):
    step = pl.program_id(1)
    qi = qi_ref[step]
    ki = ki_ref[step]

    @pl.when(ki == 0)
    def _():
        m_ref[...] = jnp.full_like(m_ref, NEG)
        acc_ref[...] = jnp.zeros_like(acc_ref)


    def unit_logits(unit):
        hd, mp, q_lo, q_n, key_blocks = unit
        idx = 2 * hd + mp
        qm = qtm_ref[0, hd, mp, :, q_lo:q_lo + q_n]
        logits = []
        for k_lo, k_n, bias_ref in key_blocks:
            s = jnp.dot(kb_ref[0, hd, k_lo:k_lo + k_n, :], qm, preferred_element_type=F32)
            if bias_ref is not None:
                s = s + bias_ref[idx]
            logits.append(s)
        return logits

    def unit_update(unit, logits):
        hd, mp, q_lo, q_n, key_blocks = unit
        idx = 2 * hd + mp
        cols = slice(q_lo, q_lo + q_n)
        m_prev = m_ref[idx, :, cols]
        m_new = m_prev
        for s in logits:
            m_new = jnp.maximum(m_new, jnp.max(s, axis=0, keepdims=True))
        alpha = jnp.exp2(m_prev - m_new)
        pv = None
        for (k_lo, k_n, _), s in zip(key_blocks, logits):
            p = jnp.exp2(s - m_new)
            d = jnp.dot(vt_ref[0, hd, :, k_lo:k_lo + k_n], p.astype(BF16), preferred_element_type=F32)
            pv = d if pv is None else pv + d
        acc_ref[idx, :, cols] = alpha * acc_ref[idx, :, cols] + pv
        m_ref[idx, :, cols] = m_new

    def run(units):
        pending = []
        for unit in units:
            pending.append((unit, unit_logits(unit)))
            if len(pending) > PIPE_DEPTH:
                unit_update(*pending.pop(0))
        for item in pending:
            unit_update(*item)

    maps = [(hd, mp) for hd in range(N_HEADS) for mp in range(2)]

    @pl.when(ki < qi - 1)
    def _():
        run([(hd, mp, q_lo, SUB, [(0, TK, None)]) for hd, mp in maps for q_lo in range(0, TQ, SUB)])

    @pl.when(ki == qi - 1)
    def _():
        run([u for hd, mp in maps for u in (
            (hd, mp, 0, SUB, [(0, SUB, None), (SUB, SUB, e1_ref)]),
            (hd, mp, SUB, SUB, [(0, TK, None)]))])

    @pl.when(ki == qi)
    def _():
        run([u for hd, mp in maps for u in (
            (hd, mp, 0, SUB, [(0, SUB, e0_ref)]),
            (hd, mp, SUB, SUB, [(0, SUB, e1_ref), (SUB, SUB, e0_ref)]))])
        lam = _lam(lamv_ref)
        for hd in range(N_HEADS):
            o1 = acc_ref[2 * hd, 0:V_DIM] * (1.0 / acc_ref[2 * hd, V_DIM:V_DIM + 1])
            o2 = acc_ref[2 * hd + 1, 0:V_DIM] * (1.0 / acc_ref[2 * hd + 1, V_DIM:V_DIM + 1])
            o = o1 - lam * o2
            inv = lax.rsqrt(jnp.mean(o * o, axis=0, keepdims=True) + EPS)
            on = o * inv * subln_ref[...] * (1.0 - LAM_INIT)
            o_ref[0, hd] = on.T.astype(BF16)


def _attn_prompt(qtm, kb, vt, e0, e1, lamv, subln_col):
    batch, _, _, _, t = qtm.shape
    nq = t // TQ
    pairs = [(q, k) for q in range(nq) for k in range(q + 1)]
    qi_arr = jnp.asarray([p[0] for p in pairs], jnp.int32)
    ki_arr = jnp.asarray([p[1] for p in pairs], jnp.int32)
    full = lambda shape: pl.BlockSpec(shape, lambda b, s, qi, ki: (0,) * len(shape),
                                      pipeline_mode=pl.Buffered(1))
    grid_spec = pltpu.PrefetchScalarGridSpec(
        num_scalar_prefetch=2,
        grid=(batch, len(pairs)),
        in_specs=[
            pl.BlockSpec((1, N_HEADS, 2, V_DIM, TQ), lambda b, s, qi, ki: (b, 0, 0, 0, qi[s])),
            pl.BlockSpec((1, N_HEADS, TK, V_DIM), lambda b, s, qi, ki: (b, 0, ki[s], 0)),
            pl.BlockSpec((1, N_HEADS, VT_ROWS, TK), lambda b, s, qi, ki: (b, 0, 0, ki[s])),
            full(e0.shape), full(e1.shape), full(lamv.shape), full(subln_col.shape),
        ],
        out_specs=pl.BlockSpec((1, N_HEADS, TQ, V_DIM), lambda b, s, qi, ki: (b, 0, qi[s], 0)),
        scratch_shapes=[
            pltpu.VMEM((N_MAPS, 1, TQ), F32),
            pltpu.VMEM((N_MAPS, VT_ROWS, TQ), F32),
        ],
    )
    return pl.pallas_call(
        _attn_prompt_kernel,
        grid_spec=grid_spec,
        out_shape=jax.ShapeDtypeStruct((batch, N_HEADS, t, V_DIM), BF16),
        compiler_params=pltpu.CompilerParams(dimension_semantics=("arbitrary", "arbitrary"),
                                             vmem_limit_bytes=VMEM_LIMIT),
        name="attn_prompt",
    )(qi_arr, ki_arr, qtm, kb, vt, e0, e1, lamv, subln_col)


def _attn_sample_kernel(pt_ref, q_ref, kn_ref, vn_ref, *rest):
    kp = rest[:PAGES_PER_STEP]
    vp = rest[PAGES_PER_STEP:2 * PAGES_PER_STEP]
    bl_ref, bn_ref, lamv_ref, subln_ref, o_ref, qbd_ref, m_ref, l_ref, acc_ref = rest[2 * PAGES_PER_STEP:]
    j = pl.program_id(1)
    last = pl.num_programs(1) - 1
    n_tok = q_ref.shape[1]
    head_rows = 2 * n_tok

    @pl.when(j == 0)
    def _():
        q = jnp.concatenate([q_ref[0]] * N_MAPS, axis=0)
        r = lax.broadcasted_iota(jnp.int32, q.shape, 0)
        c = lax.broadcasted_iota(jnp.int32, q.shape, 1)
        qbd_ref[...] = jnp.where((r // n_tok) == (c // HEAD_DIM), q, 0.0).astype(BF16)
        m_ref[...] = jnp.full_like(m_ref, NEG)
        l_ref[...] = jnp.zeros_like(l_ref)
        acc_ref[...] = jnp.zeros_like(acc_ref)

    def update(logit_blocks, values_of_head):
        s = jnp.concatenate(logit_blocks, axis=1) if len(logit_blocks) > 1 else logit_blocks[0]
        m_prev = m_ref[...]
        m_new = jnp.maximum(m_prev, jnp.max(s, axis=1, keepdims=True))
        alpha = jnp.exp2(m_prev - m_new)
        p = jnp.exp2(s - m_new)
        l_ref[...] = alpha * l_ref[...] + jnp.sum(p, axis=1, keepdims=True)
        pv = [jnp.dot(p[hd * head_rows:(hd + 1) * head_rows, :].astype(BF16), values_of_head(hd),
                      preferred_element_type=F32) for hd in range(N_HEADS)]
        acc_ref[...] = alpha * acc_ref[...] + jnp.concatenate(pv, axis=0)
        m_ref[...] = m_new

    qbd = qbd_ref[...]
    logit_blocks = [jnp.dot(qbd, kp[i][0].astype(BF16), preferred_element_type=F32)
                    for i in range(PAGES_PER_STEP)]
    logit_blocks[-1] = logit_blocks[-1] + jnp.where(j == last, bl_ref[...], 0.0)

    def cached_values(hd):
        rows = pl.ds(hd, CHUNK, stride=N_HEADS)
        return jnp.concatenate([vp[i][0, rows, :] for i in range(PAGES_PER_STEP)], axis=0).astype(BF16)

    update(logit_blocks, cached_values)

    @pl.when(j == last)
    def _():
        pad = jnp.zeros((CHUNK - n_tok, ATTN_WIDTH), F32)
        k_new = jnp.concatenate([kn_ref[0], pad], axis=0).astype(BF16)
        v_new = jnp.concatenate([vn_ref[0], pad], axis=0).astype(BF16)
        s_new = lax.dot_general(qbd, k_new, (((1,), (1,)), ((), ())), preferred_element_type=F32)
        update([s_new + bn_ref[...]], lambda hd: v_new[:, hd * V_DIM:(hd + 1) * V_DIM])

        lam = _lam(lamv_ref)
        o_all = acc_ref[...] * (1.0 / l_ref[...])
        for hd in range(N_HEADS):
            r1 = slice(hd * head_rows, hd * head_rows + n_tok)
            r2 = slice(hd * head_rows + n_tok, (hd + 1) * head_rows)
            o = o_all[r1] - lam * o_all[r2]
            o_ref[0, :, hd * V_DIM:(hd + 1) * V_DIM] = (
                _rmsnorm(o, subln_ref[...]) * (1.0 - LAM_INIT)).astype(BF16)


def _attn_sample(page_table, q, k_new, v_new, cache_k, cache_v, bl, bn, lamv, subln_row):
    n_seq, n_tok, _ = q.shape
    n_pages = page_table.shape[1]
    n_steps = n_pages // PAGES_PER_STEP
    n_rows = N_MAPS * n_tok
    tok_spec = pl.BlockSpec((1, n_tok, ATTN_WIDTH), lambda b, j, pt: (b, 0, 0))

    def page_spec(i):
        return pl.BlockSpec((1, ATTN_WIDTH, CHUNK),
                            lambda b, j, pt: (pt[b, j * PAGES_PER_STEP + i], 0, 0))

    full = lambda shape: pl.BlockSpec(shape, lambda b, j, pt: (0,) * len(shape))
    grid_spec = pltpu.PrefetchScalarGridSpec(
        num_scalar_prefetch=1,
        grid=(n_seq, n_steps),
        in_specs=([tok_spec, tok_spec, tok_spec]
                  + [page_spec(i) for i in range(PAGES_PER_STEP)]
                  + [page_spec(i) for i in range(PAGES_PER_STEP)]
                  + [full(bl.shape), full(bn.shape), full(lamv.shape), full(subln_row.shape)]),
        out_specs=tok_spec,
        scratch_shapes=[
            pltpu.VMEM((n_rows, ATTN_WIDTH), BF16),
            pltpu.VMEM((n_rows, 1), F32),
            pltpu.VMEM((n_rows, 1), F32),
            pltpu.VMEM((n_rows, V_DIM), F32),
        ],
    )
    return pl.pallas_call(
        _attn_sample_kernel,
        grid_spec=grid_spec,
        out_shape=jax.ShapeDtypeStruct((n_seq, n_tok, ATTN_WIDTH), BF16),
        compiler_params=pltpu.CompilerParams(dimension_semantics=("arbitrary", "arbitrary"),
                                             vmem_limit_bytes=VMEM_LIMIT),
        name="attn_sample",
    )(page_table, q, k_new, v_new, *([cache_k] * PAGES_PER_STEP), *([cache_v] * PAGES_PER_STEP),
      bl, bn, lamv, subln_row)


def _ffn_weights(w_gu, w_down):
    w_gu = w_gu.astype(BF16)
    return w_gu, w_gu, w_down.astype(BF16)


def kernel(x_prompt, x_sample, cache_k, cache_v, page_table, rel_table, norm_ffn1, w_ffn1_gu, w_ffn1_down, norm_mix, w_in, lambda_q1, lambda_k1, lambda_q2, lambda_k2, subln, gmlp_norm, w_spatial, b_spatial, w_out, norm_ffn2, w_ffn2_gu, w_ffn2_down, norm_final):
    depth = cache_k.shape[0]
    assert depth == 1
    batch, t_p, _ = x_prompt.shape
    n_seq, t_s, _ = x_sample.shape
    n_pool, page = cache_k.shape[1], cache_k.shape[2]
    assert page == CHUNK and CHUNK % t_s == 0 and (n_seq * t_s) % CHUNK == 0

    row = lambda a: a.reshape(1, -1)
    wg1, wu1, wd1 = _ffn_weights(w_ffn1_gu[0], w_ffn1_down[0])
    wg2, wu2, wd2 = _ffn_weights(w_ffn2_gu[0], w_ffn2_down[0])
    win = w_in[0].astype(BF16)
    wout = w_out[0].astype(BF16)
    lamv = jnp.stack([lambda_q1[0], lambda_k1[0], lambda_q2[0], lambda_k2[0]])
    reps = CHUNK // t_s
    ws_sample = jnp.tile(w_spatial[0][:, :t_s, :t_s], (1, reps, reps))
    bs_sample = jnp.tile(b_spatial[0][:, :t_s], (1, reps)).T
    ffn1 = (row(norm_ffn1[0]), wg1, wu1, wd1)
    mix = (row(norm_mix[0]), win, row(gmlp_norm[0]))

    e0, e1, bl, bn = _bias_tables(rel_table)

    x1p, kp, vp, sp, qtm, kb, vt = _ffn_in(
        x_prompt.reshape(batch * t_p, D_MODEL), *ffn1, *mix, w_spatial[0], b_spatial[0].T,
        batch=batch, seq_in_chunk=CHUNK, prompt=True)
    x1s, ks, vs, ss, qs, gvs = _ffn_in(
        x_sample.reshape(n_seq * t_s, D_MODEL), *ffn1, *mix, ws_sample, bs_sample,
        batch=1, seq_in_chunk=t_s, prompt=False)

    op = _attn_prompt(qtm, kb, vt, e0, e1, lamv, subln[0].reshape(V_DIM, 1))
    os_ = _attn_sample(
        page_table,
        qs.reshape(n_seq, t_s, ATTN_WIDTH), ks.reshape(n_seq, t_s, ATTN_WIDTH),
        vs.reshape(n_seq, t_s, ATTN_WIDTH),
        jnp.transpose(cache_k[0], (0, 2, 3, 1)).reshape(n_pool, ATTN_WIDTH, page),
        cache_v[0].reshape(n_pool, page * N_HEADS, V_DIM),
        bl, bn, lamv, row(subln[0]))

    ffn2 = (wout, row(norm_ffn2[0]), wg2, wu2, wd2, row(norm_final))
    yp = _out_ffn(x1p, op, sp, *ffn2, batch=batch, head_major=True)
    ys = _out_ffn(x1s, os_.reshape(n_seq * t_s, ATTN_WIDTH), ss, *ffn2, batch=1, head_major=False)

    return (
        yp.reshape(batch, t_p, D_MODEL),
        ys.reshape(n_seq, t_s, D_MODEL),
        jnp.transpose(kp.reshape(depth, batch, N_MAPS, HEAD_DIM, t_p), (0, 1, 4, 2, 3)),
        vp.reshape(depth, batch, t_p, N_HEADS, V_DIM),
        ks.reshape(depth, n_seq, t_s, N_MAPS, HEAD_DIM),
        vs.reshape(depth, n_seq, t_s, N_HEADS, V_DIM),
        gvs.reshape(depth, n_seq, t_s, GMLP_WIDTH),
    )
```

```python
import functools
import math

import jax
import jax.numpy as jnp
from jax import lax
from jax.experimental import pallas as pl
from jax.experimental.pallas import tpu as pltpu

F32 = jnp.float32
BF16 = jnp.bfloat16

D_MODEL = 1024
D_FF = 2816
FF_CHUNK = 256
N_FF_CHUNKS = D_FF // FF_CHUNK
ATTN_WIDTH = 512
HEAD_DIM = 64
N_HEADS = 4
N_MAPS = 2 * N_HEADS
V_DIM = 2 * HEAD_DIM
GMLP_WIDTH = 512
N_GROUPS = 4
GROUP = GMLP_WIDTH // N_GROUPS
CHUNK = 128
N_BUCKETS = 32
MAX_EXACT = N_BUCKETS // 2
MAX_DISTANCE = 128
EPS = 1e-6
NEG = -1e30
BUCKET_THRESHOLDS = tuple(
    math.ceil(MAX_EXACT * (MAX_DISTANCE / MAX_EXACT) ** (k / (N_BUCKETS - MAX_EXACT)))
    for k in range(1, N_BUCKETS - MAX_EXACT))
LAM_INIT = 0.8 - 0.6 * math.exp(-0.3 * 0)
LOG2E = math.log2(math.e)

TOKEN_BLOCK = 512
TQ = 512
TK = 512
SUB = 256
PIPE_DEPTH = 6
VT_ROWS = V_DIM + 16
PAGES_PER_STEP = 8
VMEM_LIMIT = 52 * 1024 * 1024


def _const_spec(shape):
    nd = len(shape)
    return pl.BlockSpec(shape, lambda *_: (0,) * nd, pipeline_mode=pl.Buffered(1))


def _gate_up_specs():
    half = lambda j: pl.BlockSpec((D_MODEL, D_FF), lambda *_: (0, j), pipeline_mode=pl.Buffered(1))
    return [half(0), half(1)]


def _rmsnorm(x, g):
    return x * lax.rsqrt(jnp.mean(x * x, axis=-1, keepdims=True) + EPS) * g


def _lam(small_ref):
    a = jnp.sum(small_ref[1:2, :] * small_ref[2:3, :], axis=1, keepdims=True)
    b = jnp.sum(small_ref[3:4, :] * small_ref[4:5, :], axis=1, keepdims=True)
    return jnp.exp(a) - jnp.exp(b) + LAM_INIT


def _bias_of_distance(d, tab_ref, m):
    n = jnp.maximum(d, 0)
    large = MAX_EXACT
    for thr in BUCKET_THRESHOLDS:
        large = large + (n >= thr).astype(jnp.int32)
    bucket = jnp.where(n < MAX_EXACT, n, large)
    val = jnp.zeros(d.shape, F32)
    for b in range(N_BUCKETS):
        val = jnp.where(bucket == b, tab_ref[b, m], val)
    val = (val - tab_ref[N_BUCKETS - 1, m]) * LOG2E
    return jnp.where(d >= 0, val, NEG)


def _bias_kernel(tab_ref, e_ref, b_ref, *, n_tok):
    m = pl.program_id(0)
    key = lax.broadcasted_iota(jnp.int32, (SUB, SUB), 0)
    qry = lax.broadcasted_iota(jnp.int32, (SUB, SUB), 1)
    e_ref[0, 0] = _bias_of_distance(qry - key, tab_ref, m)
    e_ref[1, 0] = _bias_of_distance(SUB + qry - key, tab_ref, m)
    t = lax.broadcasted_iota(jnp.int32, (n_tok, CHUNK), 0)
    c = lax.broadcasted_iota(jnp.int32, (n_tok, CHUNK), 1)
    b_ref[0] = _bias_of_distance(CHUNK + t - c, tab_ref, m)
    b_ref[1] = _bias_of_distance(t - c, tab_ref, m)


def _bias_tables(rel_table, n_tok):
    return pl.pallas_call(
        functools.partial(_bias_kernel, n_tok=n_tok),
        grid=(N_MAPS,),
        in_specs=[pl.BlockSpec(memory_space=pltpu.SMEM)],
        out_specs=[
            pl.BlockSpec((2, 1, SUB, SUB), lambda m: (0, m, 0, 0)),
            pl.BlockSpec((2, n_tok, CHUNK), lambda m: (0, m, 0)),
        ],
        out_shape=[
            jax.ShapeDtypeStruct((2, N_MAPS, SUB, SUB), F32),
            jax.ShapeDtypeStruct((2, N_MAPS * n_tok, CHUNK), F32),
        ],
        name="bias_tables",
    )(rel_table)


def _swiglu_half_step(x, norm_ref, wg_ref, wu_ref, wd_ref, hn_ref, acc_ref):
    hn_ref[...] = _rmsnorm(x, norm_ref[...]).astype(BF16)

    def gate_up(c):
        hn = hn_ref[...]
        cols = slice(c * FF_CHUNK, (c + 1) * FF_CHUNK)
        return (jnp.dot(hn, wg_ref[:, cols], preferred_element_type=F32),
                jnp.dot(hn, wu_ref[:, cols], preferred_element_type=F32))

    def down(c, gate, up):
        act = (jax.nn.silu(gate) * up).astype(BF16)
        d = jnp.dot(act, wd_ref[c * FF_CHUNK:(c + 1) * FF_CHUNK, :], preferred_element_type=F32)
        if c == 0:
            acc_ref[...] = d
        else:
            acc_ref[...] += d

    pending = gate_up(0)
    for c in range(N_FF_CHUNKS):
        nxt = gate_up(c + 1) if c + 1 < N_FF_CHUNKS else None
        down(c, *pending)
        pending = nxt
    return x + 0.5 * acc_ref[...]


def _ffn_in_kernel(x_ref, nf_ref, wg_ref, wu_ref, wd_ref, nm_ref, win_ref, gn_ref, ws_ref, bs_ref,
                   *rest, prompt, seq_in_chunk):
    if prompt:
        x1_ref, k_ref, v_ref, s_ref, qtm_ref, kb_ref, vt_ref, hn_ref, acc_ref = rest
    else:
        x1_ref, k_ref, v_ref, s_ref, q_ref, gv_ref, hn_ref, acc_ref = rest
    tm = x_ref.shape[0]

    x1 = _swiglu_half_step(x_ref[...], nf_ref, wg_ref, wu_ref, wd_ref, hn_ref, acc_ref)
    x1_ref[...] = x1
    h = _rmsnorm(x1, nm_ref[...]).astype(BF16)

    def proj(i):
        return jnp.dot(h, win_ref[:, i * ATTN_WIDTH:(i + 1) * ATTN_WIDTH], preferred_element_type=F32)

    q = proj(0) * (HEAD_DIM ** -0.5 * LOG2E)
    k = proj(1)
    v = proj(2)
    u = jax.nn.gelu(proj(3))
    g = jax.nn.gelu(proj(4))
    k_ref[...] = k.T[None] if prompt else k
    if prompt:
        for hd in range(N_HEADS):
            v_ref[pl.ds(hd, tm, stride=N_HEADS), :] = v[:, hd * V_DIM:(hd + 1) * V_DIM]
    else:
        v_ref[...] = v

    row = lax.broadcasted_iota(jnp.int32, (CHUNK, CHUNK), 0)
    col = lax.broadcasted_iota(jnp.int32, (CHUNK, CHUNK), 1)
    keep = col <= row
    if seq_in_chunk < CHUNK:
        keep = keep & ((row // seq_in_chunk) == (col // seq_in_chunk))
    for grp in range(N_GROUPS):
        lanes = slice(grp * GROUP, (grp + 1) * GROUP)
        gg = g[:, lanes]
        gvn = gg * lax.rsqrt(jnp.mean(gg * gg, axis=-1, keepdims=True) + EPS) * gn_ref[:, lanes]
        if not prompt:
            gv_ref[:, lanes] = gvn
        wm = jnp.where(keep, ws_ref[grp], 0.0).astype(BF16)
        bcol = bs_ref[:, grp:grp + 1]
        for c in range(tm // CHUNK):
            rows = slice(c * CHUNK, (c + 1) * CHUNK)
            mixed = jnp.dot(wm, gvn[rows, :].astype(BF16), preferred_element_type=F32) + bcol
            s_ref[rows, lanes] = (u[rows, lanes] * mixed).astype(BF16)

    if prompt:
        for hd in range(N_HEADS):
            lanes = slice(hd * V_DIM, (hd + 1) * V_DIM)
            qt = q[:, lanes].T
            feat = lax.broadcasted_iota(jnp.int32, qt.shape, 0)
            qtm_ref[0, hd, 0] = jnp.where(feat < HEAD_DIM, qt, 0.0).astype(BF16)
            qtm_ref[0, hd, 1] = jnp.where(feat >= HEAD_DIM, qt, 0.0).astype(BF16)
            kb_ref[0, hd] = k[:, lanes].astype(BF16)
            vt_ref[0, hd, 0:V_DIM] = v[:, lanes].T.astype(BF16)
            ones_row = lax.broadcasted_iota(jnp.int32, (VT_ROWS - V_DIM, tm), 0) == 0
            vt_ref[0, hd, V_DIM:VT_ROWS] = ones_row.astype(F32).astype(BF16)
    else:
        n_tok = seq_in_chunk
        r = lax.broadcasted_iota(jnp.int32, (N_MAPS * n_tok, ATTN_WIDTH), 0)
        c = lax.broadcasted_iota(jnp.int32, (N_MAPS * n_tok, ATTN_WIDTH), 1)
        own = (r // n_tok) == (c // HEAD_DIM)
        n_q = N_MAPS * n_tok
        for sq in range(tm // n_tok):
            tok = slice(sq * n_tok, (sq + 1) * n_tok)
            q_seq = jnp.concatenate([q[tok]] * N_MAPS, axis=0)
            q_ref[sq, 0:n_q] = jnp.where(own, q_seq, 0.0)
            q_ref[sq, n_q:n_q + n_tok] = k[tok]
            q_ref[sq, n_q + n_tok:n_q + 2 * n_tok] = v[tok]


def _ffn_in(x, nf, wg, wu, wd, nm, win, gn, ws, bs, *, batch, seq_in_chunk, prompt):
    n = x.shape[0]
    tm = min(TOKEN_BLOCK, n)
    nb = n // tm
    per_batch = nb // batch
    tok = lambda w: pl.BlockSpec((tm, w), lambda i: (i, 0))
    in_specs = [
        tok(D_MODEL),
        _const_spec((1, D_MODEL)),
        *_gate_up_specs(), _const_spec(wd.shape),
        _const_spec((1, D_MODEL)),
        _const_spec(win.shape),
        _const_spec((1, GMLP_WIDTH)),
        _const_spec(ws.shape), _const_spec(bs.shape),
    ]
    out_specs = [tok(D_MODEL), tok(ATTN_WIDTH), tok(ATTN_WIDTH), tok(GMLP_WIDTH)]
    out_shape = [
        jax.ShapeDtypeStruct((n, D_MODEL), F32),
        jax.ShapeDtypeStruct((n, ATTN_WIDTH), F32),
        jax.ShapeDtypeStruct((n, ATTN_WIDTH), F32),
        jax.ShapeDtypeStruct((n, GMLP_WIDTH), BF16),
    ]
    if prompt:
        t = n // batch
        out_specs[1] = pl.BlockSpec((1, ATTN_WIDTH, tm), lambda i: (i // per_batch, 0, i % per_batch))
        out_shape[1] = jax.ShapeDtypeStruct((batch, ATTN_WIDTH, t), F32)
        out_specs[2] = pl.BlockSpec((tm * N_HEADS, V_DIM), lambda i: (i, 0))
        out_shape[2] = jax.ShapeDtypeStruct((n * N_HEADS, V_DIM), F32)
        out_specs += [
            pl.BlockSpec((1, N_HEADS, 2, V_DIM, tm), lambda i: (i // per_batch, 0, 0, 0, i % per_batch)),
            pl.BlockSpec((1, N_HEADS, tm, V_DIM), lambda i: (i // per_batch, 0, i % per_batch, 0)),
            pl.BlockSpec((1, N_HEADS, VT_ROWS, tm), lambda i: (i // per_batch, 0, 0, i % per_batch)),
        ]
        out_shape += [
            jax.ShapeDtypeStruct((batch, N_HEADS, 2, V_DIM, t), BF16),
            jax.ShapeDtypeStruct((batch, N_HEADS, t, V_DIM), BF16),
            jax.ShapeDtypeStruct((batch, N_HEADS, VT_ROWS, t), BF16),
        ]
    else:
        rows = (N_MAPS + 2) * seq_in_chunk
        out_specs += [pl.BlockSpec((tm // seq_in_chunk, rows, ATTN_WIDTH), lambda i: (i, 0, 0)),
                      tok(GMLP_WIDTH)]
        out_shape += [
            jax.ShapeDtypeStruct((n // seq_in_chunk, rows, ATTN_WIDTH), F32),
            jax.ShapeDtypeStruct((n, GMLP_WIDTH), F32),
        ]
    return pl.pallas_call(
        functools.partial(_ffn_in_kernel, prompt=prompt, seq_in_chunk=seq_in_chunk),
        grid=(nb,),
        in_specs=in_specs,
        out_specs=out_specs,
        out_shape=out_shape,
        scratch_shapes=[pltpu.VMEM((tm, D_MODEL), BF16), pltpu.VMEM((tm, D_MODEL), F32)],
        compiler_params=pltpu.CompilerParams(dimension_semantics=("arbitrary",),
                                             vmem_limit_bytes=VMEM_LIMIT),
        name="ffn_in_prompt" if prompt else "ffn_in_sample",
    )(x, nf, wg, wu, wd, nm, win, gn, ws, bs)


def _out_ffn_kernel(x1_ref, o_ref, s_ref, wout_ref, nf_ref, wg_ref, wu_ref, wd_ref, nfin_ref,
                    y_ref, hn_ref, acc_ref, *, head_major):
    if head_major:
        o = jnp.concatenate([o_ref[0, hd] for hd in range(N_HEADS)], axis=1)
    else:
        o = o_ref[...]
    x2 = (x1_ref[...]
          + jnp.dot(o, wout_ref[0:ATTN_WIDTH, :], preferred_element_type=F32)
          + jnp.dot(s_ref[...], wout_ref[ATTN_WIDTH:D_MODEL, :], preferred_element_type=F32))
    x3 = _swiglu_half_step(x2, nf_ref, wg_ref, wu_ref, wd_ref, hn_ref, acc_ref)
    y_ref[...] = _rmsnorm(x3, nfin_ref[...])


def _out_ffn(x1, o, s, wout, nf, wg, wu, wd, nfin, *, batch, head_major):
    n = x1.shape[0]
    tm = min(TOKEN_BLOCK, n)
    nb = n // tm
    per_batch = nb // batch
    tok = lambda w: pl.BlockSpec((tm, w), lambda i: (i, 0))
    if head_major:
        o_spec = pl.BlockSpec((1, N_HEADS, tm, V_DIM), lambda i: (i // per_batch, 0, i % per_batch, 0))
    else:
        o_spec = tok(ATTN_WIDTH)
    return pl.pallas_call(
        functools.partial(_out_ffn_kernel, head_major=head_major),
        grid=(nb,),
        in_specs=[
            tok(D_MODEL), o_spec, tok(GMLP_WIDTH),
            _const_spec(wout.shape),
            _const_spec((1, D_MODEL)),
            *_gate_up_specs(), _const_spec(wd.shape),
            _const_spec((1, D_MODEL)),
        ],
        out_specs=tok(D_MODEL),
        out_shape=jax.ShapeDtypeStruct((n, D_MODEL), F32),
        scratch_shapes=[pltpu.VMEM((tm, D_MODEL), BF16), pltpu.VMEM((tm, D_MODEL), F32)],
        compiler_params=pltpu.CompilerParams(dimension_semantics=("arbitrary",),
                                             vmem_limit_bytes=VMEM_LIMIT),
        name="out_ffn_prompt" if head_major else "out_ffn_sample",
    )(x1, o, s, wout, nf, wg, wu, wd, nfin)


def _attn_kernel(qi_ref, ki_ref, pt_ref, qtm_ref, kb_ref, vt_ref, e_ref, b_ref, small_ref, seq_ref,
                 kt_hbm, v_hbm, o_ref, os_ref, m_ref, acc_ref, ms_ref, ls_ref, accs_ref,
                 kbuf, vbuf, page_sem, *, n_sample_steps, groups_per_seq):
    step = pl.program_id(1)
    qi = qi_ref[step]
    ki = ki_ref[step]
    g = pl.program_id(0) * pl.num_programs(1) + step
    slot = g % 2

    def group_copies(group, buf_slot):
        seq = group // groups_per_seq
        first_page = (group % groups_per_seq) * PAGES_PER_STEP
        copies = []
        for i in range(PAGES_PER_STEP):
            page = pt_ref[seq, first_page + i]
            copies.append(pltpu.make_async_copy(kt_hbm.at[page], kbuf.at[buf_slot, i], page_sem.at[buf_slot]))
            copies.append(pltpu.make_async_copy(v_hbm.at[page], vbuf.at[buf_slot, i], page_sem.at[buf_slot]))
        return copies

    @pl.when(g == 0)
    def _():
        for cp in group_copies(0, 0):
            cp.start()

    @pl.when(g + 1 < n_sample_steps)
    def _():
        for cp in group_copies(g + 1, 1 - slot):
            cp.start()

    @pl.when(g < n_sample_steps)
    def _():
        for cp in group_copies(g, slot):
            cp.wait()

    def sample_step():
        n_tok = seq_ref.shape[1] // (N_MAPS + 2)
        n_q = N_MAPS * n_tok
        head_rows = 2 * n_tok
        valid = g < n_sample_steps
        j = jnp.minimum(g, n_sample_steps - 1) % groups_per_seq
        first = j == 0
        last = j == groups_per_seq - 1

        qbd = seq_ref[0, 0:n_q].astype(BF16)
        blocks = []
        for i in range(0, PAGES_PER_STEP, 2):
            keys = jnp.concatenate([kbuf[slot, i].astype(BF16), kbuf[slot, i + 1].astype(BF16)], axis=1)
            blocks.append(jnp.dot(qbd, keys, preferred_element_type=F32))
            yield
        bl = jnp.where(last, b_ref[0], 0.0)
        blocks[-1] = blocks[-1] + jnp.concatenate([jnp.zeros_like(bl), bl], axis=1)
        pad = jnp.zeros((CHUNK - n_tok, ATTN_WIDTH), F32)
        k_new = jnp.concatenate([seq_ref[0, n_q:n_q + n_tok], pad], axis=0).astype(BF16)
        v_new = jnp.concatenate([seq_ref[0, n_q + n_tok:n_q + 2 * n_tok], pad], axis=0)
        s_new = lax.dot_general(qbd, k_new, (((1,), (1,)), ((), ())), preferred_element_type=F32)
        blocks.append(s_new + jnp.where(last, b_ref[1], NEG))
        s = jnp.concatenate(blocks, axis=1) + jnp.where(valid, 0.0, NEG)

        m_prev = jnp.where(first, NEG, ms_ref[...])
        l_prev = jnp.where(first, 0.0, ls_ref[...])
        acc_prev = jnp.where(first, 0.0, accs_ref[...])
        m_new = jnp.maximum(m_prev, jnp.max(s, axis=1, keepdims=True))
        yield
        alpha = jnp.exp2(m_prev - m_new)
        p = jnp.exp2(s - m_new)
        l_new = alpha * l_prev + jnp.sum(p, axis=1, keepdims=True)
        yield
        pv = None
        pb = p.astype(BF16)
        for i in range(PAGES_PER_STEP + 1):
            if i < PAGES_PER_STEP:
                vals = jnp.concatenate([vbuf[slot, i, pl.ds(hd, CHUNK, stride=N_HEADS), :]
                                        for hd in range(N_HEADS)], axis=1)
            else:
                vals = v_new
            d = jnp.dot(pb[:, i * CHUNK:(i + 1) * CHUNK], vals.astype(BF16),
                        preferred_element_type=F32)
            pv = d if pv is None else pv + d
            if i % 3 == 2:
                yield
        acc_new = alpha * acc_prev + pv
        ms_ref[...] = m_new
        ls_ref[...] = l_new
        accs_ref[...] = acc_new

        lam = _lam(small_ref)
        o_all = acc_new * (1.0 / l_new)
        for hd in range(N_HEADS):
            r1 = slice(hd * head_rows, hd * head_rows + n_tok)
            r2 = slice(hd * head_rows + n_tok, (hd + 1) * head_rows)
            lanes = slice(hd * V_DIM, (hd + 1) * V_DIM)
            o = o_all[r1, lanes] - lam * o_all[r2, lanes]
            os_ref[0, :, lanes] = (
                _rmsnorm(o, small_ref[0:1, :]) * (1.0 - LAM_INIT)).astype(BF16)

    @pl.when(ki == 0)
    def _():
        m_ref[...] = jnp.full_like(m_ref, NEG)
        acc_ref[...] = jnp.zeros_like(acc_ref)


    def unit_logits(unit):
        hd, mp, q_lo, q_n, key_blocks = unit
        idx = 2 * hd + mp
        qm = qtm_ref[0, hd, mp, :, q_lo:q_lo + q_n]
        logits = []
        for k_lo, k_n, bias_kind in key_blocks:
            s = jnp.dot(kb_ref[0, hd, k_lo:k_lo + k_n, :], qm, preferred_element_type=F32)
            if bias_kind is not None:
                s = s + e_ref[bias_kind, idx]
            logits.append(s)
        return logits

    def unit_update(unit, logits):
        hd, mp, q_lo, q_n, key_blocks = unit
        idx = 2 * hd + mp
        cols = slice(q_lo, q_lo + q_n)
        m_prev = m_ref[idx, :, cols]
        m_new = m_prev
        for s in logits:
            m_new = jnp.maximum(m_new, jnp.max(s, axis=0, keepdims=True))
        alpha = jnp.exp2(m_prev - m_new)
        pv = None
        for (k_lo, k_n, _), s in zip(key_blocks, logits):
            p = jnp.exp2(s - m_new)
            d = jnp.dot(vt_ref[0, hd, :, k_lo:k_lo + k_n], p.astype(BF16), preferred_element_type=F32)
            pv = d if pv is None else pv + d
        acc_ref[idx, :, cols] = alpha * acc_ref[idx, :, cols] + pv
        m_ref[idx, :, cols] = m_new

    def run(units):
        side = sample_step()
        pending = []
        for unit in units:
            pending.append((unit, unit_logits(unit)))
            next(side, None)
            if len(pending) > PIPE_DEPTH:
                unit_update(*pending.pop(0))
        for item in pending:
            unit_update(*item)
            next(side, None)
        for _ in side:
            pass

    maps = [(hd, mp) for hd in range(N_HEADS) for mp in range(2)]

    @pl.when(ki < qi - 1)
    def _():
        run([(hd, mp, q_lo, SUB, [(0, TK, None)]) for hd, mp in maps for q_lo in range(0, TQ, SUB)])

    @pl.when(ki == qi - 1)
    def _():
        run([u for hd, mp in maps for u in (
            (hd, mp, 0, SUB, [(0, SUB, None), (SUB, SUB, 1)]),
            (hd, mp, SUB, SUB, [(0, TK, None)]))])

    @pl.when(ki == qi)
    def _():
        run([u for hd, mp in maps for u in (
            (hd, mp, 0, SUB, [(0, SUB, 0)]),
            (hd, mp, SUB, SUB, [(0, SUB, 1), (SUB, SUB, 0)]))])
        lam = _lam(small_ref)
        for hd in range(N_HEADS):
            o1 = acc_ref[2 * hd, 0:V_DIM] * (1.0 / acc_ref[2 * hd, V_DIM:V_DIM + 1])
            o2 = acc_ref[2 * hd + 1, 0:V_DIM] * (1.0 / acc_ref[2 * hd + 1, V_DIM:V_DIM + 1])
            o = o1 - lam * o2
            inv = lax.rsqrt(jnp.mean(o * o, axis=0, keepdims=True) + EPS)
            on = (o * inv).T * small_ref[0:1, :] * (1.0 - LAM_INIT)
            o_ref[0, hd] = on.astype(BF16)


def _attention(qtm, kb, vt, e_tiles, b_tiles, small, page_table, seq_pack, cache_kt, cache_v):
    batch, _, _, _, t = qtm.shape
    nq = t // TQ
    pairs = [(q, k) for q in range(nq) for k in range(q + 1)]
    n_steps = len(pairs)
    qi_arr = jnp.asarray([p[0] for p in pairs], jnp.int32)
    ki_arr = jnp.asarray([p[1] for p in pairs], jnp.int32)
    n_seq = seq_pack.shape[0]
    n_tok = seq_pack.shape[1] // (N_MAPS + 2)
    groups_per_seq = page_table.shape[1] // PAGES_PER_STEP
    n_sample_steps = n_seq * groups_per_seq
    assert page_table.shape[1] % PAGES_PER_STEP == 0 and n_sample_steps <= batch * n_steps
    n_rows = N_MAPS * n_tok

    def sample_pos(b, s):
        g = jnp.minimum(b * n_steps + s, n_sample_steps - 1)
        return g // groups_per_seq, g % groups_per_seq

    def seq_spec(rows):
        return pl.BlockSpec((1, rows, ATTN_WIDTH), lambda b, s, qi, ki, pt: (sample_pos(b, s)[0], 0, 0))

    def full(shape, **kw):
        return pl.BlockSpec(shape, lambda b, s, qi, ki, pt: (0,) * len(shape), **kw)

    const = lambda a: full(a.shape, pipeline_mode=pl.Buffered(1))
    grid_spec = pltpu.PrefetchScalarGridSpec(
        num_scalar_prefetch=3,
        grid=(batch, n_steps),
        in_specs=([
            pl.BlockSpec((1, N_HEADS, 2, V_DIM, TQ), lambda b, s, qi, ki, pt: (b, 0, 0, 0, qi[s])),
            pl.BlockSpec((1, N_HEADS, TK, V_DIM), lambda b, s, qi, ki, pt: (b, 0, ki[s], 0)),
            pl.BlockSpec((1, N_HEADS, VT_ROWS, TK), lambda b, s, qi, ki, pt: (b, 0, 0, ki[s])),
            const(e_tiles), const(b_tiles), const(small), seq_spec(seq_pack.shape[1]),
            pl.BlockSpec(memory_space=pl.ANY), pl.BlockSpec(memory_space=pl.ANY)]),
        out_specs=[
            pl.BlockSpec((1, N_HEADS, TQ, V_DIM), lambda b, s, qi, ki, pt: (b, 0, qi[s], 0)),
            seq_spec(n_tok),
        ],
        scratch_shapes=[
            pltpu.VMEM((N_MAPS, 1, TQ), F32),
            pltpu.VMEM((N_MAPS, VT_ROWS, TQ), F32),
            pltpu.VMEM((n_rows, 1), F32),
            pltpu.VMEM((n_rows, 1), F32),
            pltpu.VMEM((n_rows, ATTN_WIDTH), F32),
            pltpu.VMEM((2, PAGES_PER_STEP, ATTN_WIDTH, CHUNK), F32),
            pltpu.VMEM((2, PAGES_PER_STEP, ATTN_WIDTH, CHUNK), F32),
            pltpu.SemaphoreType.DMA((2,)),
        ],
    )
    return pl.pallas_call(
        functools.partial(_attn_kernel, n_sample_steps=n_sample_steps, groups_per_seq=groups_per_seq),
        grid_spec=grid_spec,
        out_shape=[
            jax.ShapeDtypeStruct((batch, N_HEADS, t, V_DIM), BF16),
            jax.ShapeDtypeStruct((n_seq, n_tok, ATTN_WIDTH), BF16),
        ],
        compiler_params=pltpu.CompilerParams(dimension_semantics=("arbitrary", "arbitrary"),
                                             vmem_limit_bytes=VMEM_LIMIT),
        name="attention",
    )(qi_arr, ki_arr, page_table, qtm, kb, vt, e_tiles, b_tiles, small, seq_pack, cache_kt, cache_v)


def _ffn_weights(w_gu, w_down):
    w_gu = w_gu.astype(BF16)
    return w_gu, w_gu, w_down.astype(BF16)


def kernel(x_prompt, x_sample, cache_k, cache_v, page_table, rel_table, norm_ffn1, w_ffn1_gu, w_ffn1_down, norm_mix, w_in, lambda_q1, lambda_k1, lambda_q2, lambda_k2, subln, gmlp_norm, w_spatial, b_spatial, w_out, norm_ffn2, w_ffn2_gu, w_ffn2_down, norm_final):
    depth = cache_k.shape[0]
    assert depth == 1
    batch, t_p, _ = x_prompt.shape
    n_seq, t_s, _ = x_sample.shape
    n_pool, page = cache_k.shape[1], cache_k.shape[2]
    assert page == CHUNK and CHUNK % t_s == 0 and (n_seq * t_s) % CHUNK == 0

    row = lambda a: a.reshape(1, -1)
    wg1, wu1, wd1 = _ffn_weights(w_ffn1_gu[0], w_ffn1_down[0])
    wg2, wu2, wd2 = _ffn_weights(w_ffn2_gu[0], w_ffn2_down[0])
    win = w_in[0].astype(BF16)
    wout = w_out[0].astype(BF16)
    lamv = jnp.stack([lambda_q1[0], lambda_k1[0], lambda_q2[0], lambda_k2[0]])
    small = jnp.concatenate([
        row(subln[0]), jnp.pad(lamv, ((0, 0), (0, V_DIM - HEAD_DIM))), jnp.zeros((3, V_DIM), F32)])
    reps = CHUNK // t_s
    ws_sample = jnp.tile(w_spatial[0][:, :t_s, :t_s], (1, reps, reps))
    bs_sample = jnp.tile(b_spatial[0][:, :t_s], (1, reps)).T
    ffn1 = (row(norm_ffn1[0]), wg1, wu1, wd1)
    mix = (row(norm_mix[0]), win, row(gmlp_norm[0]))

    e_tiles, b_tiles = _bias_tables(rel_table, t_s)

    x1p, kp, vp, sp, qtm, kb, vt = _ffn_in(
        x_prompt.reshape(batch * t_p, D_MODEL), *ffn1, *mix, w_spatial[0], b_spatial[0].T,
        batch=batch, seq_in_chunk=CHUNK, prompt=True)
    x1s, ks, vs, ss, qs, gvs = _ffn_in(
        x_sample.reshape(n_seq * t_s, D_MODEL), *ffn1, *mix, ws_sample, bs_sample,
        batch=1, seq_in_chunk=t_s, prompt=False)

    op, os_ = _attention(
        qtm, kb, vt, e_tiles, b_tiles, small, page_table, qs,
        jnp.transpose(cache_k[0], (0, 2, 3, 1)).reshape(n_pool, ATTN_WIDTH, page),
        cache_v[0].reshape(n_pool, page * N_HEADS, V_DIM))

    ffn2 = (wout, row(norm_ffn2[0]), wg2, wu2, wd2, row(norm_final))
    yp = _out_ffn(x1p, op, sp, *ffn2, batch=batch, head_major=True)
    ys = _out_ffn(x1s, os_.reshape(n_seq * t_s, ATTN_WIDTH), ss, *ffn2, batch=1, head_major=False)

    return (
        yp.reshape(batch, t_p, D_MODEL),
        ys.reshape(n_seq, t_s, D_MODEL),
        jnp.transpose(kp.reshape(depth, batch, N_MAPS, HEAD_DIM, t_p), (0, 1, 4, 2, 3)),
        vp.reshape(depth, batch, t_p, N_HEADS, V_DIM),
        ks.reshape(depth, n_seq, t_s, N_MAPS, HEAD_DIM),
        vs.reshape(depth, n_seq, t_s, N_HEADS, V_DIM),
        gvs.reshape(depth, n_seq, t_s, GMLP_WIDTH),
    )
```

```python
import functools
import math

import jax
import jax.numpy as jnp
from jax import lax
from jax.experimental import pallas as pl
from jax.experimental.pallas import tpu as pltpu

F32 = jnp.float32
BF16 = jnp.bfloat16

D_MODEL = 1024
D_FF = 2816
FF_CHUNK = 256
N_FF_CHUNKS = D_FF // FF_CHUNK
ATTN_WIDTH = 512
HEAD_DIM = 64
N_HEADS = 4
N_MAPS = 2 * N_HEADS
V_DIM = 2 * HEAD_DIM
GMLP_WIDTH = 512
N_GROUPS = 4
GROUP = GMLP_WIDTH // N_GROUPS
CHUNK = 128
N_BUCKETS = 32
MAX_EXACT = N_BUCKETS // 2
MAX_DISTANCE = 128
EPS = 1e-6
NEG = -1e30
BUCKET_THRESHOLDS = tuple(
    math.ceil(MAX_EXACT * (MAX_DISTANCE / MAX_EXACT) ** (k / (N_BUCKETS - MAX_EXACT)))
    for k in range(1, N_BUCKETS - MAX_EXACT))
LAM_INIT = 0.8 - 0.6 * math.exp(-0.3 * 0)
LOG2E = math.log2(math.e)

TOKEN_BLOCK = 512
TQ = 512
TK = 512
SUB = 256
PIPE_DEPTH = 6
VT_ROWS = V_DIM + 16
PAGES_PER_STEP = 8
VMEM_LIMIT = 52 * 1024 * 1024


def _const_spec(shape):
    nd = len(shape)
    return pl.BlockSpec(shape, lambda *_: (0,) * nd, pipeline_mode=pl.Buffered(1))


def _gate_up_specs():
    half = lambda j: pl.BlockSpec((D_MODEL, D_FF), lambda *_: (0, j), pipeline_mode=pl.Buffered(1))
    return [half(0), half(1)]


def _rmsnorm(x, g):
    return x * lax.rsqrt(jnp.mean(x * x, axis=-1, keepdims=True) + EPS) * g


def _lam(small_ref):
    a = jnp.sum(small_ref[1:2, :] * small_ref[2:3, :], axis=1, keepdims=True)
    b = jnp.sum(small_ref[3:4, :] * small_ref[4:5, :], axis=1, keepdims=True)
    return jnp.exp(a) - jnp.exp(b) + LAM_INIT


def _bias_of_distance(d, tab_ref, m):
    n = jnp.maximum(d, 0)
    large = MAX_EXACT
    for thr in BUCKET_THRESHOLDS:
        large = large + (n >= thr).astype(jnp.int32)
    bucket = jnp.where(n < MAX_EXACT, n, large)
    val = jnp.zeros(d.shape, F32)
    for b in range(N_BUCKETS):
        val = jnp.where(bucket == b, tab_ref[b, m], val)
    val = (val - tab_ref[N_BUCKETS - 1, m]) * LOG2E
    return jnp.where(d >= 0, val, NEG)


def _bias_kernel(tab_ref, e_ref, b_ref, *, n_tok):
    m = pl.program_id(0)
    key = lax.broadcasted_iota(jnp.int32, (SUB, SUB), 0)
    qry = lax.broadcasted_iota(jnp.int32, (SUB, SUB), 1)
    e_ref[0, 0] = _bias_of_distance(qry - key, tab_ref, m)
    e_ref[1, 0] = _bias_of_distance(SUB + qry - key, tab_ref, m)
    t = lax.broadcasted_iota(jnp.int32, (n_tok, CHUNK), 0)
    c = lax.broadcasted_iota(jnp.int32, (n_tok, CHUNK), 1)
    b_ref[0] = _bias_of_distance(CHUNK + t - c, tab_ref, m)
    b_ref[1] = _bias_of_distance(t - c, tab_ref, m)


def _bias_tables(rel_table, n_tok):
    return pl.pallas_call(
        functools.partial(_bias_kernel, n_tok=n_tok),
        grid=(N_MAPS,),
        in_specs=[pl.BlockSpec(memory_space=pltpu.SMEM)],
        out_specs=[
            pl.BlockSpec((2, 1, SUB, SUB), lambda m: (0, m, 0, 0)),
            pl.BlockSpec((2, n_tok, CHUNK), lambda m: (0, m, 0)),
        ],
        out_shape=[
            jax.ShapeDtypeStruct((2, N_MAPS, SUB, SUB), F32),
            jax.ShapeDtypeStruct((2, N_MAPS * n_tok, CHUNK), F32),
        ],
        name="bias_tables",
    )(rel_table)


def _swiglu_half_step(x, norm_ref, wg_ref, wu_ref, wd_ref, hn_ref, acc_ref):
    hn_ref[...] = _rmsnorm(x, norm_ref[...]).astype(BF16)

    def gate_up(c):
        hn = hn_ref[...]
        cols = slice(c * FF_CHUNK, (c + 1) * FF_CHUNK)
        return (jnp.dot(hn, wg_ref[:, cols], preferred_element_type=F32),
                jnp.dot(hn, wu_ref[:, cols], preferred_element_type=F32))

    def down(c, gate, up):
        act = (jax.nn.silu(gate) * up).astype(BF16)
        d = jnp.dot(act, wd_ref[c * FF_CHUNK:(c + 1) * FF_CHUNK, :], preferred_element_type=F32)
        if c == 0:
            acc_ref[...] = d
        else:
            acc_ref[...] += d

    pending = gate_up(0)
    for c in range(N_FF_CHUNKS):
        nxt = gate_up(c + 1) if c + 1 < N_FF_CHUNKS else None
        down(c, *pending)
        pending = nxt
    return x + 0.5 * acc_ref[...]


def _ffn_in_kernel(x_ref, nf_ref, wg_ref, wu_ref, wd_ref, nm_ref, win_ref, gn_ref, ws_ref, bs_ref,
                   *rest, prompt, seq_in_chunk):
    if prompt:
        x1_ref, k_ref, v_ref, s_ref, qtm_ref, kb_ref, vt_ref, hn_ref, acc_ref = rest
    else:
        x1_ref, k_ref, v_ref, s_ref, q_ref, gv_ref, hn_ref, acc_ref = rest
    tm = x_ref.shape[0]

    x1 = _swiglu_half_step(x_ref[...], nf_ref, wg_ref, wu_ref, wd_ref, hn_ref, acc_ref)
    x1_ref[...] = x1
    h = _rmsnorm(x1, nm_ref[...]).astype(BF16)

    def proj(i):
        return jnp.dot(h, win_ref[:, i * ATTN_WIDTH:(i + 1) * ATTN_WIDTH], preferred_element_type=F32)

    q = proj(0) * (HEAD_DIM ** -0.5 * LOG2E)
    k = proj(1)
    v = proj(2)
    u = jax.nn.gelu(proj(3))
    g = jax.nn.gelu(proj(4))
    k_ref[...] = k.T[None] if prompt else k
    if prompt:
        for hd in range(N_HEADS):
            v_ref[pl.ds(hd, tm, stride=N_HEADS), :] = v[:, hd * V_DIM:(hd + 1) * V_DIM]
    else:
        v_ref[...] = v

    row = lax.broadcasted_iota(jnp.int32, (CHUNK, CHUNK), 0)
    col = lax.broadcasted_iota(jnp.int32, (CHUNK, CHUNK), 1)
    keep = col <= row
    if seq_in_chunk < CHUNK:
        keep = keep & ((row // seq_in_chunk) == (col // seq_in_chunk))
    for grp in range(N_GROUPS):
        lanes = slice(grp * GROUP, (grp + 1) * GROUP)
        gg = g[:, lanes]
        gvn = gg * lax.rsqrt(jnp.mean(gg * gg, axis=-1, keepdims=True) + EPS) * gn_ref[:, lanes]
        if not prompt:
            gv_ref[:, lanes] = gvn
        wm = jnp.where(keep, ws_ref[grp], 0.0).astype(BF16)
        bcol = bs_ref[:, grp:grp + 1]
        for c in range(tm // CHUNK):
            rows = slice(c * CHUNK, (c + 1) * CHUNK)
            mixed = jnp.dot(wm, gvn[rows, :].astype(BF16), preferred_element_type=F32) + bcol
            s_ref[rows, lanes] = (u[rows, lanes] * mixed).astype(BF16)

    if prompt:
        for hd in range(N_HEADS):
            lanes = slice(hd * V_DIM, (hd + 1) * V_DIM)
            qt = q[:, lanes].T
            feat = lax.broadcasted_iota(jnp.int32, qt.shape, 0)
            qtm_ref[0, hd, 0] = jnp.where(feat < HEAD_DIM, qt, 0.0).astype(BF16)
            qtm_ref[0, hd, 1] = jnp.where(feat >= HEAD_DIM, qt, 0.0).astype(BF16)
            kb_ref[0, hd] = k[:, lanes].astype(BF16)
            vt_ref[0, hd, 0:V_DIM] = v[:, lanes].T.astype(BF16)
            ones_row = lax.broadcasted_iota(jnp.int32, (VT_ROWS - V_DIM, tm), 0) == 0
            vt_ref[0, hd, V_DIM:VT_ROWS] = ones_row.astype(F32).astype(BF16)
    else:
        n_tok = seq_in_chunk
        r = lax.broadcasted_iota(jnp.int32, (N_MAPS * n_tok, ATTN_WIDTH), 0)
        c = lax.broadcasted_iota(jnp.int32, (N_MAPS * n_tok, ATTN_WIDTH), 1)
        own = (r // n_tok) == (c // HEAD_DIM)
        n_q = N_MAPS * n_tok
        for sq in range(tm // n_tok):
            tok = slice(sq * n_tok, (sq + 1) * n_tok)
            q_seq = jnp.concatenate([q[tok]] * N_MAPS, axis=0)
            q_ref[sq, 0:n_q] = jnp.where(own, q_seq, 0.0)
            q_ref[sq, n_q:n_q + n_tok] = k[tok]
            q_ref[sq, n_q + n_tok:n_q + 2 * n_tok] = v[tok]


def _ffn_in(x, nf, wg, wu, wd, nm, win, gn, ws, bs, *, batch, seq_in_chunk, prompt):
    n = x.shape[0]
    tm = min(TOKEN_BLOCK, n)
    nb = n // tm
    per_batch = nb // batch
    tok = lambda w: pl.BlockSpec((tm, w), lambda i: (i, 0))
    in_specs = [
        tok(D_MODEL),
        _const_spec((1, D_MODEL)),
        *_gate_up_specs(), _const_spec(wd.shape),
        _const_spec((1, D_MODEL)),
        _const_spec(win.shape),
        _const_spec((1, GMLP_WIDTH)),
        _const_spec(ws.shape), _const_spec(bs.shape),
    ]
    out_specs = [tok(D_MODEL), tok(ATTN_WIDTH), tok(ATTN_WIDTH), tok(GMLP_WIDTH)]
    out_shape = [
        jax.ShapeDtypeStruct((n, D_MODEL), F32),
        jax.ShapeDtypeStruct((n, ATTN_WIDTH), F32),
        jax.ShapeDtypeStruct((n, ATTN_WIDTH), F32),
        jax.ShapeDtypeStruct((n, GMLP_WIDTH), BF16),
    ]
    if prompt:
        t = n // batch
        out_specs[1] = pl.BlockSpec((1, ATTN_WIDTH, tm), lambda i: (i // per_batch, 0, i % per_batch))
        out_shape[1] = jax.ShapeDtypeStruct((batch, ATTN_WIDTH, t), F32)
        out_specs[2] = pl.BlockSpec((tm * N_HEADS, V_DIM), lambda i: (i, 0))
        out_shape[2] = jax.ShapeDtypeStruct((n * N_HEADS, V_DIM), F32)
        out_specs += [
            pl.BlockSpec((1, N_HEADS, 2, V_DIM, tm), lambda i: (i // per_batch, 0, 0, 0, i % per_batch)),
            pl.BlockSpec((1, N_HEADS, tm, V_DIM), lambda i: (i // per_batch, 0, i % per_batch, 0)),
            pl.BlockSpec((1, N_HEADS, VT_ROWS, tm), lambda i: (i // per_batch, 0, 0, i % per_batch)),
        ]
        out_shape += [
            jax.ShapeDtypeStruct((batch, N_HEADS, 2, V_DIM, t), BF16),
            jax.ShapeDtypeStruct((batch, N_HEADS, t, V_DIM), BF16),
            jax.ShapeDtypeStruct((batch, N_HEADS, VT_ROWS, t), BF16),
        ]
    else:
        rows = (N_MAPS + 2) * seq_in_chunk
        out_specs += [pl.BlockSpec((tm // seq_in_chunk, rows, ATTN_WIDTH), lambda i: (i, 0, 0)),
                      tok(GMLP_WIDTH)]
        out_shape += [
            jax.ShapeDtypeStruct((n // seq_in_chunk, rows, ATTN_WIDTH), F32),
            jax.ShapeDtypeStruct((n, GMLP_WIDTH), F32),
        ]
    return pl.pallas_call(
        functools.partial(_ffn_in_kernel, prompt=prompt, seq_in_chunk=seq_in_chunk),
        grid=(nb,),
        in_specs=in_specs,
        out_specs=out_specs,
        out_shape=out_shape,
        scratch_shapes=[pltpu.VMEM((tm, D_MODEL), BF16), pltpu.VMEM((tm, D_MODEL), F32)],
        compiler_params=pltpu.CompilerParams(dimension_semantics=("arbitrary",),
                                             vmem_limit_bytes=VMEM_LIMIT),
        name="ffn_in_prompt" if prompt else "ffn_in_sample",
    )(x, nf, wg, wu, wd, nm, win, gn, ws, bs)


def _out_ffn_kernel(x1_ref, o_ref, s_ref, wout_ref, nf_ref, wg_ref, wu_ref, wd_ref, nfin_ref,
                    y_ref, hn_ref, acc_ref, *, head_major):
    if head_major:
        o = jnp.concatenate([o_ref[0, hd] for hd in range(N_HEADS)], axis=1)
    else:
        o = o_ref[...]
    x2 = (x1_ref[...]
          + jnp.dot(o, wout_ref[0:ATTN_WIDTH, :], preferred_element_type=F32)
          + jnp.dot(s_ref[...], wout_ref[ATTN_WIDTH:D_MODEL, :], preferred_element_type=F32))
    x3 = _swiglu_half_step(x2, nf_ref, wg_ref, wu_ref, wd_ref, hn_ref, acc_ref)
    y_ref[...] = _rmsnorm(x3, nfin_ref[...])


def _out_ffn(x1, o, s, wout, nf, wg, wu, wd, nfin, *, batch, head_major):
    n = x1.shape[0]
    tm = min(TOKEN_BLOCK, n)
    nb = n // tm
    per_batch = nb // batch
    tok = lambda w: pl.BlockSpec((tm, w), lambda i: (i, 0))
    if head_major:
        o_spec = pl.BlockSpec((1, N_HEADS, tm, V_DIM), lambda i: (i // per_batch, 0, i % per_batch, 0))
    else:
        o_spec = tok(ATTN_WIDTH)
    return pl.pallas_call(
        functools.partial(_out_ffn_kernel, head_major=head_major),
        grid=(nb,),
        in_specs=[
            tok(D_MODEL), o_spec, tok(GMLP_WIDTH),
            _const_spec(wout.shape),
            _const_spec((1, D_MODEL)),
            *_gate_up_specs(), _const_spec(wd.shape),
            _const_spec((1, D_MODEL)),
        ],
        out_specs=tok(D_MODEL),
        out_shape=jax.ShapeDtypeStruct((n, D_MODEL), F32),
        scratch_shapes=[pltpu.VMEM((tm, D_MODEL), BF16), pltpu.VMEM((tm, D_MODEL), F32)],
        compiler_params=pltpu.CompilerParams(dimension_semantics=("arbitrary",),
                                             vmem_limit_bytes=VMEM_LIMIT),
        name="out_ffn_prompt" if head_major else "out_ffn_sample",
    )(x1, o, s, wout, nf, wg, wu, wd, nfin)


def _attn_kernel(qi_ref, ki_ref, pt_ref, qtm_ref, kb_ref, vt_ref, e_ref, b_ref, small_ref, seq_ref,
                 kt_hbm, v_hbm, o_ref, os_ref, m_ref, acc_ref, ms_ref, ls_ref, accs_ref,
                 kbuf, vbuf, page_sem, *, n_sample_steps, groups_per_seq):
    step = pl.program_id(1)
    qi = qi_ref[step]
    ki = ki_ref[step]
    g = pl.program_id(0) * pl.num_programs(1) + step
    slot = g % 2

    def group_copies(group, buf_slot):
        seq = group // groups_per_seq
        first_page = (group % groups_per_seq) * PAGES_PER_STEP
        copies = []
        for i in range(PAGES_PER_STEP):
            page = pt_ref[seq, first_page + i]
            copies.append(pltpu.make_async_copy(kt_hbm.at[page], kbuf.at[buf_slot, i], page_sem.at[buf_slot]))
            copies.append(pltpu.make_async_copy(v_hbm.at[page], vbuf.at[buf_slot, i], page_sem.at[buf_slot]))
        return copies

    @pl.when(g == 0)
    def _():
        for cp in group_copies(0, 0):
            cp.start()

    @pl.when(g + 1 < n_sample_steps)
    def _():
        for cp in group_copies(g + 1, 1 - slot):
            cp.start()

    @pl.when(g < n_sample_steps)
    def _():
        for cp in group_copies(g, slot):
            cp.wait()

    def sample_step():
        n_tok = seq_ref.shape[1] // (N_MAPS + 2)
        n_q = N_MAPS * n_tok
        head_rows = 2 * n_tok
        valid = g < n_sample_steps
        j = jnp.minimum(g, n_sample_steps - 1) % groups_per_seq
        first = j == 0
        last = j == groups_per_seq - 1

        qbd = seq_ref[0, 0:n_q].astype(BF16)
        blocks = []
        for i in range(0, PAGES_PER_STEP, 2):
            keys = jnp.concatenate([kbuf[slot, i].astype(BF16), kbuf[slot, i + 1].astype(BF16)], axis=1)
            blocks.append(jnp.dot(qbd, keys, preferred_element_type=F32))
            yield
        bl = jnp.where(last, b_ref[0], 0.0)
        blocks[-1] = blocks[-1] + jnp.concatenate([jnp.zeros_like(bl), bl], axis=1)
        pad = jnp.zeros((CHUNK - n_tok, ATTN_WIDTH), F32)
        k_new = jnp.concatenate([seq_ref[0, n_q:n_q + n_tok], pad], axis=0).astype(BF16)
        v_new = jnp.concatenate([seq_ref[0, n_q + n_tok:n_q + 2 * n_tok], pad], axis=0)
        s_new = lax.dot_general(qbd, k_new, (((1,), (1,)), ((), ())), preferred_element_type=F32)
        blocks.append(s_new + jnp.where(last, b_ref[1], NEG))
        s = jnp.concatenate(blocks, axis=1) + jnp.where(valid, 0.0, NEG)

        m_prev = jnp.where(first, NEG, ms_ref[...])
        l_prev = jnp.where(first, 0.0, ls_ref[...])
        acc_prev = jnp.where(first, 0.0, accs_ref[...])
        m_new = jnp.maximum(m_prev, jnp.max(s, axis=1, keepdims=True))
        yield
        alpha = jnp.exp2(m_prev - m_new)
        p = jnp.exp2(s - m_new)
        l_new = alpha * l_prev + jnp.sum(p, axis=1, keepdims=True)
        yield
        pv = None
        pb = p.astype(BF16)
        for i in range(PAGES_PER_STEP + 1):
            if i < PAGES_PER_STEP:
                vals = jnp.concatenate([vbuf[slot, i, pl.ds(hd, CHUNK, stride=N_HEADS), :]
                                        for hd in range(N_HEADS)], axis=1)
            else:
                vals = v_new
            d = jnp.dot(pb[:, i * CHUNK:(i + 1) * CHUNK], vals.astype(BF16),
                        preferred_element_type=F32)
            pv = d if pv is None else pv + d
            if i % 3 == 2:
                yield
        acc_new = alpha * acc_prev + pv
        ms_ref[...] = m_new
        ls_ref[...] = l_new
        accs_ref[...] = acc_new

        lam = _lam(small_ref)
        o_all = acc_new * (1.0 / l_new)
        for hd in range(N_HEADS):
            r1 = slice(hd * head_rows, hd * head_rows + n_tok)
            r2 = slice(hd * head_rows + n_tok, (hd + 1) * head_rows)
            lanes = slice(hd * V_DIM, (hd + 1) * V_DIM)
            o = o_all[r1, lanes] - lam * o_all[r2, lanes]
            os_ref[0, :, lanes] = (
                _rmsnorm(o, small_ref[0:1, :]) * (1.0 - LAM_INIT)).astype(BF16)

    @pl.when(ki == 0)
    def _():
        m_ref[...] = jnp.full_like(m_ref, NEG)
        acc_ref[...] = jnp.zeros_like(acc_ref)


    def unit_logits(unit):
        hd, mp, q_lo, q_n, key_blocks = unit
        idx = 2 * hd + mp
        qm = qtm_ref[0, hd, mp, :, q_lo:q_lo + q_n]
        logits = []
        for k_lo, k_n, bias_kind in key_blocks:
            s = jnp.dot(kb_ref[0, hd, k_lo:k_lo + k_n, :], qm, preferred_element_type=F32)
            if bias_kind is not None:
                s = s + e_ref[bias_kind, idx]
            logits.append(s)
        return logits

    def unit_update(unit, logits):
        hd, mp, q_lo, q_n, key_blocks = unit
        idx = 2 * hd + mp
        cols = slice(q_lo, q_lo + q_n)
        m_prev = m_ref[idx, :, cols]
        m_new = m_prev
        for s in logits:
            m_new = jnp.maximum(m_new, jnp.max(s, axis=0, keepdims=True))
        alpha = jnp.exp2(m_prev - m_new)
        pv = None
        for (k_lo, k_n, _), s in zip(key_blocks, logits):
            p = jnp.exp2(s - m_new)
            d = jnp.dot(vt_ref[0, hd, :, k_lo:k_lo + k_n], p.astype(BF16), preferred_element_type=F32)
            pv = d if pv is None else pv + d
        acc_ref[idx, :, cols] = alpha * acc_ref[idx, :, cols] + pv
        m_ref[idx, :, cols] = m_new

    def run(units):
        side = sample_step()
        pending = []
        for n, unit in enumerate(units):
            pending.append((unit, unit_logits(unit)))
            if n % 2 == 0:
                next(side, None)
            if len(pending) > PIPE_DEPTH:
                unit_update(*pending.pop(0))
        for item in pending:
            unit_update(*item)
            next(side, None)
        for _ in side:
            pass

    maps = [(hd, mp) for hd in range(N_HEADS) for mp in range(2)]

    @pl.when(ki < qi - 1)
    def _():
        run([(hd, mp, q_lo, SUB, [(0, TK, None)]) for hd, mp in maps for q_lo in range(0, TQ, SUB)])

    @pl.when(ki == qi - 1)
    def _():
        run([u for hd, mp in maps for u in (
            (hd, mp, 0, SUB, [(0, SUB, None), (SUB, SUB, 1)]),
            (hd, mp, SUB, SUB, [(0, TK, None)]))])

    @pl.when(ki == qi)
    def _():
        run([u for hd, mp in maps for u in (
            (hd, mp, 0, SUB, [(0, SUB, 0)]),
            (hd, mp, SUB, SUB, [(0, SUB, 1), (SUB, SUB, 0)]))])
        lam = _lam(small_ref)
        for hd in range(N_HEADS):
            o1 = acc_ref[2 * hd, 0:V_DIM] * (1.0 / acc_ref[2 * hd, V_DIM:V_DIM + 1])
            o2 = acc_ref[2 * hd + 1, 0:V_DIM] * (1.0 / acc_ref[2 * hd + 1, V_DIM:V_DIM + 1])
            o = o1 - lam * o2
            inv = lax.rsqrt(jnp.mean(o * o, axis=0, keepdims=True) + EPS)
            on = (o * inv).T * small_ref[0:1, :] * (1.0 - LAM_INIT)
            o_ref[0, hd] = on.astype(BF16)


def _attention(qtm, kb, vt, e_tiles, b_tiles, small, page_table, seq_pack, cache_kt, cache_v):
    batch, _, _, _, t = qtm.shape
    nq = t // TQ
    pairs = [(q, k) for q in range(nq) for k in range(q + 1)]
    n_steps = len(pairs)
    qi_arr = jnp.asarray([p[0] for p in pairs], jnp.int32)
    ki_arr = jnp.asarray([p[1] for p in pairs], jnp.int32)
    n_seq = seq_pack.shape[0]
    n_tok = seq_pack.shape[1] // (N_MAPS + 2)
    groups_per_seq = page_table.shape[1] // PAGES_PER_STEP
    n_sample_steps = n_seq * groups_per_seq
    assert page_table.shape[1] % PAGES_PER_STEP == 0 and n_sample_steps <= batch * n_steps
    n_rows = N_MAPS * n_tok

    def sample_pos(b, s):
        g = jnp.minimum(b * n_steps + s, n_sample_steps - 1)
        return g // groups_per_seq, g % groups_per_seq

    def seq_spec(rows):
        return pl.BlockSpec((1, rows, ATTN_WIDTH), lambda b, s, qi, ki, pt: (sample_pos(b, s)[0], 0, 0))

    def full(shape, **kw):
        return pl.BlockSpec(shape, lambda b, s, qi, ki, pt: (0,) * len(shape), **kw)

    const = lambda a: full(a.shape, pipeline_mode=pl.Buffered(1))
    grid_spec = pltpu.PrefetchScalarGridSpec(
        num_scalar_prefetch=3,
        grid=(batch, n_steps),
        in_specs=([
            pl.BlockSpec((1, N_HEADS, 2, V_DIM, TQ), lambda b, s, qi, ki, pt: (b, 0, 0, 0, qi[s])),
            pl.BlockSpec((1, N_HEADS, TK, V_DIM), lambda b, s, qi, ki, pt: (b, 0, ki[s], 0)),
            pl.BlockSpec((1, N_HEADS, VT_ROWS, TK), lambda b, s, qi, ki, pt: (b, 0, 0, ki[s])),
            const(e_tiles), const(b_tiles), const(small), seq_spec(seq_pack.shape[1]),
            pl.BlockSpec(memory_space=pl.ANY), pl.BlockSpec(memory_space=pl.ANY)]),
        out_specs=[
            pl.BlockSpec((1, N_HEADS, TQ, V_DIM), lambda b, s, qi, ki, pt: (b, 0, qi[s], 0)),
            seq_spec(n_tok),
        ],
        scratch_shapes=[
            pltpu.VMEM((N_MAPS, 1, TQ), F32),
            pltpu.VMEM((N_MAPS, VT_ROWS, TQ), F32),
            pltpu.VMEM((n_rows, 1), F32),
            pltpu.VMEM((n_rows, 1), F32),
            pltpu.VMEM((n_rows, ATTN_WIDTH), F32),
            pltpu.VMEM((2, PAGES_PER_STEP, ATTN_WIDTH, CHUNK), F32),
            pltpu.VMEM((2, PAGES_PER_STEP, ATTN_WIDTH, CHUNK), F32),
            pltpu.SemaphoreType.DMA((2,)),
        ],
    )
    return pl.pallas_call(
        functools.partial(_attn_kernel, n_sample_steps=n_sample_steps, groups_per_seq=groups_per_seq),
        grid_spec=grid_spec,
        out_shape=[
            jax.ShapeDtypeStruct((batch, N_HEADS, t, V_DIM), BF16),
            jax.ShapeDtypeStruct((n_seq, n_tok, ATTN_WIDTH), BF16),
        ],
        compiler_params=pltpu.CompilerParams(dimension_semantics=("arbitrary", "arbitrary"),
                                             vmem_limit_bytes=VMEM_LIMIT),
        name="attention",
    )(qi_arr, ki_arr, page_table, qtm, kb, vt, e_tiles, b_tiles, small, seq_pack, cache_kt, cache_v)


def _ffn_weights(w_gu, w_down):
    w_gu = w_gu.astype(BF16)
    return w_gu, w_gu, w_down.astype(BF16)


def kernel(x_prompt, x_sample, cache_k, cache_v, page_table, rel_table, norm_ffn1, w_ffn1_gu, w_ffn1_down, norm_mix, w_in, lambda_q1, lambda_k1, lambda_q2, lambda_k2, subln, gmlp_norm, w_spatial, b_spatial, w_out, norm_ffn2, w_ffn2_gu, w_ffn2_down, norm_final):
    depth = cache_k.shape[0]
    assert depth == 1
    batch, t_p, _ = x_prompt.shape
    n_seq, t_s, _ = x_sample.shape
    n_pool, page = cache_k.shape[1], cache_k.shape[2]
    assert page == CHUNK and CHUNK % t_s == 0 and (n_seq * t_s) % CHUNK == 0

    row = lambda a: a.reshape(1, -1)
    wg1, wu1, wd1 = _ffn_weights(w_ffn1_gu[0], w_ffn1_down[0])
    wg2, wu2, wd2 = _ffn_weights(w_ffn2_gu[0], w_ffn2_down[0])
    win = w_in[0].astype(BF16)
    wout = w_out[0].astype(BF16)
    lamv = jnp.stack([lambda_q1[0], lambda_k1[0], lambda_q2[0], lambda_k2[0]])
    small = jnp.concatenate([
        row(subln[0]), jnp.pad(lamv, ((0, 0), (0, V_DIM - HEAD_DIM))), jnp.zeros((3, V_DIM), F32)])
    reps = CHUNK // t_s
    ws_sample = jnp.tile(w_spatial[0][:, :t_s, :t_s], (1, reps, reps))
    bs_sample = jnp.tile(b_spatial[0][:, :t_s], (1, reps)).T
    ffn1 = (row(norm_ffn1[0]), wg1, wu1, wd1)
    mix = (row(norm_mix[0]), win, row(gmlp_norm[0]))

    e_tiles, b_tiles = _bias_tables(rel_table, t_s)

    x1p, kp, vp, sp, qtm, kb, vt = _ffn_in(
        x_prompt.reshape(batch * t_p, D_MODEL), *ffn1, *mix, w_spatial[0], b_spatial[0].T,
        batch=batch, seq_in_chunk=CHUNK, prompt=True)
    x1s, ks, vs, ss, qs, gvs = _ffn_in(
        x_sample.reshape(n_seq * t_s, D_MODEL), *ffn1, *mix, ws_sample, bs_sample,
        batch=1, seq_in_chunk=t_s, prompt=False)

    op, os_ = _attention(
        qtm, kb, vt, e_tiles, b_tiles, small, page_table, qs,
        jnp.transpose(cache_k[0], (0, 2, 3, 1)).reshape(n_pool, ATTN_WIDTH, page),
        cache_v[0].reshape(n_pool, page * N_HEADS, V_DIM))

    ffn2 = (wout, row(norm_ffn2[0]), wg2, wu2, wd2, row(norm_final))
    yp = _out_ffn(x1p, op, sp, *ffn2, batch=batch, head_major=True)
    ys = _out_ffn(x1s, os_.reshape(n_seq * t_s, ATTN_WIDTH), ss, *ffn2, batch=1, head_major=False)

    return (
        yp.reshape(batch, t_p, D_MODEL),
        ys.reshape(n_seq, t_s, D_MODEL),
        jnp.transpose(kp.reshape(depth, batch, N_MAPS, HEAD_DIM, t_p), (0, 1, 4, 2, 3)),
        vp.reshape(depth, batch, t_p, N_HEADS, V_DIM),
        ks.reshape(depth, n_seq, t_s, N_MAPS, HEAD_DIM),
        vs.reshape(depth, n_seq, t_s, N_HEADS, V_DIM),
        gvs.reshape(depth, n_seq, t_s, GMLP_WIDTH),
    )
```

```python
import functools
import math

import jax
import jax.numpy as jnp
from jax import lax
from jax.experimental import pallas as pl
from jax.experimental.pallas import tpu as pltpu

F32 = jnp.float32
BF16 = jnp.bfloat16

D_MODEL = 1024
D_FF = 2816
FF_CHUNK = 256
N_FF_CHUNKS = D_FF // FF_CHUNK
ATTN_WIDTH = 512
HEAD_DIM = 64
N_HEADS = 4
N_MAPS = 2 * N_HEADS
V_DIM = 2 * HEAD_DIM
GMLP_WIDTH = 512
N_GROUPS = 4
GROUP = GMLP_WIDTH // N_GROUPS
CHUNK = 128
N_BUCKETS = 32
MAX_EXACT = N_BUCKETS // 2
MAX_DISTANCE = 128
EPS = 1e-6
NEG = -1e30
BUCKET_THRESHOLDS = tuple(
    math.ceil(MAX_EXACT * (MAX_DISTANCE / MAX_EXACT) ** (k / (N_BUCKETS - MAX_EXACT)))
    for k in range(1, N_BUCKETS - MAX_EXACT))
LAM_INIT = 0.8 - 0.6 * math.exp(-0.3 * 0)
LOG2E = math.log2(math.e)

TOKEN_BLOCK = 512
TQ = 512
TK = 512
SUB = 256
PIPE_DEPTH = 6
NEW_ROWS = 16
VT_ROWS = V_DIM + 16
PAGES_PER_STEP = 8
VMEM_LIMIT = 56 * 1024 * 1024


def _const_spec(shape):
    nd = len(shape)
    return pl.BlockSpec(shape, lambda *_: (0,) * nd, pipeline_mode=pl.Buffered(1))


def _gate_up_specs():
    half = lambda j: pl.BlockSpec((D_MODEL, D_FF), lambda *_: (0, j), pipeline_mode=pl.Buffered(1))
    return [half(0), half(1)]


def _rmsnorm(x, g):
    return x * lax.rsqrt(jnp.mean(x * x, axis=-1, keepdims=True) + EPS) * g


def _lam(small_ref):
    a = jnp.sum(small_ref[1:2, :] * small_ref[2:3, :], axis=1, keepdims=True)
    b = jnp.sum(small_ref[3:4, :] * small_ref[4:5, :], axis=1, keepdims=True)
    return jnp.exp(a) - jnp.exp(b) + LAM_INIT


def _bias_of_distance(d, tab_ref, m):
    n = jnp.maximum(d, 0)
    large = MAX_EXACT
    for thr in BUCKET_THRESHOLDS:
        large = large + (n >= thr).astype(jnp.int32)
    bucket = jnp.where(n < MAX_EXACT, n, large)
    val = jnp.zeros(d.shape, F32)
    for b in range(N_BUCKETS):
        val = jnp.where(bucket == b, tab_ref[b, m], val)
    val = (val - tab_ref[N_BUCKETS - 1, m]) * LOG2E
    return jnp.where(d >= 0, val, NEG)


def _bias_kernel(tab_ref, e_ref, b_ref, *, n_tok):
    m = pl.program_id(0)
    key = lax.broadcasted_iota(jnp.int32, (SUB, SUB), 0)
    qry = lax.broadcasted_iota(jnp.int32, (SUB, SUB), 1)
    e_ref[0, 0] = _bias_of_distance(qry - key, tab_ref, m)
    e_ref[1, 0] = _bias_of_distance(SUB + qry - key, tab_ref, m)
    t = lax.broadcasted_iota(jnp.int32, (n_tok, CHUNK), 0)
    c = lax.broadcasted_iota(jnp.int32, (n_tok, CHUNK), 1)
    b_ref[0] = _bias_of_distance(CHUNK + t - c, tab_ref, m)
    b_ref[1] = _bias_of_distance(t - c, tab_ref, m)


def _bias_tables(rel_table, n_tok):
    return pl.pallas_call(
        functools.partial(_bias_kernel, n_tok=n_tok),
        grid=(N_MAPS,),
        in_specs=[pl.BlockSpec(memory_space=pltpu.SMEM)],
        out_specs=[
            pl.BlockSpec((2, 1, SUB, SUB), lambda m: (0, m, 0, 0)),
            pl.BlockSpec((2, n_tok, CHUNK), lambda m: (0, m, 0)),
        ],
        out_shape=[
            jax.ShapeDtypeStruct((2, N_MAPS, SUB, SUB), F32),
            jax.ShapeDtypeStruct((2, N_MAPS * n_tok, CHUNK), F32),
        ],
        name="bias_tables",
    )(rel_table)


def _swiglu_half_step(x, norm_ref, wg_ref, wu_ref, wd_ref, hn_ref, acc_ref):
    hn_ref[...] = _rmsnorm(x, norm_ref[...]).astype(BF16)

    def gate_up(c):
        hn = hn_ref[...]
        cols = slice(c * FF_CHUNK, (c + 1) * FF_CHUNK)
        return (jnp.dot(hn, wg_ref[:, cols], preferred_element_type=F32),
                jnp.dot(hn, wu_ref[:, cols], preferred_element_type=F32))

    def down(c, gate, up):
        act = (jax.nn.silu(gate) * up).astype(BF16)
        d = jnp.dot(act, wd_ref[c * FF_CHUNK:(c + 1) * FF_CHUNK, :], preferred_element_type=F32)
        if c == 0:
            acc_ref[...] = d
        else:
            acc_ref[...] += d

    pending = gate_up(0)
    for c in range(N_FF_CHUNKS):
        nxt = gate_up(c + 1) if c + 1 < N_FF_CHUNKS else None
        down(c, *pending)
        pending = nxt
    return x + 0.5 * acc_ref[...]


def _ffn_in_kernel(x_ref, nf_ref, wg_ref, wu_ref, wd_ref, nm_ref, win_ref, gn_ref, ws_ref, bs_ref,
                   *rest, prompt, seq_in_chunk):
    if prompt:
        x1_ref, k_ref, v_ref, s_ref, qtm_ref, kb_ref, vt_ref, hn_ref, acc_ref = rest
    else:
        x1_ref, k_ref, v_ref, s_ref, q_ref, gv_ref, hn_ref, acc_ref = rest
    tm = x_ref.shape[0]

    x1 = _swiglu_half_step(x_ref[...], nf_ref, wg_ref, wu_ref, wd_ref, hn_ref, acc_ref)
    x1_ref[...] = x1
    h = _rmsnorm(x1, nm_ref[...]).astype(BF16)

    def proj(i):
        return jnp.dot(h, win_ref[:, i * ATTN_WIDTH:(i + 1) * ATTN_WIDTH], preferred_element_type=F32)

    q = proj(0) * (HEAD_DIM ** -0.5 * LOG2E)
    k = proj(1)
    v = proj(2)
    u = jax.nn.gelu(proj(3))
    g = jax.nn.gelu(proj(4))
    k_ref[...] = k.T[None] if prompt else k
    if prompt:
        for hd in range(N_HEADS):
            v_ref[pl.ds(hd, tm, stride=N_HEADS), :] = v[:, hd * V_DIM:(hd + 1) * V_DIM]
    else:
        v_ref[...] = v

    row = lax.broadcasted_iota(jnp.int32, (CHUNK, CHUNK), 0)
    col = lax.broadcasted_iota(jnp.int32, (CHUNK, CHUNK), 1)
    keep = col <= row
    if seq_in_chunk < CHUNK:
        keep = keep & ((row // seq_in_chunk) == (col // seq_in_chunk))
    for grp in range(N_GROUPS):
        lanes = slice(grp * GROUP, (grp + 1) * GROUP)
        gg = g[:, lanes]
        gvn = gg * lax.rsqrt(jnp.mean(gg * gg, axis=-1, keepdims=True) + EPS) * gn_ref[:, lanes]
        if not prompt:
            gv_ref[:, lanes] = gvn
        wm = jnp.where(keep, ws_ref[grp], 0.0).astype(BF16)
        bcol = bs_ref[:, grp:grp + 1]
        for c in range(tm // CHUNK):
            rows = slice(c * CHUNK, (c + 1) * CHUNK)
            mixed = jnp.dot(wm, gvn[rows, :].astype(BF16), preferred_element_type=F32) + bcol
            s_ref[rows, lanes] = (u[rows, lanes] * mixed).astype(BF16)

    if prompt:
        for hd in range(N_HEADS):
            lanes = slice(hd * V_DIM, (hd + 1) * V_DIM)
            qt = q[:, lanes].T
            feat = lax.broadcasted_iota(jnp.int32, qt.shape, 0)
            qtm_ref[0, hd, 0] = jnp.where(feat < HEAD_DIM, qt, 0.0).astype(BF16)
            qtm_ref[0, hd, 1] = jnp.where(feat >= HEAD_DIM, qt, 0.0).astype(BF16)
            kb_ref[0, hd] = k[:, lanes].astype(BF16)
            vt_ref[0, hd, 0:V_DIM] = v[:, lanes].T.astype(BF16)
            ones_row = lax.broadcasted_iota(jnp.int32, (VT_ROWS - V_DIM, tm), 0) == 0
            vt_ref[0, hd, V_DIM:VT_ROWS] = ones_row.astype(F32).astype(BF16)
    else:
        n_tok = seq_in_chunk
        r = lax.broadcasted_iota(jnp.int32, (N_MAPS * n_tok, ATTN_WIDTH), 0)
        c = lax.broadcasted_iota(jnp.int32, (N_MAPS * n_tok, ATTN_WIDTH), 1)
        own = (r // n_tok) == (c // HEAD_DIM)
        n_q = N_MAPS * n_tok
        zpad = jnp.zeros((NEW_ROWS - n_tok, ATTN_WIDTH), F32)
        for sq in range(tm // n_tok):
            tok = slice(sq * n_tok, (sq + 1) * n_tok)
            q_seq = jnp.concatenate([q[tok]] * N_MAPS, axis=0)
            q_ref[sq, 0:n_q] = jnp.where(own, q_seq, 0.0).astype(BF16)
            q_ref[sq, n_q:n_q + NEW_ROWS] = jnp.concatenate([k[tok], zpad], axis=0).astype(BF16)
            q_ref[sq, n_q + NEW_ROWS:n_q + 2 * NEW_ROWS] = jnp.concatenate([v[tok], zpad], axis=0).astype(BF16)


def _ffn_in_all_kernel(xp_ref, xs_ref, nf_ref, wg_ref, wu_ref, wd_ref, nm_ref, win_ref, gn_ref,
                       wsp_ref, bsp_ref, wss_ref, bss_ref, *rest, n_prompt_blocks, seq_tokens):
    outs_p, outs_s, (hn_ref, acc_ref) = rest[:7], rest[7:13], rest[13:]
    shared = (nf_ref, wg_ref, wu_ref, wd_ref, nm_ref, win_ref, gn_ref)
    i = pl.program_id(0)

    @pl.when(i < n_prompt_blocks)
    def _():
        _ffn_in_kernel(xp_ref, *shared, wsp_ref, bsp_ref, *outs_p, hn_ref, acc_ref,
                       prompt=True, seq_in_chunk=CHUNK)

    @pl.when(i == n_prompt_blocks)
    def _():
        ts = xs_ref.shape[0]
        _ffn_in_kernel(xs_ref, *shared, wss_ref, bss_ref, *outs_s, hn_ref.at[0:ts], acc_ref.at[0:ts],
                       prompt=False, seq_in_chunk=seq_tokens)


def _ffn_in(xp, xs, nf, wg, wu, wd, nm, win, gn, wsp, bsp, wss, bss, *, batch, seq_tokens):
    n, ns = xp.shape[0], xs.shape[0]
    tm = TOKEN_BLOCK
    nb = n // tm
    per_batch = nb // batch
    t = n // batch
    assert ns <= tm and ns % seq_tokens == 0
    blk = lambda i: jnp.minimum(i, nb - 1)
    tok = lambda w: pl.BlockSpec((tm, w), lambda i: (blk(i), 0))
    whole = lambda *shape: _const_spec(shape)
    assert seq_tokens <= NEW_ROWS
    rows = N_MAPS * seq_tokens + 2 * NEW_ROWS
    return pl.pallas_call(
        functools.partial(_ffn_in_all_kernel, n_prompt_blocks=nb, seq_tokens=seq_tokens),
        grid=(nb + 1,),
        in_specs=[
            tok(D_MODEL), whole(ns, D_MODEL),
            _const_spec((1, D_MODEL)),
            *_gate_up_specs(), _const_spec(wd.shape),
            _const_spec((1, D_MODEL)),
            _const_spec(win.shape),
            _const_spec((1, GMLP_WIDTH)),
            _const_spec(wsp.shape), _const_spec(bsp.shape), _const_spec(wss.shape), _const_spec(bss.shape),
        ],
        out_specs=[
            tok(D_MODEL),
            pl.BlockSpec((1, ATTN_WIDTH, tm), lambda i: (blk(i) // per_batch, 0, blk(i) % per_batch)),
            pl.BlockSpec((tm * N_HEADS, V_DIM), lambda i: (blk(i), 0)),
            tok(GMLP_WIDTH),
            pl.BlockSpec((1, N_HEADS, 2, V_DIM, tm),
                         lambda i: (blk(i) // per_batch, 0, 0, 0, blk(i) % per_batch)),
            pl.BlockSpec((1, N_HEADS, tm, V_DIM), lambda i: (blk(i) // per_batch, 0, blk(i) % per_batch, 0)),
            pl.BlockSpec((1, N_HEADS, VT_ROWS, tm), lambda i: (blk(i) // per_batch, 0, 0, blk(i) % per_batch)),
            whole(ns, D_MODEL), whole(ns, ATTN_WIDTH), whole(ns, ATTN_WIDTH), whole(ns, GMLP_WIDTH),
            whole(ns // seq_tokens, rows, ATTN_WIDTH), whole(ns, GMLP_WIDTH),
        ],
        out_shape=[
            jax.ShapeDtypeStruct((n, D_MODEL), F32),
            jax.ShapeDtypeStruct((batch, ATTN_WIDTH, t), F32),
            jax.ShapeDtypeStruct((n * N_HEADS, V_DIM), F32),
            jax.ShapeDtypeStruct((n, GMLP_WIDTH), BF16),
            jax.ShapeDtypeStruct((batch, N_HEADS, 2, V_DIM, t), BF16),
            jax.ShapeDtypeStruct((batch, N_HEADS, t, V_DIM), BF16),
            jax.ShapeDtypeStruct((batch, N_HEADS, VT_ROWS, t), BF16),
            jax.ShapeDtypeStruct((ns, D_MODEL), F32),
            jax.ShapeDtypeStruct((ns, ATTN_WIDTH), F32),
            jax.ShapeDtypeStruct((ns, ATTN_WIDTH), F32),
            jax.ShapeDtypeStruct((ns, GMLP_WIDTH), BF16),
            jax.ShapeDtypeStruct((ns // seq_tokens, rows, ATTN_WIDTH), BF16),
            jax.ShapeDtypeStruct((ns, GMLP_WIDTH), F32),
        ],
        scratch_shapes=[pltpu.VMEM((tm, D_MODEL), BF16), pltpu.VMEM((tm, D_MODEL), F32)],
        compiler_params=pltpu.CompilerParams(dimension_semantics=("arbitrary",),
                                             vmem_limit_bytes=VMEM_LIMIT),
        name="ffn_in",
    )(xp, xs, nf, wg, wu, wd, nm, win, gn, wsp, bsp, wss, bss)


def _out_ffn_block(x1_ref, o_ref, s_ref, wout_ref, nf_ref, wg_ref, wu_ref, wd_ref, nfin_ref,
                   y_ref, hn_ref, acc_ref, *, head_major):
    if head_major:
        o = jnp.concatenate([o_ref[0, hd] for hd in range(N_HEADS)], axis=1)
    else:
        o = o_ref[...]
    x2 = (x1_ref[...]
          + jnp.dot(o, wout_ref[0:ATTN_WIDTH, :], preferred_element_type=F32)
          + jnp.dot(s_ref[...], wout_ref[ATTN_WIDTH:D_MODEL, :], preferred_element_type=F32))
    x3 = _swiglu_half_step(x2, nf_ref, wg_ref, wu_ref, wd_ref, hn_ref, acc_ref)
    y_ref[...] = _rmsnorm(x3, nfin_ref[...])


def _out_ffn_kernel(x1p_ref, op_ref, sp_ref, x1s_ref, os_ref, ss_ref, wout_ref, nf_ref, wg_ref, wu_ref,
                    wd_ref, nfin_ref, yp_ref, ys_ref, hn_ref, acc_ref, *, n_prompt_blocks):
    weights = (wout_ref, nf_ref, wg_ref, wu_ref, wd_ref, nfin_ref)
    i = pl.program_id(0)

    @pl.when(i < n_prompt_blocks)
    def _():
        _out_ffn_block(x1p_ref, op_ref, sp_ref, *weights, yp_ref, hn_ref, acc_ref, head_major=True)

    @pl.when(i == n_prompt_blocks)
    def _():
        ts = x1s_ref.shape[0]
        _out_ffn_block(x1s_ref, os_ref, ss_ref, *weights, ys_ref, hn_ref.at[0:ts], acc_ref.at[0:ts],
                       head_major=False)


def _out_ffn(x1p, op, sp, x1s, os_, ss, wout, nf, wg, wu, wd, nfin, *, batch):
    n, ns = x1p.shape[0], x1s.shape[0]
    tm = TOKEN_BLOCK
    nb = n // tm
    per_batch = nb // batch
    assert ns <= tm
    blk = lambda i: jnp.minimum(i, nb - 1)
    tok = lambda w: pl.BlockSpec((tm, w), lambda i: (blk(i), 0))
    whole = lambda *shape: _const_spec(shape)
    return pl.pallas_call(
        functools.partial(_out_ffn_kernel, n_prompt_blocks=nb),
        grid=(nb + 1,),
        in_specs=[
            tok(D_MODEL),
            pl.BlockSpec((1, N_HEADS, tm, V_DIM), lambda i: (blk(i) // per_batch, 0, blk(i) % per_batch, 0)),
            tok(GMLP_WIDTH),
            whole(ns, D_MODEL), whole(ns, ATTN_WIDTH), whole(ns, GMLP_WIDTH),
            _const_spec(wout.shape),
            _const_spec((1, D_MODEL)),
            *_gate_up_specs(), _const_spec(wd.shape),
            _const_spec((1, D_MODEL)),
        ],
        out_specs=[tok(D_MODEL), whole(ns, D_MODEL)],
        out_shape=[jax.ShapeDtypeStruct((n, D_MODEL), F32), jax.ShapeDtypeStruct((ns, D_MODEL), F32)],
        scratch_shapes=[pltpu.VMEM((tm, D_MODEL), BF16), pltpu.VMEM((tm, D_MODEL), F32)],
        compiler_params=pltpu.CompilerParams(dimension_semantics=("arbitrary",),
                                             vmem_limit_bytes=VMEM_LIMIT),
        name="out_ffn",
    )(x1p, op, sp, x1s, os_, ss, wout, nf, wg, wu, wd, nfin)


def _attn_kernel(qi_ref, ki_ref, pt_ref, qtm_ref, kb_ref, vt_ref, e_ref, b_ref, small_ref, seq_ref,
                 kt_hbm, v_hbm, o_ref, os_ref, m_ref, acc_ref, ms_ref, ls_ref, accs_ref,
                 kbuf, vbuf, page_sem, *, n_sample_steps, groups_per_seq, n_tok):
    step = pl.program_id(1)
    qi = qi_ref[step]
    ki = ki_ref[step]
    g = pl.program_id(0) * pl.num_programs(1) + step
    slot = g % 2

    def group_copies(group, buf_slot):
        seq = group // groups_per_seq
        first_page = (group % groups_per_seq) * PAGES_PER_STEP
        copies = []
        for i in range(PAGES_PER_STEP):
            page = pt_ref[seq, first_page + i]
            copies.append(pltpu.make_async_copy(kt_hbm.at[page], kbuf.at[buf_slot, i], page_sem.at[buf_slot]))
            copies.append(pltpu.make_async_copy(v_hbm.at[page], vbuf.at[buf_slot, i], page_sem.at[buf_slot]))
        return copies

    @pl.when(g == 0)
    def _():
        for cp in group_copies(0, 0):
            cp.start()

    @pl.when(g + 1 < n_sample_steps)
    def _():
        for cp in group_copies(g + 1, 1 - slot):
            cp.start()

    @pl.when(g < n_sample_steps)
    def _():
        for cp in group_copies(g, slot):
            cp.wait()

    def sample_step():
        n_q = N_MAPS * n_tok
        head_rows = 2 * n_tok
        valid = g < n_sample_steps
        j = jnp.minimum(g, n_sample_steps - 1) % groups_per_seq
        first = j == 0
        last = j == groups_per_seq - 1

        qbd = seq_ref[0, 0:n_q]
        blocks = []
        for i in range(0, PAGES_PER_STEP, 2):
            keys = jnp.concatenate([kbuf[slot, i].astype(BF16), kbuf[slot, i + 1].astype(BF16)], axis=1)
            blocks.append(jnp.dot(qbd, keys, preferred_element_type=F32))
            yield
        bl = jnp.where(last, b_ref[0], 0.0)
        blocks[-1] = blocks[-1] + jnp.concatenate([jnp.zeros_like(bl), bl], axis=1)
        pad = jnp.zeros((CHUNK - NEW_ROWS, ATTN_WIDTH), BF16)
        k_new = jnp.concatenate([seq_ref[0, n_q:n_q + NEW_ROWS], pad], axis=0)
        v_new = jnp.concatenate([seq_ref[0, n_q + NEW_ROWS:n_q + 2 * NEW_ROWS], pad], axis=0)
        s_new = lax.dot_general(qbd, k_new, (((1,), (1,)), ((), ())), preferred_element_type=F32)
        blocks.append(s_new + jnp.where(last, b_ref[1], NEG))
        s = jnp.concatenate(blocks, axis=1) + jnp.where(valid, 0.0, NEG)

        m_prev = jnp.where(first, NEG, ms_ref[...])
        l_prev = jnp.where(first, 0.0, ls_ref[...])
        acc_prev = jnp.where(first, 0.0, accs_ref[...])
        m_new = jnp.maximum(m_prev, jnp.max(s, axis=1, keepdims=True))
        yield
        alpha = jnp.exp2(m_prev - m_new)
        p = jnp.exp2(s - m_new)
        l_new = alpha * l_prev + jnp.sum(p, axis=1, keepdims=True)
        yield
        pv = None
        pb = p.astype(BF16)
        for i in range(PAGES_PER_STEP + 1):
            if i < PAGES_PER_STEP:
                vals = jnp.concatenate([vbuf[slot, i, pl.ds(hd, CHUNK, stride=N_HEADS), :]
                                        for hd in range(N_HEADS)], axis=1)
            else:
                vals = v_new
            d = jnp.dot(pb[:, i * CHUNK:(i + 1) * CHUNK], vals.astype(BF16),
                        preferred_element_type=F32)
            pv = d if pv is None else pv + d
            if i % 3 == 2:
                yield
        acc_new = alpha * acc_prev + pv
        ms_ref[...] = m_new
        ls_ref[...] = l_new
        accs_ref[...] = acc_new

        lam = _lam(small_ref)
        o_all = acc_new * (1.0 / l_new)
        for hd in range(N_HEADS):
            r1 = slice(hd * head_rows, hd * head_rows + n_tok)
            r2 = slice(hd * head_rows + n_tok, (hd + 1) * head_rows)
            lanes = slice(hd * V_DIM, (hd + 1) * V_DIM)
            o = o_all[r1, lanes] - lam * o_all[r2, lanes]
            os_ref[0, :, lanes] = (
                _rmsnorm(o, small_ref[0:1, :]) * (1.0 - LAM_INIT)).astype(BF16)

    @pl.when(ki == 0)
    def _():
        m_ref[...] = jnp.full_like(m_ref, NEG)
        acc_ref[...] = jnp.zeros_like(acc_ref)


    def unit_logits(unit):
        hd, mp, q_lo, q_n, key_blocks = unit
        idx = 2 * hd + mp
        qm = qtm_ref[0, hd, mp, :, q_lo:q_lo + q_n]
        logits = []
        for k_lo, k_n, bias_kind in key_blocks:
            s = jnp.dot(kb_ref[0, hd, k_lo:k_lo + k_n, :], qm, preferred_element_type=F32)
            if bias_kind is not None:
                s = s + e_ref[bias_kind, idx]
            logits.append(s)
        return logits

    def unit_update(unit, logits):
        hd, mp, q_lo, q_n, key_blocks = unit
        idx = 2 * hd + mp
        cols = slice(q_lo, q_lo + q_n)
        m_prev = m_ref[idx, :, cols]
        m_new = m_prev
        for s in logits:
            m_new = jnp.maximum(m_new, jnp.max(s, axis=0, keepdims=True))
        alpha = jnp.exp2(m_prev - m_new)
        pv = None
        for (k_lo, k_n, _), s in zip(key_blocks, logits):
            p = jnp.exp2(s - m_new)
            d = jnp.dot(vt_ref[0, hd, :, k_lo:k_lo + k_n], p.astype(BF16), preferred_element_type=F32)
            pv = d if pv is None else pv + d
        acc_ref[idx, :, cols] = alpha * acc_ref[idx, :, cols] + pv
        m_ref[idx, :, cols] = m_new

    def run(units):
        side = sample_step()
        pending = []
        for n, unit in enumerate(units):
            pending.append((unit, unit_logits(unit)))
            if n % 2 == 0:
                next(side, None)
            if len(pending) > PIPE_DEPTH:
                unit_update(*pending.pop(0))
        for item in pending:
            unit_update(*item)
            next(side, None)
        for _ in side:
            pass

    maps = [(hd, mp) for hd in range(N_HEADS) for mp in range(2)]

    @pl.when(ki < qi - 1)
    def _():
        run([(hd, mp, q_lo, SUB, [(0, TK, None)]) for hd, mp in maps for q_lo in range(0, TQ, SUB)])

    @pl.when(ki == qi - 1)
    def _():
        run([u for hd, mp in maps for u in (
            (hd, mp, 0, SUB, [(0, SUB, None), (SUB, SUB, 1)]),
            (hd, mp, SUB, SUB, [(0, TK, None)]))])

    @pl.when(ki == qi)
    def _():
        run([u for hd, mp in maps for u in (
            (hd, mp, 0, SUB, [(0, SUB, 0)]),
            (hd, mp, SUB, SUB, [(0, SUB, 1), (SUB, SUB, 0)]))])
        lam = _lam(small_ref)
        for hd in range(N_HEADS):
            o1 = acc_ref[2 * hd, 0:V_DIM] * (1.0 / acc_ref[2 * hd, V_DIM:V_DIM + 1])
            o2 = acc_ref[2 * hd + 1, 0:V_DIM] * (1.0 / acc_ref[2 * hd + 1, V_DIM:V_DIM + 1])
            o = o1 - lam * o2
            inv = lax.rsqrt(jnp.mean(o * o, axis=0, keepdims=True) + EPS)
            on = (o * inv).T * small_ref[0:1, :] * (1.0 - LAM_INIT)
            o_ref[0, hd] = on.astype(BF16)


def _attention(qtm, kb, vt, e_tiles, b_tiles, small, page_table, seq_pack, cache_kt, cache_v):
    batch, _, _, _, t = qtm.shape
    nq = t // TQ
    pairs = [(q, k) for q in range(nq) for k in range(q + 1)]
    n_steps = len(pairs)
    qi_arr = jnp.asarray([p[0] for p in pairs], jnp.int32)
    ki_arr = jnp.asarray([p[1] for p in pairs], jnp.int32)
    n_seq = seq_pack.shape[0]
    n_tok = (seq_pack.shape[1] - 2 * NEW_ROWS) // N_MAPS
    groups_per_seq = page_table.shape[1] // PAGES_PER_STEP
    n_sample_steps = n_seq * groups_per_seq
    assert page_table.shape[1] % PAGES_PER_STEP == 0 and n_sample_steps <= batch * n_steps
    n_rows = N_MAPS * n_tok

    def sample_pos(b, s):
        g = jnp.minimum(b * n_steps + s, n_sample_steps - 1)
        return g // groups_per_seq, g % groups_per_seq

    def seq_spec(rows):
        return pl.BlockSpec((1, rows, ATTN_WIDTH), lambda b, s, qi, ki, pt: (sample_pos(b, s)[0], 0, 0))

    def full(shape, **kw):
        return pl.BlockSpec(shape, lambda b, s, qi, ki, pt: (0,) * len(shape), **kw)

    const = lambda a: full(a.shape, pipeline_mode=pl.Buffered(1))
    grid_spec = pltpu.PrefetchScalarGridSpec(
        num_scalar_prefetch=3,
        grid=(batch, n_steps),
        in_specs=([
            pl.BlockSpec((1, N_HEADS, 2, V_DIM, TQ), lambda b, s, qi, ki, pt: (b, 0, 0, 0, qi[s])),
            pl.BlockSpec((1, N_HEADS, TK, V_DIM), lambda b, s, qi, ki, pt: (b, 0, ki[s], 0)),
            pl.BlockSpec((1, N_HEADS, VT_ROWS, TK), lambda b, s, qi, ki, pt: (b, 0, 0, ki[s])),
            const(e_tiles), const(b_tiles), const(small), seq_spec(seq_pack.shape[1]),
            pl.BlockSpec(memory_space=pl.ANY), pl.BlockSpec(memory_space=pl.ANY)]),
        out_specs=[
            pl.BlockSpec((1, N_HEADS, TQ, V_DIM), lambda b, s, qi, ki, pt: (b, 0, qi[s], 0)),
            seq_spec(n_tok),
        ],
        scratch_shapes=[
            pltpu.VMEM((N_MAPS, 1, TQ), F32),
            pltpu.VMEM((N_MAPS, VT_ROWS, TQ), F32),
            pltpu.VMEM((n_rows, 1), F32),
            pltpu.VMEM((n_rows, 1), F32),
            pltpu.VMEM((n_rows, ATTN_WIDTH), F32),
            pltpu.VMEM((2, PAGES_PER_STEP, ATTN_WIDTH, CHUNK), F32),
            pltpu.VMEM((2, PAGES_PER_STEP, ATTN_WIDTH, CHUNK), F32),
            pltpu.SemaphoreType.DMA((2,)),
        ],
    )
    return pl.pallas_call(
        functools.partial(_attn_kernel, n_sample_steps=n_sample_steps, groups_per_seq=groups_per_seq,
                          n_tok=n_tok),
        grid_spec=grid_spec,
        out_shape=[
            jax.ShapeDtypeStruct((batch, N_HEADS, t, V_DIM), BF16),
            jax.ShapeDtypeStruct((n_seq, n_tok, ATTN_WIDTH), BF16),
        ],
        compiler_params=pltpu.CompilerParams(dimension_semantics=("arbitrary", "arbitrary"),
                                             vmem_limit_bytes=VMEM_LIMIT),
        name="attention",
    )(qi_arr, ki_arr, page_table, qtm, kb, vt, e_tiles, b_tiles, small, seq_pack, cache_kt, cache_v)


def _ffn_weights(w_gu, w_down):
    w_gu = w_gu.astype(BF16)
    return w_gu, w_gu, w_down.astype(BF16)


def kernel(x_prompt, x_sample, cache_k, cache_v, page_table, rel_table, norm_ffn1, w_ffn1_gu, w_ffn1_down, norm_mix, w_in, lambda_q1, lambda_k1, lambda_q2, lambda_k2, subln, gmlp_norm, w_spatial, b_spatial, w_out, norm_ffn2, w_ffn2_gu, w_ffn2_down, norm_final):
    depth = cache_k.shape[0]
    assert depth == 1
    batch, t_p, _ = x_prompt.shape
    n_seq, t_s, _ = x_sample.shape
    n_pool, page = cache_k.shape[1], cache_k.shape[2]
    assert page == CHUNK and CHUNK % t_s == 0 and (n_seq * t_s) % CHUNK == 0

    row = lambda a: a.reshape(1, -1)
    wg1, wu1, wd1 = _ffn_weights(w_ffn1_gu[0], w_ffn1_down[0])
    wg2, wu2, wd2 = _ffn_weights(w_ffn2_gu[0], w_ffn2_down[0])
    win = w_in[0].astype(BF16)
    wout = w_out[0].astype(BF16)
    lamv = jnp.stack([lambda_q1[0], lambda_k1[0], lambda_q2[0], lambda_k2[0]])
    small = jnp.concatenate([
        row(subln[0]), jnp.pad(lamv, ((0, 0), (0, V_DIM - HEAD_DIM))), jnp.zeros((3, V_DIM), F32)])
    reps = CHUNK // t_s
    ws_sample = jnp.tile(w_spatial[0][:, :t_s, :t_s], (1, reps, reps))
    bs_sample = jnp.tile(b_spatial[0][:, :t_s], (1, reps)).T
    ffn1 = (row(norm_ffn1[0]), wg1, wu1, wd1)
    mix = (row(norm_mix[0]), win, row(gmlp_norm[0]))

    e_tiles, b_tiles = _bias_tables(rel_table, t_s)

    x1p, kp, vp, sp, qtm, kb, vt, x1s, ks, vs, ss, qs, gvs = _ffn_in(
        x_prompt.reshape(batch * t_p, D_MODEL), x_sample.reshape(n_seq * t_s, D_MODEL), *ffn1, *mix,
        w_spatial[0], b_spatial[0].T, ws_sample, bs_sample, batch=batch, seq_tokens=t_s)

    op, os_ = _attention(
        qtm, kb, vt, e_tiles, b_tiles, small, page_table, qs,
        jnp.transpose(cache_k[0], (0, 2, 3, 1)).reshape(n_pool, ATTN_WIDTH, page),
        cache_v[0].reshape(n_pool, page * N_HEADS, V_DIM))

    ffn2 = (wout, row(norm_ffn2[0]), wg2, wu2, wd2, row(norm_final))
    yp, ys = _out_ffn(x1p, op, sp, x1s, os_.reshape(n_seq * t_s, ATTN_WIDTH), ss, *ffn2, batch=batch)

    return (
        yp.reshape(batch, t_p, D_MODEL),
        ys.reshape(n_seq, t_s, D_MODEL),
        jnp.transpose(kp.reshape(depth, batch, N_MAPS, HEAD_DIM, t_p), (0, 1, 4, 2, 3)),
        vp.reshape(depth, batch, t_p, N_HEADS, V_DIM),
        ks.reshape(depth, n_seq, t_s, N_MAPS, HEAD_DIM),
        vs.reshape(depth, n_seq, t_s, N_HEADS, V_DIM),
        gvs.reshape(depth, n_seq, t_s, GMLP_WIDTH),
    )
```

```python
import functools
import math

import jax
import jax.numpy as jnp
from jax import lax
from jax.experimental import pallas as pl
from jax.experimental.pallas import tpu as pltpu

F32 = jnp.float32
BF16 = jnp.bfloat16

D_MODEL = 1024
D_FF = 2816
FF_CHUNK = 256
N_FF_CHUNKS = D_FF // FF_CHUNK
ATTN_WIDTH = 512
HEAD_DIM = 64
N_HEADS = 4
N_MAPS = 2 * N_HEADS
V_DIM = 2 * HEAD_DIM
GMLP_WIDTH = 512
N_GROUPS = 4
GROUP = GMLP_WIDTH // N_GROUPS
CHUNK = 128
N_BUCKETS = 32
MAX_EXACT = N_BUCKETS // 2
MAX_DISTANCE = 128
EPS = 1e-6
NEG = -1e30
BUCKET_THRESHOLDS = tuple(
    math.ceil(MAX_EXACT * (MAX_DISTANCE / MAX_EXACT) ** (k / (N_BUCKETS - MAX_EXACT)))
    for k in range(1, N_BUCKETS - MAX_EXACT))
LAM_INIT = 0.8 - 0.6 * math.exp(-0.3 * 0)
LOG2E = math.log2(math.e)

TOKEN_BLOCK = 512
TQ = 1024
TK = 512
SUB = 256
PIPE_DEPTH = 6
VT_ROWS = V_DIM + 16
PAGES_PER_STEP = 16
VMEM_LIMIT = 52 * 1024 * 1024


def _const_spec(shape):
    nd = len(shape)
    return pl.BlockSpec(shape, lambda *_: (0,) * nd, pipeline_mode=pl.Buffered(1))


def _gate_up_specs():
    half = lambda j: pl.BlockSpec((D_MODEL, D_FF), lambda *_: (0, j), pipeline_mode=pl.Buffered(1))
    return [half(0), half(1)]


def _rmsnorm(x, g):
    return x * lax.rsqrt(jnp.mean(x * x, axis=-1, keepdims=True) + EPS) * g


def _lam(small_ref):
    a = jnp.sum(small_ref[1:2, :] * small_ref[2:3, :], axis=1, keepdims=True)
    b = jnp.sum(small_ref[3:4, :] * small_ref[4:5, :], axis=1, keepdims=True)
    return jnp.exp(a) - jnp.exp(b) + LAM_INIT


def _bias_of_distance(d, tab_ref, m):
    n = jnp.maximum(d, 0)
    large = MAX_EXACT
    for thr in BUCKET_THRESHOLDS:
        large = large + (n >= thr).astype(jnp.int32)
    bucket = jnp.where(n < MAX_EXACT, n, large)
    val = jnp.zeros(d.shape, F32)
    for b in range(N_BUCKETS):
        val = jnp.where(bucket == b, tab_ref[b, m], val)
    val = (val - tab_ref[N_BUCKETS - 1, m]) * LOG2E
    return jnp.where(d >= 0, val, NEG)


def _bias_kernel(tab_ref, e_ref, b_ref, *, n_tok):
    m = pl.program_id(0)
    key = lax.broadcasted_iota(jnp.int32, (SUB, SUB), 0)
    qry = lax.broadcasted_iota(jnp.int32, (SUB, SUB), 1)
    e_ref[0, 0] = _bias_of_distance(qry - key, tab_ref, m)
    e_ref[1, 0] = _bias_of_distance(SUB + qry - key, tab_ref, m)
    t = lax.broadcasted_iota(jnp.int32, (n_tok, CHUNK), 0)
    c = lax.broadcasted_iota(jnp.int32, (n_tok, CHUNK), 1)
    b_ref[0] = _bias_of_distance(CHUNK + t - c, tab_ref, m)
    b_ref[1] = _bias_of_distance(t - c, tab_ref, m)


def _bias_tables(rel_table, n_tok):
    return pl.pallas_call(
        functools.partial(_bias_kernel, n_tok=n_tok),
        grid=(N_MAPS,),
        in_specs=[pl.BlockSpec(memory_space=pltpu.SMEM)],
        out_specs=[
            pl.BlockSpec((2, 1, SUB, SUB), lambda m: (0, m, 0, 0)),
            pl.BlockSpec((2, n_tok, CHUNK), lambda m: (0, m, 0)),
        ],
        out_shape=[
            jax.ShapeDtypeStruct((2, N_MAPS, SUB, SUB), F32),
            jax.ShapeDtypeStruct((2, N_MAPS * n_tok, CHUNK), F32),
        ],
        name="bias_tables",
    )(rel_table)


def _swiglu_half_step(x, norm_ref, wg_ref, wu_ref, wd_ref, hn_ref, acc_ref):
    hn_ref[...] = _rmsnorm(x, norm_ref[...]).astype(BF16)

    def gate_up(c):
        hn = hn_ref[...]
        cols = slice(c * FF_CHUNK, (c + 1) * FF_CHUNK)
        return (jnp.dot(hn, wg_ref[:, cols], preferred_element_type=F32),
                jnp.dot(hn, wu_ref[:, cols], preferred_element_type=F32))

    def down(c, gate, up):
        act = (jax.nn.silu(gate) * up).astype(BF16)
        d = jnp.dot(act, wd_ref[c * FF_CHUNK:(c + 1) * FF_CHUNK, :], preferred_element_type=F32)
        if c == 0:
            acc_ref[...] = d
        else:
            acc_ref[...] += d

    pending = gate_up(0)
    for c in range(N_FF_CHUNKS):
        nxt = gate_up(c + 1) if c + 1 < N_FF_CHUNKS else None
        down(c, *pending)
        pending = nxt
    return x + 0.5 * acc_ref[...]


def _ffn_in_kernel(x_ref, nf_ref, wg_ref, wu_ref, wd_ref, nm_ref, win_ref, gn_ref, ws_ref, bs_ref,
                   *rest, prompt, seq_in_chunk):
    if prompt:
        x1_ref, k_ref, v_ref, s_ref, qtm_ref, kb_ref, vt_ref, hn_ref, acc_ref = rest
    else:
        x1_ref, k_ref, v_ref, s_ref, q_ref, gv_ref, hn_ref, acc_ref = rest
    tm = x_ref.shape[0]

    x1 = _swiglu_half_step(x_ref[...], nf_ref, wg_ref, wu_ref, wd_ref, hn_ref, acc_ref)
    x1_ref[...] = x1
    h = _rmsnorm(x1, nm_ref[...]).astype(BF16)

    def proj(i):
        return jnp.dot(h, win_ref[:, i * ATTN_WIDTH:(i + 1) * ATTN_WIDTH], preferred_element_type=F32)

    q = proj(0) * (HEAD_DIM ** -0.5 * LOG2E)
    k = proj(1)
    v = proj(2)
    u = jax.nn.gelu(proj(3))
    g = jax.nn.gelu(proj(4))
    k_ref[...] = k.T[None] if prompt else k
    if prompt:
        for hd in range(N_HEADS):
            v_ref[pl.ds(hd, tm, stride=N_HEADS), :] = v[:, hd * V_DIM:(hd + 1) * V_DIM]
    else:
        v_ref[...] = v

    row = lax.broadcasted_iota(jnp.int32, (CHUNK, CHUNK), 0)
    col = lax.broadcasted_iota(jnp.int32, (CHUNK, CHUNK), 1)
    keep = col <= row
    if seq_in_chunk < CHUNK:
        keep = keep & ((row // seq_in_chunk) == (col // seq_in_chunk))
    for grp in range(N_GROUPS):
        lanes = slice(grp * GROUP, (grp + 1) * GROUP)
        gg = g[:, lanes]
        gvn = gg * lax.rsqrt(jnp.mean(gg * gg, axis=-1, keepdims=True) + EPS) * gn_ref[:, lanes]
        if not prompt:
            gv_ref[:, lanes] = gvn
        wm = jnp.where(keep, ws_ref[grp], 0.0).astype(BF16)
        bcol = bs_ref[:, grp:grp + 1]
        for c in range(tm // CHUNK):
            rows = slice(c * CHUNK, (c + 1) * CHUNK)
            mixed = jnp.dot(wm, gvn[rows, :].astype(BF16), preferred_element_type=F32) + bcol
            s_ref[rows, lanes] = (u[rows, lanes] * mixed).astype(BF16)

    if prompt:
        for hd in range(N_HEADS):
            lanes = slice(hd * V_DIM, (hd + 1) * V_DIM)
            qt = q[:, lanes].T
            feat = lax.broadcasted_iota(jnp.int32, qt.shape, 0)
            qtm_ref[0, hd, 0] = jnp.where(feat < HEAD_DIM, qt, 0.0).astype(BF16)
            qtm_ref[0, hd, 1] = jnp.where(feat >= HEAD_DIM, qt, 0.0).astype(BF16)
            kb_ref[0, hd] = k[:, lanes].astype(BF16)
            vt_ref[0, hd, 0:V_DIM] = v[:, lanes].T.astype(BF16)
            ones_row = lax.broadcasted_iota(jnp.int32, (VT_ROWS - V_DIM, tm), 0) == 0
            vt_ref[0, hd, V_DIM:VT_ROWS] = ones_row.astype(F32).astype(BF16)
    else:
        n_tok = seq_in_chunk
        r = lax.broadcasted_iota(jnp.int32, (N_MAPS * n_tok, ATTN_WIDTH), 0)
        c = lax.broadcasted_iota(jnp.int32, (N_MAPS * n_tok, ATTN_WIDTH), 1)
        own = (r // n_tok) == (c // HEAD_DIM)
        n_q = N_MAPS * n_tok
        for sq in range(tm // n_tok):
            tok = slice(sq * n_tok, (sq + 1) * n_tok)
            q_seq = jnp.concatenate([q[tok]] * N_MAPS, axis=0)
            q_ref[sq, 0:n_q] = jnp.where(own, q_seq, 0.0)
            q_ref[sq, n_q:n_q + n_tok] = k[tok]
            q_ref[sq, n_q + n_tok:n_q + 2 * n_tok] = v[tok]


def _ffn_in(x, nf, wg, wu, wd, nm, win, gn, ws, bs, *, batch, seq_in_chunk, prompt):
    n = x.shape[0]
    tm = min(TOKEN_BLOCK, n)
    nb = n // tm
    per_batch = nb // batch
    tok = lambda w: pl.BlockSpec((tm, w), lambda i: (i, 0))
    in_specs = [
        tok(D_MODEL),
        _const_spec((1, D_MODEL)),
        *_gate_up_specs(), _const_spec(wd.shape),
        _const_spec((1, D_MODEL)),
        _const_spec(win.shape),
        _const_spec((1, GMLP_WIDTH)),
        _const_spec(ws.shape), _const_spec(bs.shape),
    ]
    out_specs = [tok(D_MODEL), tok(ATTN_WIDTH), tok(ATTN_WIDTH), tok(GMLP_WIDTH)]
    out_shape = [
        jax.ShapeDtypeStruct((n, D_MODEL), F32),
        jax.ShapeDtypeStruct((n, ATTN_WIDTH), F32),
        jax.ShapeDtypeStruct((n, ATTN_WIDTH), F32),
        jax.ShapeDtypeStruct((n, GMLP_WIDTH), BF16),
    ]
    if prompt:
        t = n // batch
        out_specs[1] = pl.BlockSpec((1, ATTN_WIDTH, tm), lambda i: (i // per_batch, 0, i % per_batch))
        out_shape[1] = jax.ShapeDtypeStruct((batch, ATTN_WIDTH, t), F32)
        out_specs[2] = pl.BlockSpec((tm * N_HEADS, V_DIM), lambda i: (i, 0))
        out_shape[2] = jax.ShapeDtypeStruct((n * N_HEADS, V_DIM), F32)
        out_specs += [
            pl.BlockSpec((1, N_HEADS, 2, V_DIM, tm), lambda i: (i // per_batch, 0, 0, 0, i % per_batch)),
            pl.BlockSpec((1, N_HEADS, tm, V_DIM), lambda i: (i // per_batch, 0, i % per_batch, 0)),
            pl.BlockSpec((1, N_HEADS, VT_ROWS, tm), lambda i: (i // per_batch, 0, 0, i % per_batch)),
        ]
        out_shape += [
            jax.ShapeDtypeStruct((batch, N_HEADS, 2, V_DIM, t), BF16),
            jax.ShapeDtypeStruct((batch, N_HEADS, t, V_DIM), BF16),
            jax.ShapeDtypeStruct((batch, N_HEADS, VT_ROWS, t), BF16),
        ]
    else:
        rows = (N_MAPS + 2) * seq_in_chunk
        out_specs += [pl.BlockSpec((tm // seq_in_chunk, rows, ATTN_WIDTH), lambda i: (i, 0, 0)),
                      tok(GMLP_WIDTH)]
        out_shape += [
            jax.ShapeDtypeStruct((n // seq_in_chunk, rows, ATTN_WIDTH), F32),
            jax.ShapeDtypeStruct((n, GMLP_WIDTH), F32),
        ]
    return pl.pallas_call(
        functools.partial(_ffn_in_kernel, prompt=prompt, seq_in_chunk=seq_in_chunk),
        grid=(nb,),
        in_specs=in_specs,
        out_specs=out_specs,
        out_shape=out_shape,
        scratch_shapes=[pltpu.VMEM((tm, D_MODEL), BF16), pltpu.VMEM((tm, D_MODEL), F32)],
        compiler_params=pltpu.CompilerParams(dimension_semantics=("arbitrary",),
                                             vmem_limit_bytes=VMEM_LIMIT),
        name="ffn_in_prompt" if prompt else "ffn_in_sample",
    )(x, nf, wg, wu, wd, nm, win, gn, ws, bs)


def _out_ffn_kernel(x1_ref, o_ref, s_ref, wout_ref, nf_ref, wg_ref, wu_ref, wd_ref, nfin_ref,
                    y_ref, hn_ref, acc_ref, *, head_major):
    if head_major:
        o = jnp.concatenate([o_ref[0, hd] for hd in range(N_HEADS)], axis=1)
    else:
        o = o_ref[...]
    x2 = (x1_ref[...]
          + jnp.dot(o, wout_ref[0:ATTN_WIDTH, :], preferred_element_type=F32)
          + jnp.dot(s_ref[...], wout_ref[ATTN_WIDTH:D_MODEL, :], preferred_element_type=F32))
    x3 = _swiglu_half_step(x2, nf_ref, wg_ref, wu_ref, wd_ref, hn_ref, acc_ref)
    y_ref[...] = _rmsnorm(x3, nfin_ref[...])


def _out_ffn(x1, o, s, wout, nf, wg, wu, wd, nfin, *, batch, head_major):
    n = x1.shape[0]
    tm = min(TOKEN_BLOCK, n)
    nb = n // tm
    per_batch = nb // batch
    tok = lambda w: pl.BlockSpec((tm, w), lambda i: (i, 0))
    if head_major:
        o_spec = pl.BlockSpec((1, N_HEADS, tm, V_DIM), lambda i: (i // per_batch, 0, i % per_batch, 0))
    else:
        o_spec = tok(ATTN_WIDTH)
    return pl.pallas_call(
        functools.partial(_out_ffn_kernel, head_major=head_major),
        grid=(nb,),
        in_specs=[
            tok(D_MODEL), o_spec, tok(GMLP_WIDTH),
            _const_spec(wout.shape),
            _const_spec((1, D_MODEL)),
            *_gate_up_specs(), _const_spec(wd.shape),
            _const_spec((1, D_MODEL)),
        ],
        out_specs=tok(D_MODEL),
        out_shape=jax.ShapeDtypeStruct((n, D_MODEL), F32),
        scratch_shapes=[pltpu.VMEM((tm, D_MODEL), BF16), pltpu.VMEM((tm, D_MODEL), F32)],
        compiler_params=pltpu.CompilerParams(dimension_semantics=("arbitrary",),
                                             vmem_limit_bytes=VMEM_LIMIT),
        name="out_ffn_prompt" if head_major else "out_ffn_sample",
    )(x1, o, s, wout, nf, wg, wu, wd, nfin)


def _attn_kernel(qi_ref, ki_ref, pt_ref, qtm_ref, kb_ref, vt_ref, e_ref, b_ref, small_ref, seq_ref,
                 kt_hbm, v_hbm, o_ref, os_ref, m_ref, acc_ref, ms_ref, ls_ref, accs_ref,
                 kbuf, vbuf, page_sem, *, n_sample_steps, groups_per_seq):
    step = pl.program_id(1)
    qi = qi_ref[step]
    ki = ki_ref[step]
    g = pl.program_id(0) * pl.num_programs(1) + step
    slot = g % 2

    def group_copies(group, buf_slot):
        seq = group // groups_per_seq
        first_page = (group % groups_per_seq) * PAGES_PER_STEP
        copies = []
        for i in range(PAGES_PER_STEP):
            page = pt_ref[seq, first_page + i]
            copies.append(pltpu.make_async_copy(kt_hbm.at[page], kbuf.at[buf_slot, i], page_sem.at[buf_slot]))
            copies.append(pltpu.make_async_copy(v_hbm.at[page], vbuf.at[buf_slot, i], page_sem.at[buf_slot]))
        return copies

    @pl.when(g == 0)
    def _():
        for cp in group_copies(0, 0):
            cp.start()

    @pl.when(g + 1 < n_sample_steps)
    def _():
        for cp in group_copies(g + 1, 1 - slot):
            cp.start()

    @pl.when(g < n_sample_steps)
    def _():
        for cp in group_copies(g, slot):
            cp.wait()

    def sample_step():
        n_tok = seq_ref.shape[1] // (N_MAPS + 2)
        n_q = N_MAPS * n_tok
        head_rows = 2 * n_tok
        valid = g < n_sample_steps
        j = jnp.minimum(g, n_sample_steps - 1) % groups_per_seq
        first = j == 0
        last = j == groups_per_seq - 1

        qbd = seq_ref[0, 0:n_q].astype(BF16)
        blocks = []
        for i in range(0, PAGES_PER_STEP, 2):
            keys = jnp.concatenate([kbuf[slot, i].astype(BF16), kbuf[slot, i + 1].astype(BF16)], axis=1)
            blocks.append(jnp.dot(qbd, keys, preferred_element_type=F32))
            yield
        bl = jnp.where(last, b_ref[0], 0.0)
        blocks[-1] = blocks[-1] + jnp.concatenate([jnp.zeros_like(bl), bl], axis=1)
        pad = jnp.zeros((CHUNK - n_tok, ATTN_WIDTH), F32)
        k_new = jnp.concatenate([seq_ref[0, n_q:n_q + n_tok], pad], axis=0).astype(BF16)
        v_new = jnp.concatenate([seq_ref[0, n_q + n_tok:n_q + 2 * n_tok], pad], axis=0)
        s_new = lax.dot_general(qbd, k_new, (((1,), (1,)), ((), ())), preferred_element_type=F32)
        blocks.append(s_new + jnp.where(last, b_ref[1], NEG))
        s = jnp.concatenate(blocks, axis=1) + jnp.where(valid, 0.0, NEG)

        m_prev = jnp.where(first, NEG, ms_ref[...])
        l_prev = jnp.where(first, 0.0, ls_ref[...])
        acc_prev = jnp.where(first, 0.0, accs_ref[...])
        m_new = jnp.maximum(m_prev, jnp.max(s, axis=1, keepdims=True))
        yield
        alpha = jnp.exp2(m_prev - m_new)
        p = jnp.exp2(s - m_new)
        l_new = alpha * l_prev + jnp.sum(p, axis=1, keepdims=True)
        yield
        pv = None
        pb = p.astype(BF16)
        for i in range(PAGES_PER_STEP + 1):
            if i < PAGES_PER_STEP:
                vals = jnp.concatenate([vbuf[slot, i, pl.ds(hd, CHUNK, stride=N_HEADS), :]
                                        for hd in range(N_HEADS)], axis=1)
            else:
                vals = v_new
            d = jnp.dot(pb[:, i * CHUNK:(i + 1) * CHUNK], vals.astype(BF16),
                        preferred_element_type=F32)
            pv = d if pv is None else pv + d
            if i % 3 == 2:
                yield
        acc_new = alpha * acc_prev + pv
        ms_ref[...] = m_new
        ls_ref[...] = l_new
        accs_ref[...] = acc_new

        lam = _lam(small_ref)
        o_all = acc_new * (1.0 / l_new)
        for hd in range(N_HEADS):
            r1 = slice(hd * head_rows, hd * head_rows + n_tok)
            r2 = slice(hd * head_rows + n_tok, (hd + 1) * head_rows)
            lanes = slice(hd * V_DIM, (hd + 1) * V_DIM)
            o = o_all[r1, lanes] - lam * o_all[r2, lanes]
            os_ref[0, :, lanes] = (
                _rmsnorm(o, small_ref[0:1, :]) * (1.0 - LAM_INIT)).astype(BF16)

    @pl.when(ki == 0)
    def _():
        m_ref[...] = jnp.full_like(m_ref, NEG)
        acc_ref[...] = jnp.zeros_like(acc_ref)


    def unit_logits(unit):
        hd, mp, q_lo, q_n, key_blocks = unit
        idx = 2 * hd + mp
        qm = qtm_ref[0, hd, mp, :, q_lo:q_lo + q_n]
        logits = []
        for k_lo, k_n, bias_kind in key_blocks:
            s = jnp.dot(kb_ref[0, hd, k_lo:k_lo + k_n, :], qm, preferred_element_type=F32)
            if bias_kind is not None:
                s = s + e_ref[bias_kind, idx]
            logits.append(s)
        return logits

    def unit_update(unit, logits):
        hd, mp, q_lo, q_n, key_blocks = unit
        idx = 2 * hd + mp
        cols = slice(q_lo, q_lo + q_n)
        m_prev = m_ref[idx, :, cols]
        m_new = m_prev
        for s in logits:
            m_new = jnp.maximum(m_new, jnp.max(s, axis=0, keepdims=True))
        alpha = jnp.exp2(m_prev - m_new)
        pv = None
        for (k_lo, k_n, _), s in zip(key_blocks, logits):
            p = jnp.exp2(s - m_new)
            d = jnp.dot(vt_ref[0, hd, :, k_lo:k_lo + k_n], p.astype(BF16), preferred_element_type=F32)
            pv = d if pv is None else pv + d
        acc_ref[idx, :, cols] = alpha * acc_ref[idx, :, cols] + pv
        m_ref[idx, :, cols] = m_new

    def run(units):
        side = sample_step()
        pending = []
        for n, unit in enumerate(units):
            pending.append((unit, unit_logits(unit)))
            if n % 2 == 0:
                next(side, None)
            if len(pending) > PIPE_DEPTH:
                unit_update(*pending.pop(0))
        for item in pending:
            unit_update(*item)
            next(side, None)
        for _ in side:
            pass

    maps = [(hd, mp) for hd in range(N_HEADS) for mp in range(2)]

    far = lambda hd, mp, lo: [(hd, mp, lo, SUB, [(0, TK, None)]), (hd, mp, lo + SUB, SUB, [(0, TK, None)])]
    near = lambda hd, mp, lo: [(hd, mp, lo, SUB, [(0, SUB, None), (SUB, SUB, 1)]),
                               (hd, mp, lo + SUB, SUB, [(0, TK, None)])]
    diag = lambda hd, mp, lo: [(hd, mp, lo, SUB, [(0, SUB, 0)]),
                               (hd, mp, lo + SUB, SUB, [(0, SUB, 1), (SUB, SUB, 0)])]
    parts = TQ // TK
    first_key_tile = qi * parts

    def kinds_for(offset):
        kinds = []
        for r in range(parts):
            rel = r - offset
            kinds.append(far if rel >= 2 else near if rel == 1 else diag if rel == 0 else None)
        return kinds

    def units_for(kinds):
        return [u for hd, mp in maps for r, kind in enumerate(kinds) if kind is not None
                for u in kind(hd, mp, r * TK)]

    @pl.when(ki < first_key_tile - 1)
    def _():
        run(units_for([far] * parts))

    for offset in range(-1, parts):
        @pl.when(ki == first_key_tile + offset)
        def _(offset=offset):
            run(units_for(kinds_for(offset)))

    @pl.when(ki == first_key_tile + parts - 1)
    def _():
        lam = _lam(small_ref)
        for hd in range(N_HEADS):
            o1 = acc_ref[2 * hd, 0:V_DIM] * (1.0 / acc_ref[2 * hd, V_DIM:V_DIM + 1])
            o2 = acc_ref[2 * hd + 1, 0:V_DIM] * (1.0 / acc_ref[2 * hd + 1, V_DIM:V_DIM + 1])
            o = o1 - lam * o2
            inv = lax.rsqrt(jnp.mean(o * o, axis=0, keepdims=True) + EPS)
            on = (o * inv).T * small_ref[0:1, :] * (1.0 - LAM_INIT)
            o_ref[0, hd] = on.astype(BF16)


def _attention(qtm, kb, vt, e_tiles, b_tiles, small, page_table, seq_pack, cache_kt, cache_v):
    batch, _, _, _, t = qtm.shape
    nq = t // TQ
    pairs = [(q, k) for q in range(nq) for k in range((q + 1) * (TQ // TK))]
    n_steps = len(pairs)
    qi_arr = jnp.asarray([p[0] for p in pairs], jnp.int32)
    ki_arr = jnp.asarray([p[1] for p in pairs], jnp.int32)
    n_seq = seq_pack.shape[0]
    n_tok = seq_pack.shape[1] // (N_MAPS + 2)
    groups_per_seq = page_table.shape[1] // PAGES_PER_STEP
    n_sample_steps = n_seq * groups_per_seq
    assert page_table.shape[1] % PAGES_PER_STEP == 0 and n_sample_steps <= batch * n_steps
    n_rows = N_MAPS * n_tok

    def sample_pos(b, s):
        g = jnp.minimum(b * n_steps + s, n_sample_steps - 1)
        return g // groups_per_seq, g % groups_per_seq

    def seq_spec(rows):
        return pl.BlockSpec((1, rows, ATTN_WIDTH), lambda b, s, qi, ki, pt: (sample_pos(b, s)[0], 0, 0))

    def full(shape, **kw):
        return pl.BlockSpec(shape, lambda b, s, qi, ki, pt: (0,) * len(shape), **kw)

    const = lambda a: full(a.shape, pipeline_mode=pl.Buffered(1))
    grid_spec = pltpu.PrefetchScalarGridSpec(
        num_scalar_prefetch=3,
        grid=(batch, n_steps),
        in_specs=([
            pl.BlockSpec((1, N_HEADS, 2, V_DIM, TQ), lambda b, s, qi, ki, pt: (b, 0, 0, 0, qi[s])),
            pl.BlockSpec((1, N_HEADS, TK, V_DIM), lambda b, s, qi, ki, pt: (b, 0, ki[s], 0)),
            pl.BlockSpec((1, N_HEADS, VT_ROWS, TK), lambda b, s, qi, ki, pt: (b, 0, 0, ki[s])),
            const(e_tiles), const(b_tiles), const(small), seq_spec(seq_pack.shape[1]),
            pl.BlockSpec(memory_space=pl.ANY), pl.BlockSpec(memory_space=pl.ANY)]),
        out_specs=[
            pl.BlockSpec((1, N_HEADS, TQ, V_DIM), lambda b, s, qi, ki, pt: (b, 0, qi[s], 0)),
            seq_spec(n_tok),
        ],
        scratch_shapes=[
            pltpu.VMEM((N_MAPS, 1, TQ), F32),
            pltpu.VMEM((N_MAPS, VT_ROWS, TQ), F32),
            pltpu.VMEM((n_rows, 1), F32),
            pltpu.VMEM((n_rows, 1), F32),
            pltpu.VMEM((n_rows, ATTN_WIDTH), F32),
            pltpu.VMEM((2, PAGES_PER_STEP, ATTN_WIDTH, CHUNK), F32),
            pltpu.VMEM((2, PAGES_PER_STEP, ATTN_WIDTH, CHUNK), F32),
            pltpu.SemaphoreType.DMA((2,)),
        ],
    )
    return pl.pallas_call(
        functools.partial(_attn_kernel, n_sample_steps=n_sample_steps, groups_per_seq=groups_per_seq),
        grid_spec=grid_spec,
        out_shape=[
            jax.ShapeDtypeStruct((batch, N_HEADS, t, V_DIM), BF16),
            jax.ShapeDtypeStruct((n_seq, n_tok, ATTN_WIDTH), BF16),
        ],
        compiler_params=pltpu.CompilerParams(dimension_semantics=("arbitrary", "arbitrary"),
                                             vmem_limit_bytes=VMEM_LIMIT),
        name="attention",
    )(qi_arr, ki_arr, page_table, qtm, kb, vt, e_tiles, b_tiles, small, seq_pack, cache_kt, cache_v)


def _ffn_weights(w_gu, w_down):
    w_gu = w_gu.astype(BF16)
    return w_gu, w_gu, w_down.astype(BF16)


def kernel(x_prompt, x_sample, cache_k, cache_v, page_table, rel_table, norm_ffn1, w_ffn1_gu, w_ffn1_down, norm_mix, w_in, lambda_q1, lambda_k1, lambda_q2, lambda_k2, subln, gmlp_norm, w_spatial, b_spatial, w_out, norm_ffn2, w_ffn2_gu, w_ffn2_down, norm_final):
    depth = cache_k.shape[0]
    assert depth == 1
    batch, t_p, _ = x_prompt.shape
    n_seq, t_s, _ = x_sample.shape
    n_pool, page = cache_k.shape[1], cache_k.shape[2]
    assert page == CHUNK and CHUNK % t_s == 0 and (n_seq * t_s) % CHUNK == 0

    row = lambda a: a.reshape(1, -1)
    wg1, wu1, wd1 = _ffn_weights(w_ffn1_gu[0], w_ffn1_down[0])
    wg2, wu2, wd2 = _ffn_weights(w_ffn2_gu[0], w_ffn2_down[0])
    win = w_in[0].astype(BF16)
    wout = w_out[0].astype(BF16)
    lamv = jnp.stack([lambda_q1[0], lambda_k1[0], lambda_q2[0], lambda_k2[0]])
    small = jnp.concatenate([
        row(subln[0]), jnp.pad(lamv, ((0, 0), (0, V_DIM - HEAD_DIM))), jnp.zeros((3, V_DIM), F32)])
    reps = CHUNK // t_s
    ws_sample = jnp.tile(w_spatial[0][:, :t_s, :t_s], (1, reps, reps))
    bs_sample = jnp.tile(b_spatial[0][:, :t_s], (1, reps)).T
    ffn1 = (row(norm_ffn1[0]), wg1, wu1, wd1)
    mix = (row(norm_mix[0]), win, row(gmlp_norm[0]))

    e_tiles, b_tiles = _bias_tables(rel_table, t_s)

    x1p, kp, vp, sp, qtm, kb, vt = _ffn_in(
        x_prompt.reshape(batch * t_p, D_MODEL), *ffn1, *mix, w_spatial[0], b_spatial[0].T,
        batch=batch, seq_in_chunk=CHUNK, prompt=True)
    x1s, ks, vs, ss, qs, gvs = _ffn_in(
        x_sample.reshape(n_seq * t_s, D_MODEL), *ffn1, *mix, ws_sample, bs_sample,
        batch=1, seq_in_chunk=t_s, prompt=False)

    op, os_ = _attention(
        qtm, kb, vt, e_tiles, b_tiles, small, page_table, qs,
        jnp.transpose(cache_k[0], (0, 2, 3, 1)).reshape(n_pool, ATTN_WIDTH, page),
        cache_v[0].reshape(n_pool, page * N_HEADS, V_DIM))

    ffn2 = (wout, row(norm_ffn2[0]), wg2, wu2, wd2, row(norm_final))
    yp = _out_ffn(x1p, op, sp, *ffn2, batch=batch, head_major=True)
    ys = _out_ffn(x1s, os_.reshape(n_seq * t_s, ATTN_WIDTH), ss, *ffn2, batch=1, head_major=False)

    return (
        yp.reshape(batch, t_p, D_MODEL),
        ys.reshape(n_seq, t_s, D_MODEL),
        jnp.transpose(kp.reshape(depth, batch, N_MAPS, HEAD_DIM, t_p), (0, 1, 4, 2, 3)),
        vp.reshape(depth, batch, t_p, N_HEADS, V_DIM),
        ks.reshape(depth, n_seq, t_s, N_MAPS, HEAD_DIM),
        vs.reshape(depth, n_seq, t_s, N_HEADS, V_DIM),
        gvs.reshape(depth, n_seq, t_s, GMLP_WIDTH),
    )
```

```python
import functools
import math

import jax
import jax.numpy as jnp
from jax import lax
from jax.experimental import pallas as pl
from jax.experimental.pallas import tpu as pltpu

F32 = jnp.float32
BF16 = jnp.bfloat16

D_MODEL = 1024
D_FF = 2816
FF_CHUNK = 256
N_FF_CHUNKS = D_FF // FF_CHUNK
ATTN_WIDTH = 512
HEAD_DIM = 64
N_HEADS = 4
N_MAPS = 2 * N_HEADS
V_DIM = 2 * HEAD_DIM
GMLP_WIDTH = 512
N_GROUPS = 4
GROUP = GMLP_WIDTH // N_GROUPS
CHUNK = 128
N_BUCKETS = 32
MAX_EXACT = N_BUCKETS // 2
MAX_DISTANCE = 128
EPS = 1e-6
NEG = -1e30
BUCKET_THRESHOLDS = tuple(
    math.ceil(MAX_EXACT * (MAX_DISTANCE / MAX_EXACT) ** (k / (N_BUCKETS - MAX_EXACT)))
    for k in range(1, N_BUCKETS - MAX_EXACT))
LAM_INIT = 0.8 - 0.6 * math.exp(-0.3 * 0)
LOG2E = math.log2(math.e)

TOKEN_BLOCK = 512
TQ = 1024
TK = 512
SUB = 256
PIPE_DEPTH = 6
VT_ROWS = V_DIM + 16
PAGES_PER_STEP = 16
VMEM_LIMIT = 52 * 1024 * 1024


def _const_spec(shape):
    nd = len(shape)
    return pl.BlockSpec(shape, lambda *_: (0,) * nd, pipeline_mode=pl.Buffered(1))


def _gate_up_specs():
    half = lambda j: pl.BlockSpec((D_MODEL, D_FF), lambda *_: (0, j), pipeline_mode=pl.Buffered(1))
    return [half(0), half(1)]


def _rmsnorm(x, g):
    return x * lax.rsqrt(jnp.mean(x * x, axis=-1, keepdims=True) + EPS) * g


def _lam(small_ref):
    a = jnp.sum(small_ref[1:2, :] * small_ref[2:3, :], axis=1, keepdims=True)
    b = jnp.sum(small_ref[3:4, :] * small_ref[4:5, :], axis=1, keepdims=True)
    return jnp.exp(a) - jnp.exp(b) + LAM_INIT


def _bias_of_distance(d, tab_ref, m):
    n = jnp.maximum(d, 0)
    large = MAX_EXACT
    for thr in BUCKET_THRESHOLDS:
        large = large + (n >= thr).astype(jnp.int32)
    bucket = jnp.where(n < MAX_EXACT, n, large)
    val = jnp.zeros(d.shape, F32)
    for b in range(N_BUCKETS):
        val = jnp.where(bucket == b, tab_ref[b, m], val)
    val = (val - tab_ref[N_BUCKETS - 1, m]) * LOG2E
    return jnp.where(d >= 0, val, NEG)


def _bias_kernel(tab_ref, e_ref, b_ref, *, n_tok):
    m = pl.program_id(0)
    key = lax.broadcasted_iota(jnp.int32, (SUB, SUB), 0)
    qry = lax.broadcasted_iota(jnp.int32, (SUB, SUB), 1)
    e_ref[0, 0] = _bias_of_distance(qry - key, tab_ref, m)
    e_ref[1, 0] = _bias_of_distance(SUB + qry - key, tab_ref, m)
    t = lax.broadcasted_iota(jnp.int32, (n_tok, CHUNK), 0)
    c = lax.broadcasted_iota(jnp.int32, (n_tok, CHUNK), 1)
    b_ref[0] = _bias_of_distance(CHUNK + t - c, tab_ref, m)
    b_ref[1] = _bias_of_distance(t - c, tab_ref, m)


def _bias_tables(rel_table, n_tok):
    return pl.pallas_call(
        functools.partial(_bias_kernel, n_tok=n_tok),
        grid=(N_MAPS,),
        in_specs=[pl.BlockSpec(memory_space=pltpu.SMEM)],
        out_specs=[
            pl.BlockSpec((2, 1, SUB, SUB), lambda m: (0, m, 0, 0)),
            pl.BlockSpec((2, n_tok, CHUNK), lambda m: (0, m, 0)),
        ],
        out_shape=[
            jax.ShapeDtypeStruct((2, N_MAPS, SUB, SUB), F32),
            jax.ShapeDtypeStruct((2, N_MAPS * n_tok, CHUNK), F32),
        ],
        name="bias_tables",
    )(rel_table)


def _swiglu_half_step(x, norm_ref, wg_ref, wu_ref, wd_ref, hn_ref, acc_ref):
    hn_ref[...] = _rmsnorm(x, norm_ref[...]).astype(BF16)

    def gate_up(c):
        hn = hn_ref[...]
        cols = slice(c * FF_CHUNK, (c + 1) * FF_CHUNK)
        return (jnp.dot(hn, wg_ref[:, cols], preferred_element_type=F32),
                jnp.dot(hn, wu_ref[:, cols], preferred_element_type=F32))

    def down(c, gate, up):
        act = (jax.nn.silu(gate) * up).astype(BF16)
        d = jnp.dot(act, wd_ref[c * FF_CHUNK:(c + 1) * FF_CHUNK, :], preferred_element_type=F32)
        if c == 0:
            acc_ref[...] = d
        else:
            acc_ref[...] += d

    pending = gate_up(0)
    for c in range(N_FF_CHUNKS):
        nxt = gate_up(c + 1) if c + 1 < N_FF_CHUNKS else None
        down(c, *pending)
        pending = nxt
    return x + 0.5 * acc_ref[...]


def _ffn_in_kernel(x_ref, nf_ref, wg_ref, wu_ref, wd_ref, nm_ref, win_ref, gn_ref, ws_ref, bs_ref,
                   *rest, prompt, seq_in_chunk):
    if prompt:
        x1_ref, k_ref, v_ref, s_ref, qtm_ref, kb_ref, vt_ref, hn_ref, acc_ref = rest
    else:
        x1_ref, k_ref, v_ref, s_ref, q_ref, gv_ref, hn_ref, acc_ref = rest
    tm = x_ref.shape[0]

    x1 = _swiglu_half_step(x_ref[...], nf_ref, wg_ref, wu_ref, wd_ref, hn_ref, acc_ref)
    x1_ref[...] = x1
    h = _rmsnorm(x1, nm_ref[...]).astype(BF16)

    def proj(i):
        return jnp.dot(h, win_ref[:, i * ATTN_WIDTH:(i + 1) * ATTN_WIDTH], preferred_element_type=F32)

    q = proj(0) * (HEAD_DIM ** -0.5 * LOG2E)
    k = proj(1)
    v = proj(2)
    u = jax.nn.gelu(proj(3))
    g = jax.nn.gelu(proj(4))
    k_ref[...] = k.T[None] if prompt else k
    if prompt:
        for hd in range(N_HEADS):
            v_ref[pl.ds(hd, tm, stride=N_HEADS), :] = v[:, hd * V_DIM:(hd + 1) * V_DIM]
    else:
        v_ref[...] = v

    row = lax.broadcasted_iota(jnp.int32, (CHUNK, CHUNK), 0)
    col = lax.broadcasted_iota(jnp.int32, (CHUNK, CHUNK), 1)
    keep = col <= row
    if seq_in_chunk < CHUNK:
        keep = keep & ((row // seq_in_chunk) == (col // seq_in_chunk))
    for grp in range(N_GROUPS):
        lanes = slice(grp * GROUP, (grp + 1) * GROUP)
        gg = g[:, lanes]
        gvn = gg * lax.rsqrt(jnp.mean(gg * gg, axis=-1, keepdims=True) + EPS) * gn_ref[:, lanes]
        if not prompt:
            gv_ref[:, lanes] = gvn
        wm = jnp.where(keep, ws_ref[grp], 0.0).astype(BF16)
        bcol = bs_ref[:, grp:grp + 1]
        for c in range(tm // CHUNK):
            rows = slice(c * CHUNK, (c + 1) * CHUNK)
            mixed = jnp.dot(wm, gvn[rows, :].astype(BF16), preferred_element_type=F32) + bcol
            s_ref[rows, lanes] = (u[rows, lanes] * mixed).astype(BF16)

    if prompt:
        for hd in range(N_HEADS):
            lanes = slice(hd * V_DIM, (hd + 1) * V_DIM)
            qt = q[:, lanes].T
            feat = lax.broadcasted_iota(jnp.int32, qt.shape, 0)
            qtm_ref[0, hd, 0] = jnp.where(feat < HEAD_DIM, qt, 0.0).astype(BF16)
            qtm_ref[0, hd, 1] = jnp.where(feat >= HEAD_DIM, qt, 0.0).astype(BF16)
            kb_ref[0, hd] = k[:, lanes].astype(BF16)
            vt_ref[0, hd, 0:V_DIM] = v[:, lanes].T.astype(BF16)
            ones_row = lax.broadcasted_iota(jnp.int32, (VT_ROWS - V_DIM, tm), 0) == 0
            vt_ref[0, hd, V_DIM:VT_ROWS] = ones_row.astype(F32).astype(BF16)
    else:
        n_tok = seq_in_chunk
        r = lax.broadcasted_iota(jnp.int32, (N_MAPS * n_tok, ATTN_WIDTH), 0)
        c = lax.broadcasted_iota(jnp.int32, (N_MAPS * n_tok, ATTN_WIDTH), 1)
        own = (r // n_tok) == (c // HEAD_DIM)
        n_q = N_MAPS * n_tok
        for sq in range(tm // n_tok):
            tok = slice(sq * n_tok, (sq + 1) * n_tok)
            q_seq = jnp.concatenate([q[tok]] * N_MAPS, axis=0)
            q_ref[sq, 0:n_q] = jnp.where(own, q_seq, 0.0)
            q_ref[sq, n_q:n_q + n_tok] = k[tok]
            q_ref[sq, n_q + n_tok:n_q + 2 * n_tok] = v[tok]


def _ffn_in(x, nf, wg, wu, wd, nm, win, gn, ws, bs, *, batch, seq_in_chunk, prompt):
    n = x.shape[0]
    tm = min(TOKEN_BLOCK, n)
    nb = n // tm
    per_batch = nb // batch
    tok = lambda w: pl.BlockSpec((tm, w), lambda i: (i, 0))
    in_specs = [
        tok(D_MODEL),
        _const_spec((1, D_MODEL)),
        *_gate_up_specs(), _const_spec(wd.shape),
        _const_spec((1, D_MODEL)),
        _const_spec(win.shape),
        _const_spec((1, GMLP_WIDTH)),
        _const_spec(ws.shape), _const_spec(bs.shape),
    ]
    out_specs = [tok(D_MODEL), tok(ATTN_WIDTH), tok(ATTN_WIDTH), tok(GMLP_WIDTH)]
    out_shape = [
        jax.ShapeDtypeStruct((n, D_MODEL), F32),
        jax.ShapeDtypeStruct((n, ATTN_WIDTH), F32),
        jax.ShapeDtypeStruct((n, ATTN_WIDTH), F32),
        jax.ShapeDtypeStruct((n, GMLP_WIDTH), BF16),
    ]
    if prompt:
        t = n // batch
        out_specs[1] = pl.BlockSpec((1, ATTN_WIDTH, tm), lambda i: (i // per_batch, 0, i % per_batch))
        out_shape[1] = jax.ShapeDtypeStruct((batch, ATTN_WIDTH, t), F32)
        out_specs[2] = pl.BlockSpec((tm * N_HEADS, V_DIM), lambda i: (i, 0))
        out_shape[2] = jax.ShapeDtypeStruct((n * N_HEADS, V_DIM), F32)
        out_specs += [
            pl.BlockSpec((1, N_HEADS, 2, V_DIM, tm), lambda i: (i // per_batch, 0, 0, 0, i % per_batch)),
            pl.BlockSpec((1, N_HEADS, tm, V_DIM), lambda i: (i // per_batch, 0, i % per_batch, 0)),
            pl.BlockSpec((1, N_HEADS, VT_ROWS, tm), lambda i: (i // per_batch, 0, 0, i % per_batch)),
        ]
        out_shape += [
            jax.ShapeDtypeStruct((batch, N_HEADS, 2, V_DIM, t), BF16),
            jax.ShapeDtypeStruct((batch, N_HEADS, t, V_DIM), BF16),
            jax.ShapeDtypeStruct((batch, N_HEADS, VT_ROWS, t), BF16),
        ]
    else:
        rows = (N_MAPS + 2) * seq_in_chunk
        out_specs += [pl.BlockSpec((tm // seq_in_chunk, rows, ATTN_WIDTH), lambda i: (i, 0, 0)),
                      tok(GMLP_WIDTH)]
        out_shape += [
            jax.ShapeDtypeStruct((n // seq_in_chunk, rows, ATTN_WIDTH), F32),
            jax.ShapeDtypeStruct((n, GMLP_WIDTH), F32),
        ]
    return pl.pallas_call(
        functools.partial(_ffn_in_kernel, prompt=prompt, seq_in_chunk=seq_in_chunk),
        grid=(nb,),
        in_specs=in_specs,
        out_specs=out_specs,
        out_shape=out_shape,
        scratch_shapes=[pltpu.VMEM((tm, D_MODEL), BF16), pltpu.VMEM((tm, D_MODEL), F32)],
        compiler_params=pltpu.CompilerParams(dimension_semantics=("arbitrary",),
                                             vmem_limit_bytes=VMEM_LIMIT),
        name="ffn_in_prompt" if prompt else "ffn_in_sample",
    )(x, nf, wg, wu, wd, nm, win, gn, ws, bs)


def _out_ffn_kernel(x1_ref, o_ref, s_ref, wout_ref, nf_ref, wg_ref, wu_ref, wd_ref, nfin_ref,
                    y_ref, hn_ref, acc_ref, *, head_major):
    if head_major:
        o = jnp.concatenate([o_ref[0, hd] for hd in range(N_HEADS)], axis=1)
    else:
        o = o_ref[...]
    x2 = (x1_ref[...]
          + jnp.dot(o, wout_ref[0:ATTN_WIDTH, :], preferred_element_type=F32)
          + jnp.dot(s_ref[...], wout_ref[ATTN_WIDTH:D_MODEL, :], preferred_element_type=F32))
    x3 = _swiglu_half_step(x2, nf_ref, wg_ref, wu_ref, wd_ref, hn_ref, acc_ref)
    y_ref[...] = _rmsnorm(x3, nfin_ref[...])


def _out_ffn(x1, o, s, wout, nf, wg, wu, wd, nfin, *, batch, head_major):
    n = x1.shape[0]
    tm = min(TOKEN_BLOCK, n)
    nb = n // tm
    per_batch = nb // batch
    tok = lambda w: pl.BlockSpec((tm, w), lambda i: (i, 0))
    if head_major:
        o_spec = pl.BlockSpec((1, N_HEADS, tm, V_DIM), lambda i: (i // per_batch, 0, i % per_batch, 0))
    else:
        o_spec = tok(ATTN_WIDTH)
    return pl.pallas_call(
        functools.partial(_out_ffn_kernel, head_major=head_major),
        grid=(nb,),
        in_specs=[
            tok(D_MODEL), o_spec, tok(GMLP_WIDTH),
            _const_spec(wout.shape),
            _const_spec((1, D_MODEL)),
            *_gate_up_specs(), _const_spec(wd.shape),
            _const_spec((1, D_MODEL)),
        ],
        out_specs=tok(D_MODEL),
        out_shape=jax.ShapeDtypeStruct((n, D_MODEL), F32),
        scratch_shapes=[pltpu.VMEM((tm, D_MODEL), BF16), pltpu.VMEM((tm, D_MODEL), F32)],
        compiler_params=pltpu.CompilerParams(dimension_semantics=("arbitrary",),
                                             vmem_limit_bytes=VMEM_LIMIT),
        name="out_ffn_prompt" if head_major else "out_ffn_sample",
    )(x1, o, s, wout, nf, wg, wu, wd, nfin)


def _attn_kernel(qi_ref, ki_ref, pt_ref, grp_ref, seq_of_ref, qtm_ref, kb_ref, vt_ref, e_ref, b_ref,
                 small_ref, seq_ref, kt_hbm, v_hbm, o_ref, os_ref, m_ref, acc_ref, ms_ref, ls_ref, accs_ref,
                 kbuf, vbuf, page_sem, *, n_sample_steps, groups_per_seq):
    step = pl.program_id(1)
    qi = qi_ref[step]
    ki = ki_ref[step]
    g_raw = grp_ref[pl.program_id(0) * pl.num_programs(1) + step]
    has_group = g_raw >= 0
    g = jnp.maximum(g_raw, 0)
    slot = g % 2

    def group_copies(group, buf_slot):
        seq = group // groups_per_seq
        first_page = (group % groups_per_seq) * PAGES_PER_STEP
        copies = []
        for i in range(PAGES_PER_STEP):
            page = pt_ref[seq, first_page + i]
            copies.append(pltpu.make_async_copy(kt_hbm.at[page], kbuf.at[buf_slot, i], page_sem.at[buf_slot]))
            copies.append(pltpu.make_async_copy(v_hbm.at[page], vbuf.at[buf_slot, i], page_sem.at[buf_slot]))
        return copies

    @pl.when(g_raw == 0)
    def _():
        for cp in group_copies(0, 0):
            cp.start()

    @pl.when(has_group & (g_raw + 1 < n_sample_steps))
    def _():
        for cp in group_copies(g + 1, 1 - slot):
            cp.start()

    @pl.when(has_group)
    def _():
        for cp in group_copies(g, slot):
            cp.wait()

    def sample_step():
        n_tok = seq_ref.shape[1] // (N_MAPS + 2)
        n_q = N_MAPS * n_tok
        head_rows = 2 * n_tok
        valid = has_group
        j = g % groups_per_seq
        first = j == 0
        last = j == groups_per_seq - 1

        qbd = seq_ref[0, 0:n_q].astype(BF16)
        blocks = []
        for i in range(0, PAGES_PER_STEP, 2):
            keys = jnp.concatenate([kbuf[slot, i].astype(BF16), kbuf[slot, i + 1].astype(BF16)], axis=1)
            blocks.append(jnp.dot(qbd, keys, preferred_element_type=F32))
            yield
        bl = jnp.where(last, b_ref[0], 0.0)
        blocks[-1] = blocks[-1] + jnp.concatenate([jnp.zeros_like(bl), bl], axis=1)
        pad = jnp.zeros((CHUNK - n_tok, ATTN_WIDTH), F32)
        k_new = jnp.concatenate([seq_ref[0, n_q:n_q + n_tok], pad], axis=0).astype(BF16)
        v_new = jnp.concatenate([seq_ref[0, n_q + n_tok:n_q + 2 * n_tok], pad], axis=0)
        s_new = lax.dot_general(qbd, k_new, (((1,), (1,)), ((), ())), preferred_element_type=F32)
        blocks.append(s_new + jnp.where(last, b_ref[1], NEG))
        s = jnp.concatenate(blocks, axis=1) + jnp.where(valid, 0.0, NEG)

        m_prev = jnp.where(first, NEG, ms_ref[...])
        l_prev = jnp.where(first, 0.0, ls_ref[...])
        acc_prev = jnp.where(first, 0.0, accs_ref[...])
        m_new = jnp.maximum(m_prev, jnp.max(s, axis=1, keepdims=True))
        yield
        alpha = jnp.exp2(m_prev - m_new)
        p = jnp.exp2(s - m_new)
        l_new = alpha * l_prev + jnp.sum(p, axis=1, keepdims=True)
        yield
        pv = None
        pb = p.astype(BF16)
        for i in range(PAGES_PER_STEP + 1):
            if i < PAGES_PER_STEP:
                vals = jnp.concatenate([vbuf[slot, i, pl.ds(hd, CHUNK, stride=N_HEADS), :]
                                        for hd in range(N_HEADS)], axis=1)
            else:
                vals = v_new
            d = jnp.dot(pb[:, i * CHUNK:(i + 1) * CHUNK], vals.astype(BF16),
                        preferred_element_type=F32)
            pv = d if pv is None else pv + d
            if i % 3 == 2:
                yield
        acc_new = alpha * acc_prev + pv
        ms_ref[...] = m_new
        ls_ref[...] = l_new
        accs_ref[...] = acc_new

        lam = _lam(small_ref)
        o_all = acc_new * (1.0 / l_new)
        for hd in range(N_HEADS):
            r1 = slice(hd * head_rows, hd * head_rows + n_tok)
            r2 = slice(hd * head_rows + n_tok, (hd + 1) * head_rows)
            lanes = slice(hd * V_DIM, (hd + 1) * V_DIM)
            o = o_all[r1, lanes] - lam * o_all[r2, lanes]
            os_ref[0, :, lanes] = (
                _rmsnorm(o, small_ref[0:1, :]) * (1.0 - LAM_INIT)).astype(BF16)

    @pl.when(ki == 0)
    def _():
        m_ref[...] = jnp.full_like(m_ref, NEG)
        acc_ref[...] = jnp.zeros_like(acc_ref)


    def unit_logits(unit):
        hd, mp, q_lo, q_n, key_blocks = unit
        idx = 2 * hd + mp
        qm = qtm_ref[0, hd, mp, :, q_lo:q_lo + q_n]
        logits = []
        for k_lo, k_n, bias_kind in key_blocks:
            s = jnp.dot(kb_ref[0, hd, k_lo:k_lo + k_n, :], qm, preferred_element_type=F32)
            if bias_kind is not None:
                s = s + e_ref[bias_kind, idx]
            logits.append(s)
        return logits

    def unit_update(unit, logits):
        hd, mp, q_lo, q_n, key_blocks = unit
        idx = 2 * hd + mp
        cols = slice(q_lo, q_lo + q_n)
        m_prev = m_ref[idx, :, cols]
        m_new = m_prev
        for s in logits:
            m_new = jnp.maximum(m_new, jnp.max(s, axis=0, keepdims=True))
        alpha = jnp.exp2(m_prev - m_new)
        pv = None
        for (k_lo, k_n, _), s in zip(key_blocks, logits):
            p = jnp.exp2(s - m_new)
            d = jnp.dot(vt_ref[0, hd, :, k_lo:k_lo + k_n], p.astype(BF16), preferred_element_type=F32)
            pv = d if pv is None else pv + d
        acc_ref[idx, :, cols] = alpha * acc_ref[idx, :, cols] + pv
        m_ref[idx, :, cols] = m_new

    def run(units, with_sample):
        side = sample_step() if with_sample else iter(())
        pending = []
        for n, unit in enumerate(units):
            pending.append((unit, unit_logits(unit)))
            if n % 2 == 0:
                next(side, None)
            if len(pending) > PIPE_DEPTH:
                unit_update(*pending.pop(0))
        for item in pending:
            unit_update(*item)
            next(side, None)
        for _ in side:
            pass

    maps = [(hd, mp) for hd in range(N_HEADS) for mp in range(2)]

    far = lambda hd, mp, lo: [(hd, mp, lo, SUB, [(0, TK, None)]), (hd, mp, lo + SUB, SUB, [(0, TK, None)])]
    near = lambda hd, mp, lo: [(hd, mp, lo, SUB, [(0, SUB, None), (SUB, SUB, 1)]),
                               (hd, mp, lo + SUB, SUB, [(0, TK, None)])]
    diag = lambda hd, mp, lo: [(hd, mp, lo, SUB, [(0, SUB, 0)]),
                               (hd, mp, lo + SUB, SUB, [(0, SUB, 1), (SUB, SUB, 0)])]
    parts = TQ // TK
    first_key_tile = qi * parts

    def kinds_for(offset):
        kinds = []
        for r in range(parts):
            rel = r - offset
            kinds.append(far if rel >= 2 else near if rel == 1 else diag if rel == 0 else None)
        return kinds

    def units_for(kinds):
        return [u for hd, mp in maps for r, kind in enumerate(kinds) if kind is not None
                for u in kind(hd, mp, r * TK)]

    @pl.when(ki < first_key_tile - 1)
    def _():
        run(units_for([far] * parts), with_sample=True)

    for offset in range(-1, parts):
        @pl.when(ki == first_key_tile + offset)
        def _(offset=offset):
            run(units_for(kinds_for(offset)), with_sample=offset <= 0)

    @pl.when(ki == first_key_tile + parts - 1)
    def _():
        lam = _lam(small_ref)
        for hd in range(N_HEADS):
            o1 = acc_ref[2 * hd, 0:V_DIM] * (1.0 / acc_ref[2 * hd, V_DIM:V_DIM + 1])
            o2 = acc_ref[2 * hd + 1, 0:V_DIM] * (1.0 / acc_ref[2 * hd + 1, V_DIM:V_DIM + 1])
            o = o1 - lam * o2
            inv = lax.rsqrt(jnp.mean(o * o, axis=0, keepdims=True) + EPS)
            on = (o * inv).T * small_ref[0:1, :] * (1.0 - LAM_INIT)
            o_ref[0, hd] = on.astype(BF16)


def _attention(qtm, kb, vt, e_tiles, b_tiles, small, page_table, seq_pack, cache_kt, cache_v):
    batch, _, _, _, t = qtm.shape
    nq = t // TQ
    pairs = [(q, k) for q in range(nq) for k in range((q + 1) * (TQ // TK))]
    n_steps = len(pairs)
    qi_arr = jnp.asarray([p[0] for p in pairs], jnp.int32)
    ki_arr = jnp.asarray([p[1] for p in pairs], jnp.int32)
    n_seq = seq_pack.shape[0]
    n_tok = seq_pack.shape[1] // (N_MAPS + 2)
    groups_per_seq = page_table.shape[1] // PAGES_PER_STEP
    n_sample_steps = n_seq * groups_per_seq
    n_rows = N_MAPS * n_tok
    grp, seq_of, nxt = [], [], 0
    for _ in range(batch):
        for q, k in pairs:
            takes = k - q * (TQ // TK) <= 0 and nxt < n_sample_steps
            grp.append(nxt if takes else -1)
            nxt += takes
            seq_of.append(max(nxt - 1, 0) // groups_per_seq)
    assert page_table.shape[1] % PAGES_PER_STEP == 0 and nxt == n_sample_steps
    grp_arr = jnp.asarray(grp, jnp.int32)
    seq_arr = jnp.asarray(seq_of, jnp.int32)

    def seq_spec(rows):
        return pl.BlockSpec((1, rows, ATTN_WIDTH),
                            lambda b, s, qi, ki, pt, grp, seq_of: (seq_of[b * n_steps + s], 0, 0))

    def full(shape, **kw):
        return pl.BlockSpec(shape, lambda b, s, qi, ki, pt, grp, seq_of: (0,) * len(shape), **kw)

    const = lambda a: full(a.shape, pipeline_mode=pl.Buffered(1))
    grid_spec = pltpu.PrefetchScalarGridSpec(
        num_scalar_prefetch=5,
        grid=(batch, n_steps),
        in_specs=([
            pl.BlockSpec((1, N_HEADS, 2, V_DIM, TQ), lambda b, s, qi, ki, pt, grp, seq_of: (b, 0, 0, 0, qi[s])),
            pl.BlockSpec((1, N_HEADS, TK, V_DIM), lambda b, s, qi, ki, pt, grp, seq_of: (b, 0, ki[s], 0)),
            pl.BlockSpec((1, N_HEADS, VT_ROWS, TK), lambda b, s, qi, ki, pt, grp, seq_of: (b, 0, 0, ki[s])),
            const(e_tiles), const(b_tiles), const(small), seq_spec(seq_pack.shape[1]),
            pl.BlockSpec(memory_space=pl.ANY), pl.BlockSpec(memory_space=pl.ANY)]),
        out_specs=[
            pl.BlockSpec((1, N_HEADS, TQ, V_DIM), lambda b, s, qi, ki, pt, grp, seq_of: (b, 0, qi[s], 0)),
            seq_spec(n_tok),
        ],
        scratch_shapes=[
            pltpu.VMEM((N_MAPS, 1, TQ), F32),
            pltpu.VMEM((N_MAPS, VT_ROWS, TQ), F32),
            pltpu.VMEM((n_rows, 1), F32),
            pltpu.VMEM((n_rows, 1), F32),
            pltpu.VMEM((n_rows, ATTN_WIDTH), F32),
            pltpu.VMEM((2, PAGES_PER_STEP, ATTN_WIDTH, CHUNK), F32),
            pltpu.VMEM((2, PAGES_PER_STEP, ATTN_WIDTH, CHUNK), F32),
            pltpu.SemaphoreType.DMA((2,)),
        ],
    )
    return pl.pallas_call(
        functools.partial(_attn_kernel, n_sample_steps=n_sample_steps, groups_per_seq=groups_per_seq),
        grid_spec=grid_spec,
        out_shape=[
            jax.ShapeDtypeStruct((batch, N_HEADS, t, V_DIM), BF16),
            jax.ShapeDtypeStruct((n_seq, n_tok, ATTN_WIDTH), BF16),
        ],
        compiler_params=pltpu.CompilerParams(dimension_semantics=("arbitrary", "arbitrary"),
                                             vmem_limit_bytes=VMEM_LIMIT),
        name="attention",
    )(qi_arr, ki_arr, page_table, grp_arr, seq_arr, qtm, kb, vt, e_tiles, b_tiles, small, seq_pack,
      cache_kt, cache_v)


def _ffn_weights(w_gu, w_down):
    w_gu = w_gu.astype(BF16)
    return w_gu, w_gu, w_down.astype(BF16)


def kernel(x_prompt, x_sample, cache_k, cache_v, page_table, rel_table, norm_ffn1, w_ffn1_gu, w_ffn1_down, norm_mix, w_in, lambda_q1, lambda_k1, lambda_q2, lambda_k2, subln, gmlp_norm, w_spatial, b_spatial, w_out, norm_ffn2, w_ffn2_gu, w_ffn2_down, norm_final):
    depth = cache_k.shape[0]
    assert depth == 1
    batch, t_p, _ = x_prompt.shape
    n_seq, t_s, _ = x_sample.shape
    n_pool, page = cache_k.shape[1], cache_k.shape[2]
    assert page == CHUNK and CHUNK % t_s == 0 and (n_seq * t_s) % CHUNK == 0

    row = lambda a: a.reshape(1, -1)
    wg1, wu1, wd1 = _ffn_weights(w_ffn1_gu[0], w_ffn1_down[0])
    wg2, wu2, wd2 = _ffn_weights(w_ffn2_gu[0], w_ffn2_down[0])
    win = w_in[0].astype(BF16)
    wout = w_out[0].astype(BF16)
    lamv = jnp.stack([lambda_q1[0], lambda_k1[0], lambda_q2[0], lambda_k2[0]])
    small = jnp.concatenate([
        row(subln[0]), jnp.pad(lamv, ((0, 0), (0, V_DIM - HEAD_DIM))), jnp.zeros((3, V_DIM), F32)])
    reps = CHUNK // t_s
    ws_sample = jnp.tile(w_spatial[0][:, :t_s, :t_s], (1, reps, reps))
    bs_sample = jnp.tile(b_spatial[0][:, :t_s], (1, reps)).T
    ffn1 = (row(norm_ffn1[0]), wg1, wu1, wd1)
    mix = (row(norm_mix[0]), win, row(gmlp_norm[0]))

    e_tiles, b_tiles = _bias_tables(rel_table, t_s)

    x1p, kp, vp, sp, qtm, kb, vt = _ffn_in(
        x_prompt.reshape(batch * t_p, D_MODEL), *ffn1, *mix, w_spatial[0], b_spatial[0].T,
        batch=batch, seq_in_chunk=CHUNK, prompt=True)
    x1s, ks, vs, ss, qs, gvs = _ffn_in(
        x_sample.reshape(n_seq * t_s, D_MODEL), *ffn1, *mix, ws_sample, bs_sample,
        batch=1, seq_in_chunk=t_s, prompt=False)

    op, os_ = _attention(
        qtm, kb, vt, e_tiles, b_tiles, small, page_table, qs,
        jnp.transpose(cache_k[0], (0, 2, 3, 1)).reshape(n_pool, ATTN_WIDTH, page),
        cache_v[0].reshape(n_pool, page * N_HEADS, V_DIM))

    ffn2 = (wout, row(norm_ffn2[0]), wg2, wu2, wd2, row(norm_final))
    yp = _out_ffn(x1p, op, sp, *ffn2, batch=batch, head_major=True)
    ys = _out_ffn(x1s, os_.reshape(n_seq * t_s, ATTN_WIDTH), ss, *ffn2, batch=1, head_major=False)

    return (
        yp.reshape(batch, t_p, D_MODEL),
        ys.reshape(n_seq, t_s, D_MODEL),
        jnp.transpose(kp.reshape(depth, batch, N_MAPS, HEAD_DIM, t_p), (0, 1, 4, 2, 3)),
        vp.reshape(depth, batch, t_p, N_HEADS, V_DIM),
        ks.reshape(depth, n_seq, t_s, N_MAPS, HEAD_DIM),
        vs.reshape(depth, n_seq, t_s, N_HEADS, V_DIM),
        gvs.reshape(depth, n_seq, t_s, GMLP_WIDTH),
    )
```

```python
import functools
import math

import jax
import jax.numpy as jnp
from jax import lax
from jax.experimental import pallas as pl
from jax.experimental.pallas import tpu as pltpu

F32 = jnp.float32
BF16 = jnp.bfloat16

D_MODEL = 1024
D_FF = 2816
FF_CHUNK = 256
N_FF_CHUNKS = D_FF // FF_CHUNK
ATTN_WIDTH = 512
HEAD_DIM = 64
N_HEADS = 4
N_MAPS = 2 * N_HEADS
V_DIM = 2 * HEAD_DIM
GMLP_WIDTH = 512
N_GROUPS = 4
GROUP = GMLP_WIDTH // N_GROUPS
CHUNK = 128
N_BUCKETS = 32
MAX_EXACT = N_BUCKETS // 2
MAX_DISTANCE = 128
EPS = 1e-6
NEG = -1e30
BUCKET_THRESHOLDS = tuple(
    math.ceil(MAX_EXACT * (MAX_DISTANCE / MAX_EXACT) ** (k / (N_BUCKETS - MAX_EXACT)))
    for k in range(1, N_BUCKETS - MAX_EXACT))
LAM_INIT = 0.8 - 0.6 * math.exp(-0.3 * 0)
LOG2E = math.log2(math.e)

TOKEN_BLOCK = 512
TQ = 1024
TK = 512
SUB = 256
PIPE_DEPTH = 6
VT_ROWS = V_DIM + 16
PAGES_PER_STEP = 16
VMEM_LIMIT = 52 * 1024 * 1024


def _const_spec(shape):
    nd = len(shape)
    return pl.BlockSpec(shape, lambda *_: (0,) * nd, pipeline_mode=pl.Buffered(1))


def _gate_up_specs():
    half = lambda j: pl.BlockSpec((D_MODEL, D_FF), lambda *_: (0, j), pipeline_mode=pl.Buffered(1))
    return [half(0), half(1)]


def _rmsnorm(x, g):
    return x * lax.rsqrt(jnp.mean(x * x, axis=-1, keepdims=True) + EPS) * g


def _lam(small_ref):
    a = jnp.sum(small_ref[1:2, :] * small_ref[2:3, :], axis=1, keepdims=True)
    b = jnp.sum(small_ref[3:4, :] * small_ref[4:5, :], axis=1, keepdims=True)
    return jnp.exp(a) - jnp.exp(b) + LAM_INIT


def _bias_of_distance(d, tab_ref, m):
    n = jnp.maximum(d, 0)
    large = MAX_EXACT
    for thr in BUCKET_THRESHOLDS:
        large = large + (n >= thr).astype(jnp.int32)
    bucket = jnp.where(n < MAX_EXACT, n, large)
    val = jnp.zeros(d.shape, F32)
    for b in range(N_BUCKETS):
        val = jnp.where(bucket == b, tab_ref[b, m], val)
    val = (val - tab_ref[N_BUCKETS - 1, m]) * LOG2E
    return jnp.where(d >= 0, val, NEG)


def _bias_kernel(tab_ref, e_ref, b_ref, *, n_tok):
    m = pl.program_id(0)
    key = lax.broadcasted_iota(jnp.int32, (SUB, SUB), 0)
    qry = lax.broadcasted_iota(jnp.int32, (SUB, SUB), 1)
    e_ref[0, 0] = _bias_of_distance(qry - key, tab_ref, m)
    e_ref[1, 0] = _bias_of_distance(SUB + qry - key, tab_ref, m)
    t = lax.broadcasted_iota(jnp.int32, (n_tok, CHUNK), 0)
    c = lax.broadcasted_iota(jnp.int32, (n_tok, CHUNK), 1)
    b_ref[0] = _bias_of_distance(CHUNK + t - c, tab_ref, m)
    b_ref[1] = _bias_of_distance(t - c, tab_ref, m)


def _bias_tables(rel_table, n_tok):
    return pl.pallas_call(
        functools.partial(_bias_kernel, n_tok=n_tok),
        grid=(N_MAPS,),
        in_specs=[pl.BlockSpec(memory_space=pltpu.SMEM)],
        out_specs=[
            pl.BlockSpec((2, 1, SUB, SUB), lambda m: (0, m, 0, 0)),
            pl.BlockSpec((2, n_tok, CHUNK), lambda m: (0, m, 0)),
        ],
        out_shape=[
            jax.ShapeDtypeStruct((2, N_MAPS, SUB, SUB), F32),
            jax.ShapeDtypeStruct((2, N_MAPS * n_tok, CHUNK), F32),
        ],
        name="bias_tables",
    )(rel_table)


def _swiglu_half_step(x, norm_ref, wg_ref, wu_ref, wd_ref, hn_ref, acc_ref):
    hn_ref[...] = _rmsnorm(x, norm_ref[...]).astype(BF16)

    def gate_up(c):
        hn = hn_ref[...]
        cols = slice(c * FF_CHUNK, (c + 1) * FF_CHUNK)
        return (jnp.dot(hn, wg_ref[:, cols], preferred_element_type=F32),
                jnp.dot(hn, wu_ref[:, cols], preferred_element_type=F32))

    def down(c, gate, up):
        act = (jax.nn.silu(gate) * up).astype(BF16)
        d = jnp.dot(act, wd_ref[c * FF_CHUNK:(c + 1) * FF_CHUNK, :], preferred_element_type=F32)
        if c == 0:
            acc_ref[...] = d
        else:
            acc_ref[...] += d

    pending = gate_up(0)
    for c in range(N_FF_CHUNKS):
        nxt = gate_up(c + 1) if c + 1 < N_FF_CHUNKS else None
        down(c, *pending)
        pending = nxt
    return x + 0.5 * acc_ref[...]


def _ffn_in_kernel(x_ref, nf_ref, wg_ref, wu_ref, wd_ref, nm_ref, win_ref, gn_ref, ws_ref, bs_ref,
                   *rest, prompt, seq_in_chunk):
    if prompt:
        x1_ref, k_ref, v_ref, s_ref, qtm_ref, kb_ref, vt_ref, hn_ref, acc_ref = rest
    else:
        x1_ref, k_ref, v_ref, s_ref, q_ref, gv_ref, hn_ref, acc_ref = rest
    tm = x_ref.shape[0]

    x1 = _swiglu_half_step(x_ref[...], nf_ref, wg_ref, wu_ref, wd_ref, hn_ref, acc_ref)
    x1_ref[...] = x1
    h = _rmsnorm(x1, nm_ref[...]).astype(BF16)

    def proj(i):
        return jnp.dot(h, win_ref[:, i * ATTN_WIDTH:(i + 1) * ATTN_WIDTH], preferred_element_type=F32)

    q = proj(0) * (HEAD_DIM ** -0.5 * LOG2E)
    k = proj(1)
    v = proj(2)
    u = jax.nn.gelu(proj(3))
    g = jax.nn.gelu(proj(4))
    k_ref[...] = k.T[None] if prompt else k
    if prompt:
        for hd in range(N_HEADS):
            v_ref[pl.ds(hd, tm, stride=N_HEADS), :] = v[:, hd * V_DIM:(hd + 1) * V_DIM]
    else:
        v_ref[...] = v

    row = lax.broadcasted_iota(jnp.int32, (CHUNK, CHUNK), 0)
    col = lax.broadcasted_iota(jnp.int32, (CHUNK, CHUNK), 1)
    keep = col <= row
    if seq_in_chunk < CHUNK:
        keep = keep & ((row // seq_in_chunk) == (col // seq_in_chunk))
    for grp in range(N_GROUPS):
        lanes = slice(grp * GROUP, (grp + 1) * GROUP)
        gg = g[:, lanes]
        gvn = gg * lax.rsqrt(jnp.mean(gg * gg, axis=-1, keepdims=True) + EPS) * gn_ref[:, lanes]
        if not prompt:
            gv_ref[:, lanes] = gvn
        wm = jnp.where(keep, ws_ref[grp], 0.0).astype(BF16)
        bcol = bs_ref[:, grp:grp + 1]
        for c in range(tm // CHUNK):
            rows = slice(c * CHUNK, (c + 1) * CHUNK)
            mixed = jnp.dot(wm, gvn[rows, :].astype(BF16), preferred_element_type=F32) + bcol
            s_ref[rows, lanes] = (u[rows, lanes] * mixed).astype(BF16)

    if prompt:
        for hd in range(N_HEADS):
            lanes = slice(hd * V_DIM, (hd + 1) * V_DIM)
            qt = q[:, lanes].T
            feat = lax.broadcasted_iota(jnp.int32, qt.shape, 0)
            qtm_ref[0, hd, 0] = jnp.where(feat < HEAD_DIM, qt, 0.0).astype(BF16)
            qtm_ref[0, hd, 1] = jnp.where(feat >= HEAD_DIM, qt, 0.0).astype(BF16)
            kb_ref[0, hd] = k[:, lanes].astype(BF16)
            vt_ref[0, hd, 0:V_DIM] = v[:, lanes].T.astype(BF16)
            ones_row = lax.broadcasted_iota(jnp.int32, (VT_ROWS - V_DIM, tm), 0) == 0
            vt_ref[0, hd, V_DIM:VT_ROWS] = ones_row.astype(F32).astype(BF16)
    else:
        n_tok = seq_in_chunk
        r = lax.broadcasted_iota(jnp.int32, (N_MAPS * n_tok, ATTN_WIDTH), 0)
        c = lax.broadcasted_iota(jnp.int32, (N_MAPS * n_tok, ATTN_WIDTH), 1)
        own = (r // n_tok) == (c // HEAD_DIM)
        n_q = N_MAPS * n_tok
        for sq in range(tm // n_tok):
            tok = slice(sq * n_tok, (sq + 1) * n_tok)
            q_seq = jnp.concatenate([q[tok]] * N_MAPS, axis=0)
            q_ref[sq, 0:n_q] = jnp.where(own, q_seq, 0.0)
            q_ref[sq, n_q:n_q + n_tok] = k[tok]
            q_ref[sq, n_q + n_tok:n_q + 2 * n_tok] = v[tok]


def _ffn_in(x, nf, wg, wu, wd, nm, win, gn, ws, bs, *, batch, seq_in_chunk, prompt):
    n = x.shape[0]
    tm = min(TOKEN_BLOCK, n)
    nb = n // tm
    per_batch = nb // batch
    tok = lambda w: pl.BlockSpec((tm, w), lambda i: (i, 0))
    in_specs = [
        tok(D_MODEL),
        _const_spec((1, D_MODEL)),
        *_gate_up_specs(), _const_spec(wd.shape),
        _const_spec((1, D_MODEL)),
        _const_spec(win.shape),
        _const_spec((1, GMLP_WIDTH)),
        _const_spec(ws.shape), _const_spec(bs.shape),
    ]
    out_specs = [tok(D_MODEL), tok(ATTN_WIDTH), tok(ATTN_WIDTH), tok(GMLP_WIDTH)]
    out_shape = [
        jax.ShapeDtypeStruct((n, D_MODEL), F32),
        jax.ShapeDtypeStruct((n, ATTN_WIDTH), F32),
        jax.ShapeDtypeStruct((n, ATTN_WIDTH), F32),
        jax.ShapeDtypeStruct((n, GMLP_WIDTH), BF16),
    ]
    if prompt:
        t = n // batch
        out_specs[1] = pl.BlockSpec((1, ATTN_WIDTH, tm), lambda i: (i // per_batch, 0, i % per_batch))
        out_shape[1] = jax.ShapeDtypeStruct((batch, ATTN_WIDTH, t), F32)
        out_specs[2] = pl.BlockSpec((tm * N_HEADS, V_DIM), lambda i: (i, 0))
        out_shape[2] = jax.ShapeDtypeStruct((n * N_HEADS, V_DIM), F32)
        out_specs += [
            pl.BlockSpec((1, N_HEADS, 2, V_DIM, tm), lambda i: (i // per_batch, 0, 0, 0, i % per_batch)),
            pl.BlockSpec((1, N_HEADS, tm, V_DIM), lambda i: (i // per_batch, 0, i % per_batch, 0)),
            pl.BlockSpec((1, N_HEADS, VT_ROWS, tm), lambda i: (i // per_batch, 0, 0, i % per_batch)),
        ]
        out_shape += [
            jax.ShapeDtypeStruct((batch, N_HEADS, 2, V_DIM, t), BF16),
            jax.ShapeDtypeStruct((batch, N_HEADS, t, V_DIM), BF16),
            jax.ShapeDtypeStruct((batch, N_HEADS, VT_ROWS, t), BF16),
        ]
    else:
        rows = (N_MAPS + 2) * seq_in_chunk
        out_specs += [pl.BlockSpec((tm // seq_in_chunk, rows, ATTN_WIDTH), lambda i: (i, 0, 0)),
                      tok(GMLP_WIDTH)]
        out_shape += [
            jax.ShapeDtypeStruct((n // seq_in_chunk, rows, ATTN_WIDTH), F32),
            jax.ShapeDtypeStruct((n, GMLP_WIDTH), F32),
        ]
    return pl.pallas_call(
        functools.partial(_ffn_in_kernel, prompt=prompt, seq_in_chunk=seq_in_chunk),
        grid=(nb,),
        in_specs=in_specs,
        out_specs=out_specs,
        out_shape=out_shape,
        scratch_shapes=[pltpu.VMEM((tm, D_MODEL), BF16), pltpu.VMEM((tm, D_MODEL), F32)],
        compiler_params=pltpu.CompilerParams(dimension_semantics=("arbitrary",),
                                             vmem_limit_bytes=VMEM_LIMIT),
        name="ffn_in_prompt" if prompt else "ffn_in_sample",
    )(x, nf, wg, wu, wd, nm, win, gn, ws, bs)


def _out_ffn_kernel(x1_ref, o_ref, s_ref, wout_ref, nf_ref, wg_ref, wu_ref, wd_ref, nfin_ref,
                    y_ref, hn_ref, acc_ref, *, head_major):
    if head_major:
        o = jnp.concatenate([o_ref[0, hd] for hd in range(N_HEADS)], axis=1)
    else:
        o = o_ref[...]
    x2 = (x1_ref[...]
          + jnp.dot(o, wout_ref[0:ATTN_WIDTH, :], preferred_element_type=F32)
          + jnp.dot(s_ref[...], wout_ref[ATTN_WIDTH:D_MODEL, :], preferred_element_type=F32))
    x3 = _swiglu_half_step(x2, nf_ref, wg_ref, wu_ref, wd_ref, hn_ref, acc_ref)
    y_ref[...] = _rmsnorm(x3, nfin_ref[...])


def _out_ffn(x1, o, s, wout, nf, wg, wu, wd, nfin, *, batch, head_major):
    n = x1.shape[0]
    tm = min(TOKEN_BLOCK, n)
    nb = n // tm
    per_batch = nb // batch
    tok = lambda w: pl.BlockSpec((tm, w), lambda i: (i, 0))
    if head_major:
        o_spec = pl.BlockSpec((1, N_HEADS, tm, V_DIM), lambda i: (i // per_batch, 0, i % per_batch, 0))
    else:
        o_spec = tok(ATTN_WIDTH)
    return pl.pallas_call(
        functools.partial(_out_ffn_kernel, head_major=head_major),
        grid=(nb,),
        in_specs=[
            tok(D_MODEL), o_spec, tok(GMLP_WIDTH),
            _const_spec(wout.shape),
            _const_spec((1, D_MODEL)),
            *_gate_up_specs(), _const_spec(wd.shape),
            _const_spec((1, D_MODEL)),
        ],
        out_specs=tok(D_MODEL),
        out_shape=jax.ShapeDtypeStruct((n, D_MODEL), F32),
        scratch_shapes=[pltpu.VMEM((tm, D_MODEL), BF16), pltpu.VMEM((tm, D_MODEL), F32)],
        compiler_params=pltpu.CompilerParams(dimension_semantics=("arbitrary",),
                                             vmem_limit_bytes=VMEM_LIMIT),
        name="out_ffn_prompt" if head_major else "out_ffn_sample",
    )(x1, o, s, wout, nf, wg, wu, wd, nfin)


def _attn_kernel(qi_ref, ki_ref, pt_ref, grp_ref, seq_of_ref, qtm_ref, kb_ref, vt_ref, e_ref, b_ref,
                 small_ref, seq_ref, kt_hbm, v_hbm, o_ref, os_ref, m_ref, acc_ref, ms_ref, ls_ref, accs_ref,
                 kbuf, vbuf, page_sem, os_keep, *, n_sample_steps, groups_per_seq):
    step = pl.program_id(1)
    qi = qi_ref[step]
    ki = ki_ref[step]
    g_raw = grp_ref[pl.program_id(0) * pl.num_programs(1) + step]
    has_group = g_raw >= 0
    g = jnp.maximum(g_raw, 0)
    slot = g % 2

    def group_copies(group, buf_slot):
        seq = group // groups_per_seq
        first_page = (group % groups_per_seq) * PAGES_PER_STEP
        copies = []
        for i in range(PAGES_PER_STEP):
            page = pt_ref[seq, first_page + i]
            copies.append(pltpu.make_async_copy(kt_hbm.at[page], kbuf.at[buf_slot, i], page_sem.at[buf_slot]))
            copies.append(pltpu.make_async_copy(v_hbm.at[page], vbuf.at[buf_slot, i], page_sem.at[buf_slot]))
        return copies

    @pl.when(g_raw == 0)
    def _():
        for cp in group_copies(0, 0):
            cp.start()

    @pl.when(has_group & (g_raw + 1 < n_sample_steps))
    def _():
        for cp in group_copies(g + 1, 1 - slot):
            cp.start()

    @pl.when(has_group)
    def _():
        for cp in group_copies(g, slot):
            cp.wait()

    def sample_step():
        n_tok = seq_ref.shape[1] // (N_MAPS + 2)
        n_q = N_MAPS * n_tok
        head_rows = 2 * n_tok
        valid = has_group
        j = g % groups_per_seq
        first = j == 0
        last = j == groups_per_seq - 1

        qbd = seq_ref[0, 0:n_q].astype(BF16)
        blocks = []
        for i in range(0, PAGES_PER_STEP, 2):
            keys = jnp.concatenate([kbuf[slot, i].astype(BF16), kbuf[slot, i + 1].astype(BF16)], axis=1)
            blocks.append(jnp.dot(qbd, keys, preferred_element_type=F32))
            yield
        bl = jnp.where(last, b_ref[0], 0.0)
        blocks[-1] = blocks[-1] + jnp.concatenate([jnp.zeros_like(bl), bl], axis=1)
        pad = jnp.zeros((CHUNK - n_tok, ATTN_WIDTH), F32)
        k_new = jnp.concatenate([seq_ref[0, n_q:n_q + n_tok], pad], axis=0).astype(BF16)
        v_new = jnp.concatenate([seq_ref[0, n_q + n_tok:n_q + 2 * n_tok], pad], axis=0)
        s_new = lax.dot_general(qbd, k_new, (((1,), (1,)), ((), ())), preferred_element_type=F32)
        blocks.append(s_new + jnp.where(last, b_ref[1], NEG))
        s = jnp.concatenate(blocks, axis=1) + jnp.where(valid, 0.0, NEG)

        m_prev = jnp.where(first, NEG, ms_ref[...])
        l_prev = jnp.where(first, 0.0, ls_ref[...])
        acc_prev = jnp.where(first, 0.0, accs_ref[...])
        m_new = jnp.maximum(m_prev, jnp.max(s, axis=1, keepdims=True))
        yield
        alpha = jnp.exp2(m_prev - m_new)
        p = jnp.exp2(s - m_new)
        l_new = alpha * l_prev + jnp.sum(p, axis=1, keepdims=True)
        yield
        pv = None
        pb = p.astype(BF16)
        for i in range(PAGES_PER_STEP + 1):
            if i < PAGES_PER_STEP:
                vals = jnp.concatenate([vbuf[slot, i, pl.ds(hd, CHUNK, stride=N_HEADS), :]
                                        for hd in range(N_HEADS)], axis=1)
            else:
                vals = v_new
            d = jnp.dot(pb[:, i * CHUNK:(i + 1) * CHUNK], vals.astype(BF16),
                        preferred_element_type=F32)
            pv = d if pv is None else pv + d
            if i % 3 == 2:
                yield
        acc_new = alpha * acc_prev + pv
        ms_ref[...] = m_new
        ls_ref[...] = l_new
        accs_ref[...] = acc_new

        lam = _lam(small_ref)
        o_all = acc_new * (1.0 / l_new)
        for hd in range(N_HEADS):
            r1 = slice(hd * head_rows, hd * head_rows + n_tok)
            r2 = slice(hd * head_rows + n_tok, (hd + 1) * head_rows)
            lanes = slice(hd * V_DIM, (hd + 1) * V_DIM)
            o = o_all[r1, lanes] - lam * o_all[r2, lanes]
            out = (_rmsnorm(o, small_ref[0:1, :]) * (1.0 - LAM_INIT)).astype(BF16)
            os_ref[0, :, lanes] = out
            os_keep[:, lanes] = out

    @pl.when(ki == 0)
    def _():
        m_ref[...] = jnp.full_like(m_ref, NEG)
        acc_ref[...] = jnp.zeros_like(acc_ref)


    def unit_logits(unit):
        hd, mp, q_lo, q_n, key_blocks = unit
        idx = 2 * hd + mp
        qm = qtm_ref[0, hd, mp, :, q_lo:q_lo + q_n]
        logits = []
        for k_lo, k_n, bias_kind in key_blocks:
            s = jnp.dot(kb_ref[0, hd, k_lo:k_lo + k_n, :], qm, preferred_element_type=F32)
            if bias_kind is not None:
                s = s + e_ref[bias_kind, idx]
            logits.append(s)
        return logits

    def unit_update(unit, logits):
        hd, mp, q_lo, q_n, key_blocks = unit
        idx = 2 * hd + mp
        cols = slice(q_lo, q_lo + q_n)
        m_prev = m_ref[idx, :, cols]
        m_new = m_prev
        for s in logits:
            m_new = jnp.maximum(m_new, jnp.max(s, axis=0, keepdims=True))
        alpha = jnp.exp2(m_prev - m_new)
        pv = None
        for (k_lo, k_n, _), s in zip(key_blocks, logits):
            p = jnp.exp2(s - m_new)
            d = jnp.dot(vt_ref[0, hd, :, k_lo:k_lo + k_n], p.astype(BF16), preferred_element_type=F32)
            pv = d if pv is None else pv + d
        acc_ref[idx, :, cols] = alpha * acc_ref[idx, :, cols] + pv
        m_ref[idx, :, cols] = m_new

    def run(units, with_sample):
        side = sample_step() if with_sample else iter(())
        if not with_sample:
            os_ref[0] = os_keep[...]
        pending = []
        for n, unit in enumerate(units):
            pending.append((unit, unit_logits(unit)))
            if n % 2 == 0:
                next(side, None)
            if len(pending) > PIPE_DEPTH:
                unit_update(*pending.pop(0))
        for item in pending:
            unit_update(*item)
            next(side, None)
        for _ in side:
            pass

    maps = [(hd, mp) for hd in range(N_HEADS) for mp in range(2)]

    far = lambda hd, mp, lo: [(hd, mp, lo, SUB, [(0, TK, None)]), (hd, mp, lo + SUB, SUB, [(0, TK, None)])]
    near = lambda hd, mp, lo: [(hd, mp, lo, SUB, [(0, SUB, None), (SUB, SUB, 1)]),
                               (hd, mp, lo + SUB, SUB, [(0, TK, None)])]
    diag = lambda hd, mp, lo: [(hd, mp, lo, SUB, [(0, SUB, 0)]),
                               (hd, mp, lo + SUB, SUB, [(0, SUB, 1), (SUB, SUB, 0)])]
    parts = TQ // TK
    first_key_tile = qi * parts

    def kinds_for(offset):
        kinds = []
        for r in range(parts):
            rel = r - offset
            kinds.append(far if rel >= 2 else near if rel == 1 else diag if rel == 0 else None)
        return kinds

    def units_for(kinds):
        return [u for hd, mp in maps for r, kind in enumerate(kinds) if kind is not None
                for u in kind(hd, mp, r * TK)]

    @pl.when(ki < first_key_tile - 1)
    def _():
        run(units_for([far] * parts), with_sample=True)

    for offset in range(-1, parts):
        @pl.when(ki == first_key_tile + offset)
        def _(offset=offset):
            run(units_for(kinds_for(offset)), with_sample=offset <= 0)

    @pl.when(ki == first_key_tile + parts - 1)
    def _():
        lam = _lam(small_ref)
        for hd in range(N_HEADS):
            o1 = acc_ref[2 * hd, 0:V_DIM] * (1.0 / acc_ref[2 * hd, V_DIM:V_DIM + 1])
            o2 = acc_ref[2 * hd + 1, 0:V_DIM] * (1.0 / acc_ref[2 * hd + 1, V_DIM:V_DIM + 1])
            o = o1 - lam * o2
            inv = lax.rsqrt(jnp.mean(o * o, axis=0, keepdims=True) + EPS)
            on = (o * inv).T * small_ref[0:1, :] * (1.0 - LAM_INIT)
            o_ref[0, hd] = on.astype(BF16)


def _attention(qtm, kb, vt, e_tiles, b_tiles, small, page_table, seq_pack, cache_kt, cache_v):
    batch, _, _, _, t = qtm.shape
    nq = t // TQ
    pairs = [(q, k) for q in range(nq) for k in range((q + 1) * (TQ // TK))]
    n_steps = len(pairs)
    qi_arr = jnp.asarray([p[0] for p in pairs], jnp.int32)
    ki_arr = jnp.asarray([p[1] for p in pairs], jnp.int32)
    n_seq = seq_pack.shape[0]
    n_tok = seq_pack.shape[1] // (N_MAPS + 2)
    groups_per_seq = page_table.shape[1] // PAGES_PER_STEP
    n_sample_steps = n_seq * groups_per_seq
    n_rows = N_MAPS * n_tok
    grp, seq_of, nxt = [], [], 0
    for _ in range(batch):
        for q, k in pairs:
            takes = k - q * (TQ // TK) <= 0 and nxt < n_sample_steps
            grp.append(nxt if takes else -1)
            nxt += takes
            seq_of.append(max(nxt - 1, 0) // groups_per_seq)
    assert page_table.shape[1] % PAGES_PER_STEP == 0 and nxt == n_sample_steps
    grp_arr = jnp.asarray(grp, jnp.int32)
    seq_arr = jnp.asarray(seq_of, jnp.int32)

    def seq_spec(rows):
        return pl.BlockSpec((1, rows, ATTN_WIDTH),
                            lambda b, s, qi, ki, pt, grp, seq_of: (seq_of[b * n_steps + s], 0, 0))

    def full(shape, **kw):
        return pl.BlockSpec(shape, lambda b, s, qi, ki, pt, grp, seq_of: (0,) * len(shape), **kw)

    const = lambda a: full(a.shape, pipeline_mode=pl.Buffered(1))
    grid_spec = pltpu.PrefetchScalarGridSpec(
        num_scalar_prefetch=5,
        grid=(batch, n_steps),
        in_specs=([
            pl.BlockSpec((1, N_HEADS, 2, V_DIM, TQ), lambda b, s, qi, ki, pt, grp, seq_of: (b, 0, 0, 0, qi[s])),
            pl.BlockSpec((1, N_HEADS, TK, V_DIM), lambda b, s, qi, ki, pt, grp, seq_of: (b, 0, ki[s], 0)),
            pl.BlockSpec((1, N_HEADS, VT_ROWS, TK), lambda b, s, qi, ki, pt, grp, seq_of: (b, 0, 0, ki[s])),
            const(e_tiles), const(b_tiles), const(small), seq_spec(seq_pack.shape[1]),
            pl.BlockSpec(memory_space=pl.ANY), pl.BlockSpec(memory_space=pl.ANY)]),
        out_specs=[
            pl.BlockSpec((1, N_HEADS, TQ, V_DIM), lambda b, s, qi, ki, pt, grp, seq_of: (b, 0, qi[s], 0)),
            seq_spec(n_tok),
        ],
        scratch_shapes=[
            pltpu.VMEM((N_MAPS, 1, TQ), F32),
            pltpu.VMEM((N_MAPS, VT_ROWS, TQ), F32),
            pltpu.VMEM((n_rows, 1), F32),
            pltpu.VMEM((n_rows, 1), F32),
            pltpu.VMEM((n_rows, ATTN_WIDTH), F32),
            pltpu.VMEM((2, PAGES_PER_STEP, ATTN_WIDTH, CHUNK), F32),
            pltpu.VMEM((2, PAGES_PER_STEP, ATTN_WIDTH, CHUNK), F32),
            pltpu.SemaphoreType.DMA((2,)),
            pltpu.VMEM((n_tok, ATTN_WIDTH), BF16),
        ],
    )
    return pl.pallas_call(
        functools.partial(_attn_kernel, n_sample_steps=n_sample_steps, groups_per_seq=groups_per_seq),
        grid_spec=grid_spec,
        out_shape=[
            jax.ShapeDtypeStruct((batch, N_HEADS, t, V_DIM), BF16),
            jax.ShapeDtypeStruct((n_seq, n_tok, ATTN_WIDTH), BF16),
        ],
        compiler_params=pltpu.CompilerParams(dimension_semantics=("arbitrary", "arbitrary"),
                                             vmem_limit_bytes=VMEM_LIMIT),
        name="attention",
    )(qi_arr, ki_arr, page_table, grp_arr, seq_arr, qtm, kb, vt, e_tiles, b_tiles, small, seq_pack,
      cache_kt, cache_v)


def _ffn_weights(w_gu, w_down):
    w_gu = w_gu.astype(BF16)
    return w_gu, w_gu, w_down.astype(BF16)


def kernel(x_prompt, x_sample, cache_k, cache_v, page_table, rel_table, norm_ffn1, w_ffn1_gu, w_ffn1_down, norm_mix, w_in, lambda_q1, lambda_k1, lambda_q2, lambda_k2, subln, gmlp_norm, w_spatial, b_spatial, w_out, norm_ffn2, w_ffn2_gu, w_ffn2_down, norm_final):
    depth = cache_k.shape[0]
    assert depth == 1
    batch, t_p, _ = x_prompt.shape
    n_seq, t_s, _ = x_sample.shape
    n_pool, page = cache_k.shape[1], cache_k.shape[2]
    assert page == CHUNK and CHUNK % t_s == 0 and (n_seq * t_s) % CHUNK == 0

    row = lambda a: a.reshape(1, -1)
    wg1, wu1, wd1 = _ffn_weights(w_ffn1_gu[0], w_ffn1_down[0])
    wg2, wu2, wd2 = _ffn_weights(w_ffn2_gu[0], w_ffn2_down[0])
    win = w_in[0].astype(BF16)
    wout = w_out[0].astype(BF16)
    lamv = jnp.stack([lambda_q1[0], lambda_k1[0], lambda_q2[0], lambda_k2[0]])
    small = jnp.concatenate([
        row(subln[0]), jnp.pad(lamv, ((0, 0), (0, V_DIM - HEAD_DIM))), jnp.zeros((3, V_DIM), F32)])
    reps = CHUNK // t_s
    ws_sample = jnp.tile(w_spatial[0][:, :t_s, :t_s], (1, reps, reps))
    bs_sample = jnp.tile(b_spatial[0][:, :t_s], (1, reps)).T
    ffn1 = (row(norm_ffn1[0]), wg1, wu1, wd1)
    mix = (row(norm_mix[0]), win, row(gmlp_norm[0]))

    e_tiles, b_tiles = _bias_tables(rel_table, t_s)

    x1p, kp, vp, sp, qtm, kb, vt = _ffn_in(
        x_prompt.reshape(batch * t_p, D_MODEL), *ffn1, *mix, w_spatial[0], b_spatial[0].T,
        batch=batch, seq_in_chunk=CHUNK, prompt=True)
    x1s, ks, vs, ss, qs, gvs = _ffn_in(
        x_sample.reshape(n_seq * t_s, D_MODEL), *ffn1, *mix, ws_sample, bs_sample,
        batch=1, seq_in_chunk=t_s, prompt=False)

    op, os_ = _attention(
        qtm, kb, vt, e_tiles, b_tiles, small, page_table, qs,
        jnp.transpose(cache_k[0], (0, 2, 3, 1)).reshape(n_pool, ATTN_WIDTH, page),
        cache_v[0].reshape(n_pool, page * N_HEADS, V_DIM))

    ffn2 = (wout, row(norm_ffn2[0]), wg2, wu2, wd2, row(norm_final))
    yp = _out_ffn(x1p, op, sp, *ffn2, batch=batch, head_major=True)
    ys = _out_ffn(x1s, os_.reshape(n_seq * t_s, ATTN_WIDTH), ss, *ffn2, batch=1, head_major=False)

    return (
        yp.reshape(batch, t_p, D_MODEL),
        ys.reshape(n_seq, t_s, D_MODEL),
        jnp.transpose(kp.reshape(depth, batch, N_MAPS, HEAD_DIM, t_p), (0, 1, 4, 2, 3)),
        vp.reshape(depth, batch, t_p, N_HEADS, V_DIM),
        ks.reshape(depth, n_seq, t_s, N_MAPS, HEAD_DIM),
        vs.reshape(depth, n_seq, t_s, N_HEADS, V_DIM),
        gvs.reshape(depth, n_seq, t_s, GMLP_WIDTH),
    )
```

```python
import functools
import math

import jax
import jax.numpy as jnp
from jax import lax
from jax.experimental import pallas as pl
from jax.experimental.pallas import tpu as pltpu

F32 = jnp.float32
BF16 = jnp.bfloat16

D_MODEL = 1024
D_FF = 2816
FF_CHUNK = 256
N_FF_CHUNKS = D_FF // FF_CHUNK
ATTN_WIDTH = 512
HEAD_DIM = 64
N_HEADS = 4
N_MAPS = 2 * N_HEADS
V_DIM = 2 * HEAD_DIM
GMLP_WIDTH = 512
N_GROUPS = 4
GROUP = GMLP_WIDTH // N_GROUPS
CHUNK = 128
N_BUCKETS = 32
MAX_EXACT = N_BUCKETS // 2
MAX_DISTANCE = 128
EPS = 1e-6
NEG = -1e30
BUCKET_THRESHOLDS = tuple(
    math.ceil(MAX_EXACT * (MAX_DISTANCE / MAX_EXACT) ** (k / (N_BUCKETS - MAX_EXACT)))
    for k in range(1, N_BUCKETS - MAX_EXACT))
LAM_INIT = 0.8 - 0.6 * math.exp(-0.3 * 0)
LOG2E = math.log2(math.e)

TOKEN_BLOCK = 512
TQ = 1024
TK = 512
SUB = 256
PIPE_DEPTH = 6
VT_ROWS = V_DIM + 16
PAGES_PER_STEP = 16
VMEM_LIMIT = 52 * 1024 * 1024


def _const_spec(shape):
    nd = len(shape)
    return pl.BlockSpec(shape, lambda *_: (0,) * nd, pipeline_mode=pl.Buffered(1))


def _gate_up_specs():
    half = lambda j: pl.BlockSpec((D_MODEL, D_FF), lambda *_: (0, j), pipeline_mode=pl.Buffered(1))
    return [half(0), half(1)]


def _rmsnorm(x, g):
    return x * lax.rsqrt(jnp.mean(x * x, axis=-1, keepdims=True) + EPS) * g


def _lam(small_ref):
    a = jnp.sum(small_ref[1:2, :] * small_ref[2:3, :], axis=1, keepdims=True)
    b = jnp.sum(small_ref[3:4, :] * small_ref[4:5, :], axis=1, keepdims=True)
    return jnp.exp(a) - jnp.exp(b) + LAM_INIT


def _bias_of_distance(d, tab_ref, m):
    n = jnp.maximum(d, 0)
    large = MAX_EXACT
    for thr in BUCKET_THRESHOLDS:
        large = large + (n >= thr).astype(jnp.int32)
    bucket = jnp.where(n < MAX_EXACT, n, large)
    val = jnp.zeros(d.shape, F32)
    for b in range(N_BUCKETS):
        val = jnp.where(bucket == b, tab_ref[b, m], val)
    val = (val - tab_ref[N_BUCKETS - 1, m]) * LOG2E
    return jnp.where(d >= 0, val, NEG)


def _bias_kernel(tab_ref, e_ref, b_ref, *, n_tok):
    m = pl.program_id(0)
    key = lax.broadcasted_iota(jnp.int32, (SUB, SUB), 0)
    qry = lax.broadcasted_iota(jnp.int32, (SUB, SUB), 1)
    e_ref[0, 0] = _bias_of_distance(qry - key, tab_ref, m)
    e_ref[1, 0] = _bias_of_distance(SUB + qry - key, tab_ref, m)
    t = lax.broadcasted_iota(jnp.int32, (n_tok, CHUNK), 0)
    c = lax.broadcasted_iota(jnp.int32, (n_tok, CHUNK), 1)
    b_ref[0] = _bias_of_distance(CHUNK + t - c, tab_ref, m)
    b_ref[1] = _bias_of_distance(t - c, tab_ref, m)


def _bias_tables(rel_table, n_tok):
    return pl.pallas_call(
        functools.partial(_bias_kernel, n_tok=n_tok),
        grid=(N_MAPS,),
        in_specs=[pl.BlockSpec(memory_space=pltpu.SMEM)],
        out_specs=[
            pl.BlockSpec((2, 1, SUB, SUB), lambda m: (0, m, 0, 0)),
            pl.BlockSpec((2, n_tok, CHUNK), lambda m: (0, m, 0)),
        ],
        out_shape=[
            jax.ShapeDtypeStruct((2, N_MAPS, SUB, SUB), F32),
            jax.ShapeDtypeStruct((2, N_MAPS * n_tok, CHUNK), F32),
        ],
        name="bias_tables",
    )(rel_table)


def _swiglu_half_step(x, norm_ref, wg_ref, wu_ref, wd_ref, hn_ref, acc_ref):
    hn_ref[...] = _rmsnorm(x, norm_ref[...]).astype(BF16)

    def gate_up(c):
        hn = hn_ref[...]
        cols = slice(c * FF_CHUNK, (c + 1) * FF_CHUNK)
        return (jnp.dot(hn, wg_ref[:, cols], preferred_element_type=F32),
                jnp.dot(hn, wu_ref[:, cols], preferred_element_type=F32))

    def down(c, gate, up):
        act = (jax.nn.silu(gate) * up).astype(BF16)
        d = jnp.dot(act, wd_ref[c * FF_CHUNK:(c + 1) * FF_CHUNK, :], preferred_element_type=F32)
        if c == 0:
            acc_ref[...] = d
        else:
            acc_ref[...] += d

    pending = gate_up(0)
    for c in range(N_FF_CHUNKS):
        nxt = gate_up(c + 1) if c + 1 < N_FF_CHUNKS else None
        down(c, *pending)
        pending = nxt
    return x + 0.5 * acc_ref[...]


def _ffn_in_kernel(x_ref, nf_ref, wg_ref, wu_ref, wd_ref, nm_ref, win_ref, gn_ref, ws_ref, bs_ref,
                   *rest, prompt, seq_in_chunk):
    if prompt:
        x1_ref, k_ref, v_ref, s_ref, qtm_ref, kb_ref, vt_ref, hn_ref, acc_ref = rest
    else:
        x1_ref, k_ref, v_ref, s_ref, q_ref, gv_ref, hn_ref, acc_ref = rest
    tm = x_ref.shape[0]

    x1 = _swiglu_half_step(x_ref[...], nf_ref, wg_ref, wu_ref, wd_ref, hn_ref, acc_ref)
    x1_ref[...] = x1
    h = _rmsnorm(x1, nm_ref[...]).astype(BF16)

    def proj(i):
        return jnp.dot(h, win_ref[:, i * ATTN_WIDTH:(i + 1) * ATTN_WIDTH], preferred_element_type=F32)

    q = proj(0) * (HEAD_DIM ** -0.5 * LOG2E)
    k = proj(1)
    v = proj(2)
    u = jax.nn.gelu(proj(3))
    g = jax.nn.gelu(proj(4))
    k_ref[...] = k.T[None] if prompt else k
    if prompt:
        for hd in range(N_HEADS):
            v_ref[pl.ds(hd, tm, stride=N_HEADS), :] = v[:, hd * V_DIM:(hd + 1) * V_DIM]
    else:
        v_ref[...] = v

    row = lax.broadcasted_iota(jnp.int32, (CHUNK, CHUNK), 0)
    col = lax.broadcasted_iota(jnp.int32, (CHUNK, CHUNK), 1)
    keep = col <= row
    if seq_in_chunk < CHUNK:
        keep = keep & ((row // seq_in_chunk) == (col // seq_in_chunk))
    for grp in range(N_GROUPS):
        lanes = slice(grp * GROUP, (grp + 1) * GROUP)
        gg = g[:, lanes]
        gvn = gg * lax.rsqrt(jnp.mean(gg * gg, axis=-1, keepdims=True) + EPS) * gn_ref[:, lanes]
        if not prompt:
            gv_ref[:, lanes] = gvn
        wm = jnp.where(keep, ws_ref[grp], 0.0).astype(BF16)
        bcol = bs_ref[:, grp:grp + 1]
        for c in range(tm // CHUNK):
            rows = slice(c * CHUNK, (c + 1) * CHUNK)
            mixed = jnp.dot(wm, gvn[rows, :].astype(BF16), preferred_element_type=F32) + bcol
            s_ref[rows, lanes] = (u[rows, lanes] * mixed).astype(BF16)

    if prompt:
        for hd in range(N_HEADS):
            lanes = slice(hd * V_DIM, (hd + 1) * V_DIM)
            qt = q[:, lanes].T
            feat = lax.broadcasted_iota(jnp.int32, qt.shape, 0)
            qtm_ref[0, hd, 0] = jnp.where(feat < HEAD_DIM, qt, 0.0).astype(BF16)
            qtm_ref[0, hd, 1] = jnp.where(feat >= HEAD_DIM, qt, 0.0).astype(BF16)
            kb_ref[0, hd] = k[:, lanes].astype(BF16)
            vt_ref[0, hd, 0:V_DIM] = v[:, lanes].T.astype(BF16)
            ones_row = lax.broadcasted_iota(jnp.int32, (VT_ROWS - V_DIM, tm), 0) == 0
            vt_ref[0, hd, V_DIM:VT_ROWS] = ones_row.astype(F32).astype(BF16)
    else:
        n_tok = seq_in_chunk
        r = lax.broadcasted_iota(jnp.int32, (N_MAPS * n_tok, ATTN_WIDTH), 0)
        c = lax.broadcasted_iota(jnp.int32, (N_MAPS * n_tok, ATTN_WIDTH), 1)
        own = (r // n_tok) == (c // HEAD_DIM)
        n_q = N_MAPS * n_tok
        for sq in range(tm // n_tok):
            tok = slice(sq * n_tok, (sq + 1) * n_tok)
            q_seq = jnp.concatenate([q[tok]] * N_MAPS, axis=0)
            q_ref[sq, 0:n_q] = jnp.where(own, q_seq, 0.0)
            q_ref[sq, n_q:n_q + n_tok] = k[tok]
            q_ref[sq, n_q + n_tok:n_q + 2 * n_tok] = v[tok]


def _ffn_in(x, nf, wg, wu, wd, nm, win, gn, ws, bs, *, batch, seq_in_chunk, prompt):
    n = x.shape[0]
    tm = min(TOKEN_BLOCK, n)
    nb = n // tm
    per_batch = nb // batch
    tok = lambda w: pl.BlockSpec((tm, w), lambda i: (i, 0))
    in_specs = [
        tok(D_MODEL),
        _const_spec((1, D_MODEL)),
        *_gate_up_specs(), _const_spec(wd.shape),
        _const_spec((1, D_MODEL)),
        _const_spec(win.shape),
        _const_spec((1, GMLP_WIDTH)),
        _const_spec(ws.shape), _const_spec(bs.shape),
    ]
    out_specs = [tok(D_MODEL), tok(ATTN_WIDTH), tok(ATTN_WIDTH), tok(GMLP_WIDTH)]
    out_shape = [
        jax.ShapeDtypeStruct((n, D_MODEL), F32),
        jax.ShapeDtypeStruct((n, ATTN_WIDTH), F32),
        jax.ShapeDtypeStruct((n, ATTN_WIDTH), F32),
        jax.ShapeDtypeStruct((n, GMLP_WIDTH), BF16),
    ]
    if prompt:
        t = n // batch
        out_specs[1] = pl.BlockSpec((1, ATTN_WIDTH, tm), lambda i: (i // per_batch, 0, i % per_batch))
        out_shape[1] = jax.ShapeDtypeStruct((batch, ATTN_WIDTH, t), F32)
        out_specs[2] = pl.BlockSpec((tm * N_HEADS, V_DIM), lambda i: (i, 0))
        out_shape[2] = jax.ShapeDtypeStruct((n * N_HEADS, V_DIM), F32)
        out_specs += [
            pl.BlockSpec((1, N_HEADS, 2, V_DIM, tm), lambda i: (i // per_batch, 0, 0, 0, i % per_batch)),
            pl.BlockSpec((1, N_HEADS, tm, V_DIM), lambda i: (i // per_batch, 0, i % per_batch, 0)),
            pl.BlockSpec((1, N_HEADS, VT_ROWS, tm), lambda i: (i // per_batch, 0, 0, i % per_batch)),
        ]
        out_shape += [
            jax.ShapeDtypeStruct((batch, N_HEADS, 2, V_DIM, t), BF16),
            jax.ShapeDtypeStruct((batch, N_HEADS, t, V_DIM), BF16),
            jax.ShapeDtypeStruct((batch, N_HEADS, VT_ROWS, t), BF16),
        ]
    else:
        rows = (N_MAPS + 2) * seq_in_chunk
        out_specs += [pl.BlockSpec((tm // seq_in_chunk, rows, ATTN_WIDTH), lambda i: (i, 0, 0)),
                      tok(GMLP_WIDTH)]
        out_shape += [
            jax.ShapeDtypeStruct((n // seq_in_chunk, rows, ATTN_WIDTH), F32),
            jax.ShapeDtypeStruct((n, GMLP_WIDTH), F32),
        ]
    return pl.pallas_call(
        functools.partial(_ffn_in_kernel, prompt=prompt, seq_in_chunk=seq_in_chunk),
        grid=(nb,),
        in_specs=in_specs,
        out_specs=out_specs,
        out_shape=out_shape,
        scratch_shapes=[pltpu.VMEM((tm, D_MODEL), BF16), pltpu.VMEM((tm, D_MODEL), F32)],
        compiler_params=pltpu.CompilerParams(dimension_semantics=("arbitrary",),
                                             vmem_limit_bytes=VMEM_LIMIT),
        name="ffn_in_prompt" if prompt else "ffn_in_sample",
    )(x, nf, wg, wu, wd, nm, win, gn, ws, bs)


def _out_ffn_kernel(x1_ref, o_ref, s_ref, wout_ref, nf_ref, wg_ref, wu_ref, wd_ref, nfin_ref,
                    y_ref, hn_ref, acc_ref, *, head_major):
    if head_major:
        o = jnp.concatenate([o_ref[0, hd] for hd in range(N_HEADS)], axis=1)
    else:
        o = o_ref[...]
    x2 = (x1_ref[...]
          + jnp.dot(o, wout_ref[0:ATTN_WIDTH, :], preferred_element_type=F32)
          + jnp.dot(s_ref[...], wout_ref[ATTN_WIDTH:D_MODEL, :], preferred_element_type=F32))
    x3 = _swiglu_half_step(x2, nf_ref, wg_ref, wu_ref, wd_ref, hn_ref, acc_ref)
    y_ref[...] = _rmsnorm(x3, nfin_ref[...])


def _out_ffn(x1, o, s, wout, nf, wg, wu, wd, nfin, *, batch, head_major):
    n = x1.shape[0]
    tm = min(TOKEN_BLOCK, n)
    nb = n // tm
    per_batch = nb // batch
    tok = lambda w: pl.BlockSpec((tm, w), lambda i: (i, 0))
    if head_major:
        o_spec = pl.BlockSpec((1, N_HEADS, tm, V_DIM), lambda i: (i // per_batch, 0, i % per_batch, 0))
    else:
        o_spec = tok(ATTN_WIDTH)
    return pl.pallas_call(
        functools.partial(_out_ffn_kernel, head_major=head_major),
        grid=(nb,),
        in_specs=[
            tok(D_MODEL), o_spec, tok(GMLP_WIDTH),
            _const_spec(wout.shape),
            _const_spec((1, D_MODEL)),
            *_gate_up_specs(), _const_spec(wd.shape),
            _const_spec((1, D_MODEL)),
        ],
        out_specs=tok(D_MODEL),
        out_shape=jax.ShapeDtypeStruct((n, D_MODEL), F32),
        scratch_shapes=[pltpu.VMEM((tm, D_MODEL), BF16), pltpu.VMEM((tm, D_MODEL), F32)],
        compiler_params=pltpu.CompilerParams(dimension_semantics=("arbitrary",),
                                             vmem_limit_bytes=VMEM_LIMIT),
        name="out_ffn_prompt" if head_major else "out_ffn_sample",
    )(x1, o, s, wout, nf, wg, wu, wd, nfin)


def _attn_kernel(qi_ref, ki_ref, pt_ref, grp_ref, seq_of_ref, qtm_ref, kb_ref, vt_ref, e_ref, b_ref,
                 small_ref, seq_ref, kt_hbm, v_hbm, o_ref, os_ref, m_ref, acc_ref, ms_ref, ls_ref, accs_ref,
                 kbuf, vbuf, page_sem, *, n_sample_steps, groups_per_seq):
    step = pl.program_id(1)
    qi = qi_ref[step]
    ki = ki_ref[step]
    g_raw = grp_ref[pl.program_id(0) * pl.num_programs(1) + step]
    has_group = g_raw >= 0
    g = jnp.maximum(g_raw, 0)
    slot = g % 2

    def group_copies(group, buf_slot):
        seq = group // groups_per_seq
        first_page = (group % groups_per_seq) * PAGES_PER_STEP
        copies = []
        for i in range(PAGES_PER_STEP):
            page = pt_ref[seq, first_page + i]
            copies.append(pltpu.make_async_copy(kt_hbm.at[page], kbuf.at[buf_slot, i], page_sem.at[buf_slot]))
            copies.append(pltpu.make_async_copy(v_hbm.at[page], vbuf.at[buf_slot, i], page_sem.at[buf_slot]))
        return copies

    @pl.when(g_raw == 0)
    def _():
        for cp in group_copies(0, 0):
            cp.start()
        ms_ref[...] = jnp.full_like(ms_ref, NEG)
        ls_ref[...] = jnp.zeros_like(ls_ref)
        accs_ref[...] = jnp.zeros_like(accs_ref)

    @pl.when(has_group & (g_raw + 1 < n_sample_steps))
    def _():
        for cp in group_copies(g + 1, 1 - slot):
            cp.start()

    @pl.when(has_group)
    def _():
        for cp in group_copies(g, slot):
            cp.wait()

    def sample_step():
        n_tok = seq_ref.shape[1] // (N_MAPS + 2)
        n_q = N_MAPS * n_tok
        head_rows = 2 * n_tok
        valid = has_group
        j = g % groups_per_seq
        first = j == 0
        last = j == groups_per_seq - 1

        qbd = seq_ref[0, 0:n_q].astype(BF16)
        blocks = []
        for i in range(0, PAGES_PER_STEP, 2):
            keys = jnp.concatenate([kbuf[slot, i].astype(BF16), kbuf[slot, i + 1].astype(BF16)], axis=1)
            blocks.append(jnp.dot(qbd, keys, preferred_element_type=F32))
            yield
        bl = jnp.where(last, b_ref[0], 0.0)
        blocks[-1] = blocks[-1] + jnp.concatenate([jnp.zeros_like(bl), bl], axis=1)
        pad = jnp.zeros((CHUNK - n_tok, ATTN_WIDTH), F32)
        k_new = jnp.concatenate([seq_ref[0, n_q:n_q + n_tok], pad], axis=0).astype(BF16)
        v_new = jnp.concatenate([seq_ref[0, n_q + n_tok:n_q + 2 * n_tok], pad], axis=0)
        s_new = lax.dot_general(qbd, k_new, (((1,), (1,)), ((), ())), preferred_element_type=F32)
        blocks.append(s_new + jnp.where(last, b_ref[1], NEG))
        s = jnp.concatenate(blocks, axis=1) + jnp.where(valid, 0.0, NEG)

        m_prev = jnp.where(first, NEG, ms_ref[...])
        l_prev = jnp.where(first, 0.0, ls_ref[...])
        acc_prev = jnp.where(first, 0.0, accs_ref[...])
        m_new = jnp.maximum(m_prev, jnp.max(s, axis=1, keepdims=True))
        yield
        alpha = jnp.exp2(m_prev - m_new)
        p = jnp.exp2(s - m_new)
        l_new = alpha * l_prev + jnp.sum(p, axis=1, keepdims=True)
        yield
        pv = None
        pb = p.astype(BF16)
        for i in range(PAGES_PER_STEP + 1):
            if i < PAGES_PER_STEP:
                vals = jnp.concatenate([vbuf[slot, i, pl.ds(hd, CHUNK, stride=N_HEADS), :]
                                        for hd in range(N_HEADS)], axis=1)
            else:
                vals = v_new
            d = jnp.dot(pb[:, i * CHUNK:(i + 1) * CHUNK], vals.astype(BF16),
                        preferred_element_type=F32)
            pv = d if pv is None else pv + d
            if i % 3 == 2:
                yield
        acc_new = alpha * acc_prev + pv
        ms_ref[...] = m_new
        ls_ref[...] = l_new
        accs_ref[...] = acc_new

        lam = _lam(small_ref)
        o_all = acc_new * (1.0 / l_new)
        for hd in range(N_HEADS):
            r1 = slice(hd * head_rows, hd * head_rows + n_tok)
            r2 = slice(hd * head_rows + n_tok, (hd + 1) * head_rows)
            lanes = slice(hd * V_DIM, (hd + 1) * V_DIM)
            o = o_all[r1, lanes] - lam * o_all[r2, lanes]
            os_ref[0, :, lanes] = (
                _rmsnorm(o, small_ref[0:1, :]) * (1.0 - LAM_INIT)).astype(BF16)

    @pl.when(ki == 0)
    def _():
        m_ref[...] = jnp.full_like(m_ref, NEG)
        acc_ref[...] = jnp.zeros_like(acc_ref)


    def unit_logits(unit):
        hd, mp, q_lo, q_n, key_blocks = unit
        idx = 2 * hd + mp
        qm = qtm_ref[0, hd, mp, :, q_lo:q_lo + q_n]
        logits = []
        for k_lo, k_n, bias_kind in key_blocks:
            s = jnp.dot(kb_ref[0, hd, k_lo:k_lo + k_n, :], qm, preferred_element_type=F32)
            if bias_kind is not None:
                s = s + e_ref[bias_kind, idx]
            logits.append(s)
        return logits

    def unit_update(unit, logits):
        hd, mp, q_lo, q_n, key_blocks = unit
        idx = 2 * hd + mp
        cols = slice(q_lo, q_lo + q_n)
        m_prev = m_ref[idx, :, cols]
        m_new = m_prev
        for s in logits:
            m_new = jnp.maximum(m_new, jnp.max(s, axis=0, keepdims=True))
        alpha = jnp.exp2(m_prev - m_new)
        pv = None
        for (k_lo, k_n, _), s in zip(key_blocks, logits):
            p = jnp.exp2(s - m_new)
            d = jnp.dot(vt_ref[0, hd, :, k_lo:k_lo + k_n], p.astype(BF16), preferred_element_type=F32)
            pv = d if pv is None else pv + d
        acc_ref[idx, :, cols] = alpha * acc_ref[idx, :, cols] + pv
        m_ref[idx, :, cols] = m_new

    def run(units, with_sample):
        side = sample_step() if with_sample else iter(())
        pending = []
        for n, unit in enumerate(units):
            pending.append((unit, unit_logits(unit)))
            if n % 2 == 0:
                next(side, None)
            if len(pending) > PIPE_DEPTH:
                unit_update(*pending.pop(0))
        for item in pending:
            unit_update(*item)
            next(side, None)
        for _ in side:
            pass

    maps = [(hd, mp) for hd in range(N_HEADS) for mp in range(2)]

    far = lambda hd, mp, lo: [(hd, mp, lo, SUB, [(0, TK, None)]), (hd, mp, lo + SUB, SUB, [(0, TK, None)])]
    near = lambda hd, mp, lo: [(hd, mp, lo, SUB, [(0, SUB, None), (SUB, SUB, 1)]),
                               (hd, mp, lo + SUB, SUB, [(0, TK, None)])]
    diag = lambda hd, mp, lo: [(hd, mp, lo, SUB, [(0, SUB, 0)]),
                               (hd, mp, lo + SUB, SUB, [(0, SUB, 1), (SUB, SUB, 0)])]
    parts = TQ // TK
    first_key_tile = qi * parts

    def kinds_for(offset):
        kinds = []
        for r in range(parts):
            rel = r - offset
            kinds.append(far if rel >= 2 else near if rel == 1 else diag if rel == 0 else None)
        return kinds

    def units_for(kinds):
        return [u for hd, mp in maps for r, kind in enumerate(kinds) if kind is not None
                for u in kind(hd, mp, r * TK)]

    @pl.when(ki < first_key_tile - 1)
    def _():
        run(units_for([far] * parts), with_sample=True)

    for offset in range(-1, parts):
        @pl.when(ki == first_key_tile + offset)
        def _(offset=offset):
            run(units_for(kinds_for(offset)), with_sample=offset <= 0)

    @pl.when(ki == first_key_tile + parts - 1)
    def _():
        lam = _lam(small_ref)
        for hd in range(N_HEADS):
            o1 = acc_ref[2 * hd, 0:V_DIM] * (1.0 / acc_ref[2 * hd, V_DIM:V_DIM + 1])
            o2 = acc_ref[2 * hd + 1, 0:V_DIM] * (1.0 / acc_ref[2 * hd + 1, V_DIM:V_DIM + 1])
            o = o1 - lam * o2
            inv = lax.rsqrt(jnp.mean(o * o, axis=0, keepdims=True) + EPS)
            on = (o * inv).T * small_ref[0:1, :] * (1.0 - LAM_INIT)
            o_ref[0, hd] = on.astype(BF16)


def _attention(qtm, kb, vt, e_tiles, b_tiles, small, page_table, seq_pack, cache_kt, cache_v):
    batch, _, _, _, t = qtm.shape
    nq = t // TQ
    pairs = [(q, k) for q in range(nq) for k in range((q + 1) * (TQ // TK))]
    n_steps = len(pairs)
    qi_arr = jnp.asarray([p[0] for p in pairs], jnp.int32)
    ki_arr = jnp.asarray([p[1] for p in pairs], jnp.int32)
    n_seq = seq_pack.shape[0]
    n_tok = seq_pack.shape[1] // (N_MAPS + 2)
    groups_per_seq = page_table.shape[1] // PAGES_PER_STEP
    n_sample_steps = n_seq * groups_per_seq
    n_rows = N_MAPS * n_tok
    grp, seq_of, nxt = [], [], 0
    for _ in range(batch):
        for q, k in pairs:
            takes = k - q * (TQ // TK) <= 0 and nxt < n_sample_steps
            grp.append(nxt if takes else -1)
            nxt += takes
            seq_of.append(max(nxt - 1, 0) // groups_per_seq)
    assert page_table.shape[1] % PAGES_PER_STEP == 0 and nxt == n_sample_steps
    grp_arr = jnp.asarray(grp, jnp.int32)
    seq_arr = jnp.asarray(seq_of, jnp.int32)

    def seq_spec(rows):
        return pl.BlockSpec((1, rows, ATTN_WIDTH),
                            lambda b, s, qi, ki, pt, grp, seq_of: (seq_of[b * n_steps + s], 0, 0))

    def full(shape, **kw):
        return pl.BlockSpec(shape, lambda b, s, qi, ki, pt, grp, seq_of: (0,) * len(shape), **kw)

    const = lambda a: full(a.shape, pipeline_mode=pl.Buffered(1))
    grid_spec = pltpu.PrefetchScalarGridSpec(
        num_scalar_prefetch=5,
        grid=(batch, n_steps),
        in_specs=([
            pl.BlockSpec((1, N_HEADS, 2, V_DIM, TQ), lambda b, s, qi, ki, pt, grp, seq_of: (b, 0, 0, 0, qi[s])),
            pl.BlockSpec((1, N_HEADS, TK, V_DIM), lambda b, s, qi, ki, pt, grp, seq_of: (b, 0, ki[s], 0)),
            pl.BlockSpec((1, N_HEADS, VT_ROWS, TK), lambda b, s, qi, ki, pt, grp, seq_of: (b, 0, 0, ki[s])),
            const(e_tiles), const(b_tiles), const(small), seq_spec(seq_pack.shape[1]),
            pl.BlockSpec(memory_space=pl.ANY), pl.BlockSpec(memory_space=pl.ANY)]),
        out_specs=[
            pl.BlockSpec((1, N_HEADS, TQ, V_DIM), lambda b, s, qi, ki, pt, grp, seq_of: (b, 0, qi[s], 0)),
            seq_spec(n_tok),
        ],
        scratch_shapes=[
            pltpu.VMEM((N_MAPS, 1, TQ), F32),
            pltpu.VMEM((N_MAPS, VT_ROWS, TQ), F32),
            pltpu.VMEM((n_rows, 1), F32),
            pltpu.VMEM((n_rows, 1), F32),
            pltpu.VMEM((n_rows, ATTN_WIDTH), F32),
            pltpu.VMEM((2, PAGES_PER_STEP, ATTN_WIDTH, CHUNK), F32),
            pltpu.VMEM((2, PAGES_PER_STEP, ATTN_WIDTH, CHUNK), F32),
            pltpu.SemaphoreType.DMA((2,)),
        ],
    )
    return pl.pallas_call(
        functools.partial(_attn_kernel, n_sample_steps=n_sample_steps, groups_per_seq=groups_per_seq),
        grid_spec=grid_spec,
        out_shape=[
            jax.ShapeDtypeStruct((batch, N_HEADS, t, V_DIM), BF16),
            jax.ShapeDtypeStruct((n_seq, n_tok, ATTN_WIDTH), BF16),
        ],
        compiler_params=pltpu.CompilerParams(dimension_semantics=("arbitrary", "arbitrary"),
                                             vmem_limit_bytes=VMEM_LIMIT),
        name="attention",
    )(qi_arr, ki_arr, page_table, grp_arr, seq_arr, qtm, kb, vt, e_tiles, b_tiles, small, seq_pack,
      cache_kt, cache_v)


def _ffn_weights(w_gu, w_down):
    w_gu = w_gu.astype(BF16)
    return w_gu, w_gu, w_down.astype(BF16)


def kernel(x_prompt, x_sample, cache_k, cache_v, page_table, rel_table, norm_ffn1, w_ffn1_gu, w_ffn1_down, norm_mix, w_in, lambda_q1, lambda_k1, lambda_q2, lambda_k2, subln, gmlp_norm, w_spatial, b_spatial, w_out, norm_ffn2, w_ffn2_gu, w_ffn2_down, norm_final):
    depth = cache_k.shape[0]
    assert depth == 1
    batch, t_p, _ = x_prompt.shape
    n_seq, t_s, _ = x_sample.shape
    n_pool, page = cache_k.shape[1], cache_k.shape[2]
    assert page == CHUNK and CHUNK % t_s == 0 and (n_seq * t_s) % CHUNK == 0

    row = lambda a: a.reshape(1, -1)
    wg1, wu1, wd1 = _ffn_weights(w_ffn1_gu[0], w_ffn1_down[0])
    wg2, wu2, wd2 = _ffn_weights(w_ffn2_gu[0], w_ffn2_down[0])
    win = w_in[0].astype(BF16)
    wout = w_out[0].astype(BF16)
    lamv = jnp.stack([lambda_q1[0], lambda_k1[0], lambda_q2[0], lambda_k2[0]])
    small = jnp.concatenate([
        row(subln[0]), jnp.pad(lamv, ((0, 0), (0, V_DIM - HEAD_DIM))), jnp.zeros((3, V_DIM), F32)])
    reps = CHUNK // t_s
    ws_sample = jnp.tile(w_spatial[0][:, :t_s, :t_s], (1, reps, reps))
    bs_sample = jnp.tile(b_spatial[0][:, :t_s], (1, reps)).T
    ffn1 = (row(norm_ffn1[0]), wg1, wu1, wd1)
    mix = (row(norm_mix[0]), win, row(gmlp_norm[0]))

    e_tiles, b_tiles = _bias_tables(rel_table, t_s)

    x1p, kp, vp, sp, qtm, kb, vt = _ffn_in(
        x_prompt.reshape(batch * t_p, D_MODEL), *ffn1, *mix, w_spatial[0], b_spatial[0].T,
        batch=batch, seq_in_chunk=CHUNK, prompt=True)
    x1s, ks, vs, ss, qs, gvs = _ffn_in(
        x_sample.reshape(n_seq * t_s, D_MODEL), *ffn1, *mix, ws_sample, bs_sample,
        batch=1, seq_in_chunk=t_s, prompt=False)

    op, os_ = _attention(
        qtm, kb, vt, e_tiles, b_tiles, small, page_table, qs,
        jnp.transpose(cache_k[0], (0, 2, 3, 1)).reshape(n_pool, ATTN_WIDTH, page),
        cache_v[0].reshape(n_pool, page * N_HEADS, V_DIM))

    ffn2 = (wout, row(norm_ffn2[0]), wg2, wu2, wd2, row(norm_final))
    yp = _out_ffn(x1p, op, sp, *ffn2, batch=batch, head_major=True)
    ys = _out_ffn(x1s, os_.reshape(n_seq * t_s, ATTN_WIDTH), ss, *ffn2, batch=1, head_major=False)

    return (
        yp.reshape(batch, t_p, D_MODEL),
        ys.reshape(n_seq, t_s, D_MODEL),
        jnp.transpose(kp.reshape(depth, batch, N_MAPS, HEAD_DIM, t_p), (0, 1, 4, 2, 3)),
        vp.reshape(depth, batch, t_p, N_HEADS, V_DIM),
        ks.reshape(depth, n_seq, t_s, N_MAPS, HEAD_DIM),
        vs.reshape(depth, n_seq, t_s, N_HEADS, V_DIM),
        gvs.reshape(depth, n_seq, t_s, GMLP_WIDTH),
    )
```
